```python
import math
import jax, jax.numpy as jnp
from jax import lax
import numpy as np

D_MODEL = 1024
BATCH = 32
SEQ = 2048
DEPTH = 1

HEAD_DIM = 64
DIL_GROUPS = ((128, 1), (512, 4), (2048, 16))
A_HEADS_PER_GROUP = 4
A_HEADS = A_HEADS_PER_GROUP * len(DIL_GROUPS)
A_OUT = A_HEADS_PER_GROUP * HEAD_DIM
B_Q_HEADS = 8
B_KV_HEADS = 2
B_WINDOW = 128
B_OUT = B_Q_HEADS * HEAD_DIM
BLOCK = 128
N_BUCKETS = 32
MAX_DISTANCE = 2048
TOTAL_HEADS = A_HEADS + B_Q_HEADS
D_FF = 2816
EPS = 1e-6
A_QKV_W = A_HEADS * HEAD_DIM
B_KV_W = B_KV_HEADS * HEAD_DIM
SPLITS = (A_QKV_W, A_QKV_W, A_QKV_W, B_OUT, B_KV_W, B_KV_W, D_MODEL, D_MODEL)
D_IN = sum(SPLITS)

kernel_name = "hybrid_dilated_swa_sink_macaron"


def rms_norm(x, g):
    x32 = x.astype(jnp.float32)
    y = x32 * lax.rsqrt(jnp.mean(x32 * x32, axis=-1, keepdims=True) + EPS)
    return (y * g.astype(jnp.float32)).astype(x.dtype)


def swiglu(x, w_gate, w_up, w_down):
    return (jax.nn.silu(x @ w_gate) * (x @ w_up)) @ w_down


def t5_bucket(dist):
    max_exact = N_BUCKETS // 2
    n = jnp.maximum(dist, 0)
    nf = jnp.maximum(n, 1).astype(jnp.float32)
    large = max_exact + (jnp.log(nf / max_exact) / math.log(MAX_DISTANCE / max_exact)
                         * (N_BUCKETS - max_exact)).astype(jnp.int32)
    large = jnp.minimum(large, N_BUCKETS - 1)
    return jnp.where(n < max_exact, n, large)


def band_geometry(max_steps):
    qi = jnp.arange(BLOCK)[:, None]
    ki = jnp.arange(2 * BLOCK)[None, :]
    dist = qi + BLOCK - ki
    band = (dist >= 0) & (dist <= max_steps)
    return dist, band


def rel_bias_block(table, dist, stride, h0, h1):
    b = table[t5_bucket(dist * stride)][..., h0:h1]
    return jnp.transpose(b, (2, 0, 1))


def banded_attention(q, k, v, bias, max_steps, sinks=None):
    N, L, H, hd = q.shape
    hkv = k.shape[2]
    g = H // hkv
    nb = -(-L // BLOCK)
    lp = nb * BLOCK
    pad = ((0, 0), (0, lp - L), (0, 0), (0, 0))
    qb = jnp.pad(q, pad).reshape(N, nb, BLOCK, hkv, g, hd)
    kb = jnp.pad(k, pad).reshape(N, nb, BLOCK, hkv, hd)
    vb = jnp.pad(v, pad).reshape(N, nb, BLOCK, hkv, hd)

    def with_prev(t):
        prev = jnp.pad(t, ((0, 0), (1, 0), (0, 0), (0, 0), (0, 0)))[:, :-1]
        return jnp.concatenate([prev, t], axis=2)

    kc, vc = with_prev(kb), with_prev(vb)
    _, band = band_geometry(max_steps)
    first = (jnp.arange(nb) == 0)[:, None, None]
    before_start = (jnp.arange(2 * BLOCK) < BLOCK)[None, None, :]
    valid = band[None] & ~(first & before_start)

    logits = jnp.einsum('nbqhgd,nbkhd->nbhgqk', qb, kc).astype(jnp.float32) * (hd ** -0.5)
    logits = logits + bias.astype(jnp.float32).reshape(hkv, g, BLOCK, 2 * BLOCK)
    logits = jnp.where(valid[None, :, None, None], logits, -jnp.inf)
    m = jnp.max(logits, axis=-1)
    if sinks is not None:
        s = sinks.astype(jnp.float32).reshape(hkv, g)[:, :, None]
        m = jnp.maximum(m, s)
    p = jnp.exp(logits - m[..., None])
    denom = jnp.sum(p, axis=-1)
    if sinks is not None:
        denom = denom + jnp.exp(s - m)
    out = jnp.einsum('nbhgqk,nbkhd->nbqhgd', p.astype(vc.dtype), vc)
    out = (out / jnp.transpose(denom, (0, 1, 4, 2, 3))[..., None]).astype(q.dtype)
    out = out.reshape(N, lp, H, hd)[:, :L]
    lse = jnp.transpose(m + jnp.log(denom), (0, 1, 4, 2, 3)).reshape(N, lp, H)[:, :L]
    return out, lse


def fold_residues(t, d):
    B, S, H, hd = t.shape
    t = t.reshape(B, S // d, d, H, hd)
    return jnp.transpose(t, (0, 2, 1, 3, 4)).reshape(B * d, S // d, H, hd)


def unfold_residues(t, B, d):
    Bd, Ld = t.shape[:2]
    rest = t.shape[2:]
    t = t.reshape((B, d, Ld) + rest)
    perm = (0, 2, 1) + tuple(range(3, t.ndim))
    return jnp.transpose(t, perm).reshape((B, d * Ld) + rest)


def dilated_attention(q, k, v, table):
    B, S = q.shape[:2]
    outs, lses = [], []
    for gi, (window, d) in enumerate(DIL_GROUPS):
        h0, h1 = gi * A_HEADS_PER_GROUP, (gi + 1) * A_HEADS_PER_GROUP
        steps = window // d
        dist, _ = band_geometry(steps)
        bias = rel_bias_block(table, dist, d, h0, h1)
        o, lse = banded_attention(fold_residues(q[:, :, h0:h1], d), fold_residues(k[:, :, h0:h1], d),
                                  fold_residues(v[:, :, h0:h1], d), bias, steps)
        outs.append(unfold_residues(o, B, d))
        lses.append(unfold_residues(lse, B, d))
    alpha = jax.nn.softmax(jnp.stack(lses, axis=0), axis=0)
    y = jnp.sum(alpha[..., None] * jnp.stack(outs, axis=0).astype(jnp.float32), axis=0)
    return y.reshape(B, S, A_OUT).astype(q.dtype)


def sliding_window_sink_attention(q, k, v, sinks, table):
    B, S = q.shape[:2]
    dist, _ = band_geometry(B_WINDOW - 1)
    bias = rel_bias_block(table, dist, 1, A_HEADS, TOTAL_HEADS)
    o, _ = banded_attention(q, k, v, bias, B_WINDOW - 1, sinks)
    return o.reshape(B, S, B_OUT)


def hybrid_layer(x, ffn1_norm, ffn1_w_gate, ffn1_w_up, ffn1_w_down, mix_norm, w_in, b_in,
                 w_branch_a, w_branch_b, w_out, sinks, rel_bias,
                 ffn2_norm, ffn2_w_gate, ffn2_w_up, ffn2_w_down):
    B, S, _ = x.shape
    h = x + 0.5 * swiglu(rms_norm(x, ffn1_norm), ffn1_w_gate, ffn1_w_up, ffn1_w_down)
    u = rms_norm(h, mix_norm)
    z = u @ w_in + b_in
    aq, ak, av, bq, bk, bv, ga, gb = jnp.split(z, np.cumsum(SPLITS)[:-1].tolist(), axis=-1)
    heads = lambda t: t.reshape(B, S, -1, HEAD_DIM)
    ya = dilated_attention(heads(aq), heads(ak), heads(av), rel_bias)
    yb = sliding_window_sink_attention(heads(bq), heads(bk), heads(bv), sinks, rel_bias)
    merged = jax.nn.sigmoid(ga) * (ya @ w_branch_a) + jax.nn.sigmoid(gb) * (yb @ w_branch_b)
    h = h + merged @ w_out
    h = h + 0.5 * swiglu(rms_norm(h, ffn2_norm), ffn2_w_gate, ffn2_w_up, ffn2_w_down)
    return h


def setup_inputs(seed: int = 0) -> dict:
    key = jax.random.key(seed)
    ks = jax.random.split(key, 20)
    L = DEPTH

    def w(k, shape, fan_in):
        return jax.random.normal(k, shape, jnp.float32) * (fan_in ** -0.5)

    def gain(k, shape):
        return 1.0 + 0.01 * jax.random.normal(k, shape, jnp.float32)

    return {
        "x": jax.random.normal(ks[0], (BATCH, SEQ, D_MODEL), jnp.float32),
        "ffn1_norm": gain(ks[1], (L, D_MODEL)),
        "ffn1_w_gate": w(ks[2], (L, D_MODEL, D_FF), D_MODEL),
        "ffn1_w_up": w(ks[3], (L, D_MODEL, D_FF), D_MODEL),
        "ffn1_w_down": w(ks[4], (L, D_FF, D_MODEL), D_FF),
        "mix_norm": gain(ks[5], (L, D_MODEL)),
        "w_in": w(ks[6], (L, D_MODEL, D_IN), D_MODEL),
        "b_in": 0.01 * jax.random.normal(ks[7], (L, D_IN), jnp.float32),
        "w_branch_a": w(ks[8], (L, A_OUT, D_MODEL), A_OUT),
        "w_branch_b": w(ks[9], (L, B_OUT, D_MODEL), B_OUT),
        "w_out": w(ks[10], (L, D_MODEL, D_MODEL), D_MODEL),
        "sinks": 0.5 * jax.random.normal(ks[11], (L, B_Q_HEADS), jnp.float32),
        "rel_bias": 0.5 * jax.random.normal(ks[12], (N_BUCKETS, TOTAL_HEADS), jnp.float32),
        "ffn2_norm": gain(ks[13], (L, D_MODEL)),
        "ffn2_w_gate": w(ks[14], (L, D_MODEL, D_FF), D_MODEL),
        "ffn2_w_up": w(ks[15], (L, D_MODEL, D_FF), D_MODEL),
        "ffn2_w_down": w(ks[16], (L, D_FF, D_MODEL), D_FF),
        "final_norm": gain(ks[17], (D_MODEL,)),
    }


def reference(x, ffn1_norm, ffn1_w_gate, ffn1_w_up, ffn1_w_down, mix_norm, w_in, b_in,
              w_branch_a, w_branch_b, w_out, sinks, rel_bias,
              ffn2_norm, ffn2_w_gate, ffn2_w_up, ffn2_w_down, final_norm):
    h = x
    for l in range(DEPTH):
        h = hybrid_layer(h, ffn1_norm[l], ffn1_w_gate[l], ffn1_w_up[l], ffn1_w_down[l], mix_norm[l],
                         w_in[l], b_in[l], w_branch_a[l], w_branch_b[l], w_out[l], sinks[l], rel_bias,
                         ffn2_norm[l], ffn2_w_gate[l], ffn2_w_up[l], ffn2_w_down[l])
    return rms_norm(h, final_norm)
```

```python
import functools
import math

import numpy as np
import jax
import jax.numpy as jnp
from jax import lax
from jax.experimental import pallas as pl
from jax.experimental.pallas import tpu as pltpu

HEAD_DIM = 64
DIL_GROUPS = ((128, 1), (512, 4), (2048, 16))
A_HEADS_PER_GROUP = 4
A_HEADS = A_HEADS_PER_GROUP * len(DIL_GROUPS)
A_OUT = A_HEADS_PER_GROUP * HEAD_DIM
B_Q_HEADS = 8
B_KV_HEADS = 2
B_WINDOW = 128
B_OUT = B_Q_HEADS * HEAD_DIM
B_KV_W = B_KV_HEADS * HEAD_DIM
BLOCK = 128
N_BUCKETS = 32
MAX_DISTANCE = 2048
EPS = 1e-6
GROUP_W = 3 * A_OUT
B_W = B_OUT + 2 * B_KV_W
LANES = 128
MASKED = -1e30

TOKEN_TILE = 256
VMEM_LIMIT = 56 * 1024 * 1024

_BF16 = jnp.bfloat16
_F32 = jnp.float32


def _rms(x, g):
    return (x * lax.rsqrt(jnp.mean(x * x, axis=-1, keepdims=True) + EPS)) * g


def _dot(a, b):
    return jnp.dot(a, b, preferred_element_type=_F32)


def _swiglu(xb, wg_ref, wu_ref, wd_ref):
    g = _dot(xb, wg_ref[...])
    u = _dot(xb, wu_ref[...])
    a = (g * jax.nn.sigmoid(g)) * u
    return _dot(a.astype(_BF16), wd_ref[...])


def _stage1_kernel(x_ref, g1_ref, wg_ref, wu_ref, wd_ref, gmix_ref, wqkv_ref, bqkv_ref,
                   h1_ref, u2_ref, qkv0_ref, qkv1_ref, qkv2_ref, qkvb_ref, zs_ref):
    tm = x_ref.shape[0]
    x = x_ref[...]
    y = _swiglu(_rms(x, g1_ref[...]).astype(_BF16), wg_ref, wu_ref, wd_ref)
    h1 = x + 0.5 * y
    h1_ref[...] = h1
    u2 = _rms(h1, gmix_ref[...]).astype(_BF16)
    u2_ref[...] = u2
    z = _dot(u2, wqkv_ref[...]) + bqkv_ref[...]
    qkv0_ref[...] = z[:, :GROUP_W].astype(_BF16)
    qkvb_ref[...] = z[:, 3 * GROUP_W:].astype(_BF16)
    n_slab = GROUP_W // LANES
    for j in range(2 * n_slab):
        zs_ref[j] = z[:, GROUP_W + j * LANES:GROUP_W + (j + 1) * LANES]
    for gi, out_ref in ((1, qkv1_ref), (2, qkv2_ref)):
        d = DIL_GROUPS[gi][1]
        for j in range(n_slab):
            for r in range(d):
                rows = zs_ref[(gi - 1) * n_slab + j, pl.ds(r, tm // d, stride=d), :]
                out_ref[0, r, :, j * LANES:(j + 1) * LANES] = rows.astype(_BF16)


def _softmax_pv(s, vc, sink_col=None):
    m = jnp.max(s, axis=-1, keepdims=True)
    if sink_col is not None:
        m = jnp.maximum(m, sink_col)
    p = jnp.exp(s - m)
    l = jnp.sum(p, axis=-1, keepdims=True)
    if sink_col is not None:
        l = l + jnp.exp(sink_col - m)
    o = _dot(p.astype(_BF16), vc)
    return o, m, l


def _qk(qm, kc):
    return lax.dot_general(qm, kc, (((1,), (1,)), ((), ())), preferred_element_type=_F32)


def _pair_attend(q2, kc, vc, bias2):
    lane = lax.broadcasted_iota(jnp.int32, q2.shape, 1)
    lo = lane < HEAD_DIM
    zero = jnp.zeros_like(q2)
    qm = jnp.concatenate([jnp.where(lo, q2, zero), jnp.where(lo, zero, q2)], axis=0)
    s = _qk(qm, kc) + bias2.reshape(2 * BLOCK, bias2.shape[-1])
    o, m, l = _softmax_pv(s, vc)
    on = o / l
    lse = m + jnp.log(l)
    lo_f = lax.broadcasted_iota(jnp.int32, (BLOCK, LANES), 1) < HEAD_DIM
    out = jnp.where(lo_f, on[:BLOCK], on[BLOCK:])
    lse2 = jnp.where(lo_f, lse[:BLOCK], lse[BLOCK:])
    return out, lse2


def _stage2_kernel(sinks_ref, qkv0_ref, qkv1_ref, qkv2_ref, qkvb_ref, bias_ref,
                   ya_ref, yb_ref, o1_ref, l1_ref, o2_ref, l2_ref):
    n_pair = A_OUT // LANES
    seq = qkv0_ref.shape[0]

    def group_cols(ref_rows, j):
        q = ref_rows[:, j * LANES:(j + 1) * LANES]
        return q

    d2 = DIL_GROUPS[2][1]

    def g2_body(r, carry):
        for j in range(n_pair):
            q2 = qkv2_ref[0, r, :, j * LANES:(j + 1) * LANES]
            kc = qkv2_ref[0, r, :, A_OUT + j * LANES:A_OUT + (j + 1) * LANES]
            vc = qkv2_ref[0, r, :, 2 * A_OUT + j * LANES:2 * A_OUT + (j + 1) * LANES]
            bias2 = bias_ref[8 + 2 * j:8 + 2 * j + 2, :, BLOCK:]
            out, lse2 = _pair_attend(q2, kc, vc, bias2)
            o2_ref[j, pl.ds(r, BLOCK, stride=d2), :] = out
            l2_ref[j, pl.ds(r, BLOCK, stride=d2), :] = lse2
        return carry

    lax.fori_loop(0, d2, g2_body, 0)

    d1 = DIL_GROUPS[1][1]
    nb1 = seq // d1 // BLOCK

    def g1_block(r, b, first):
        row0 = 0 if first else pl.multiple_of(b * BLOCK, BLOCK)
        k0 = 0 if first else pl.multiple_of((b - 1) * BLOCK, BLOCK)
        kn = BLOCK if first else 2 * BLOCK
        for j in range(n_pair):
            q2 = qkv1_ref[0, r, pl.ds(row0, BLOCK), j * LANES:(j + 1) * LANES]
            kc = qkv1_ref[0, r, pl.ds(k0, kn), A_OUT + j * LANES:A_OUT + (j + 1) * LANES]
            vc = qkv1_ref[0, r, pl.ds(k0, kn), 2 * A_OUT + j * LANES:2 * A_OUT + (j + 1) * LANES]
            if first:
                bias2 = bias_ref[4 + 2 * j:4 + 2 * j + 2, :, BLOCK:]
            else:
                bias2 = bias_ref[4 + 2 * j:4 + 2 * j + 2]
            out, lse2 = _pair_attend(q2, kc, vc, bias2)
            start = row0 * d1 + r
            o1_ref[j, pl.ds(start, BLOCK, stride=d1), :] = out
            l1_ref[j, pl.ds(start, BLOCK, stride=d1), :] = lse2

    def g1_body(r, carry):
        g1_block(r, 0, True)

        def blk(b, c):
            g1_block(r, b, False)
            return c

        return lax.fori_loop(1, nb1, blk, carry)

    lax.fori_loop(0, d1, g1_body, 0)

    lo_f = lax.broadcasted_iota(jnp.int32, (BLOCK, LANES), 1) < HEAD_DIM
    lane_q = lax.broadcasted_iota(jnp.int32, (BLOCK, LANES), 1) < HEAD_DIM

    def d1_block(b, first):
        row0 = 0 if first else pl.multiple_of(b * BLOCK, BLOCK)
        k0 = 0 if first else pl.multiple_of((b - 1) * BLOCK, BLOCK)
        kn = BLOCK if first else 2 * BLOCK
        rows = pl.ds(row0, BLOCK)
        keys = pl.ds(k0, kn)
        for j in range(n_pair):
            q2 = qkv0_ref[rows, j * LANES:(j + 1) * LANES]
            kc = qkv0_ref[keys, A_OUT + j * LANES:A_OUT + (j + 1) * LANES]
            vc = qkv0_ref[keys, 2 * A_OUT + j * LANES:2 * A_OUT + (j + 1) * LANES]
            if first:
                bias2 = bias_ref[2 * j:2 * j + 2, :, BLOCK:]
            else:
                bias2 = bias_ref[2 * j:2 * j + 2]
            o0, l0 = _pair_attend(q2, kc, vc, bias2)
            o1 = o1_ref[j, rows, :]
            l1 = l1_ref[j, rows, :]
            o2 = o2_ref[j, rows, :]
            l2 = l2_ref[j, rows, :]
            top = jnp.maximum(jnp.maximum(l0, l1), l2)
            w0 = jnp.exp(l0 - top)
            w1 = jnp.exp(l1 - top)
            w2 = jnp.exp(l2 - top)
            y = (w0 * o0 + w1 * o1 + w2 * o2) / (w0 + w1 + w2)
            ya_ref[rows, j * LANES:(j + 1) * LANES] = y.astype(_BF16)
        kc = qkvb_ref[keys, B_OUT:B_OUT + B_KV_W]
        vc = qkvb_ref[keys, B_OUT + B_KV_W:B_OUT + 2 * B_KV_W]
        n_qp = B_OUT // LANES
        halves = []
        for half in range(2):
            parts = []
            for j in range(n_qp):
                q2 = qkvb_ref[rows, j * LANES:(j + 1) * LANES]
                zero = jnp.zeros_like(q2)
                parts.append(jnp.where(lane_q, q2, zero) if half == 0 else jnp.where(lane_q, zero, q2))
            qm = jnp.concatenate(parts, axis=0)
            h0 = A_HEADS + n_qp * half
            if first:
                bias4 = bias_ref[h0:h0 + n_qp, :, BLOCK:]
            else:
                bias4 = bias_ref[h0:h0 + n_qp]
            s = _qk(qm, kc) + bias4.reshape(n_qp * BLOCK, kn)
            sink_col = jnp.concatenate(
                [jnp.full((BLOCK, 1), sinks_ref[n_qp * half + j], _F32) for j in range(n_qp)], axis=0)
            o, m, l = _softmax_pv(s, vc, sink_col)
            halves.append(o / l)
        for j in range(n_qp):
            out = jnp.where(lo_f, halves[0][j * BLOCK:(j + 1) * BLOCK], halves[1][j * BLOCK:(j + 1) * BLOCK])
            yb_ref[rows, j * LANES:(j + 1) * LANES] = out.astype(_BF16)

    d1_block(0, True)

    def d1_body(b, carry):
        d1_block(b, False)
        return carry

    lax.fori_loop(1, seq // BLOCK, d1_body, 0)


def _stage3_kernel(h1_ref, u2_ref, ya_ref, yb_ref, wgate_ref, bgate_ref, wa_ref, wb_ref, wo_ref,
                   g2_ref, wg_ref, wu_ref, wd_ref, gf_ref, out_ref):
    d_model = h1_ref.shape[1]
    gates = jax.nn.sigmoid(_dot(u2_ref[...], wgate_ref[...]) + bgate_ref[...])
    pa = _dot(ya_ref[...], wa_ref[...])
    pb = _dot(yb_ref[...], wb_ref[...])
    merged = gates[:, :d_model] * pa + gates[:, d_model:] * pb
    h2 = h1_ref[...] + _dot(merged.astype(_BF16), wo_ref[...])
    y = _swiglu(_rms(h2, g2_ref[...]).astype(_BF16), wg_ref, wu_ref, wd_ref)
    h3 = h2 + 0.5 * y
    out_ref[...] = _rms(h3, gf_ref[...])


def _t5_bucket_const(dist):
    max_exact = N_BUCKETS // 2
    n = jnp.maximum(dist, 0)
    nf = jnp.maximum(n, 1).astype(jnp.float32)
    large = max_exact + (jnp.log(nf / max_exact) / math.log(MAX_DISTANCE / max_exact)
                         * (N_BUCKETS - max_exact)).astype(jnp.int32)
    large = jnp.minimum(large, N_BUCKETS - 1)
    return jnp.where(n < max_exact, n, large)


def _band_bias(rel_bias):
    qi = jnp.arange(BLOCK)[:, None]
    ki = jnp.arange(2 * BLOCK)[None, :]
    dist = qi + BLOCK - ki
    out = []
    specs = [(d, w // d, gi * A_HEADS_PER_GROUP, (gi + 1) * A_HEADS_PER_GROUP)
             for gi, (w, d) in enumerate(DIL_GROUPS)]
    specs.append((1, B_WINDOW - 1, A_HEADS, A_HEADS + B_Q_HEADS))
    for stride, max_steps, h0, h1 in specs:
        b = rel_bias[_t5_bucket_const(dist * stride)][..., h0:h1]
        b = jnp.transpose(b, (2, 0, 1)).astype(_F32)
        valid = (dist >= 0) & (dist <= max_steps)
        out.append(jnp.where(valid[None], b, MASKED))
    return jnp.concatenate(out, axis=0)


def _const_spec(shape):
    nd = len(shape)
    return pl.BlockSpec(shape, lambda *_: (0,) * nd, pipeline_mode=pl.Buffered(1))


def _layer(x, ffn1_norm, ffn1_w_gate, ffn1_w_up, ffn1_w_down, mix_norm, w_in, b_in,
           w_branch_a, w_branch_b, w_out, sinks, rel_bias,
           ffn2_norm, ffn2_w_gate, ffn2_w_up, ffn2_w_down, final_norm):
    batch, seq, d_model = x.shape
    tokens = batch * seq
    tm = TOKEN_TILE
    tiles_per_seq = seq // tm
    d_ff = ffn1_w_gate.shape[1]
    a_w = A_HEADS * HEAD_DIM
    scale = HEAD_DIM ** -0.5

    aq, ak, av = w_in[:, :a_w], w_in[:, a_w:2 * a_w], w_in[:, 2 * a_w:3 * a_w]
    baq, bak, bav = b_in[:a_w], b_in[a_w:2 * a_w], b_in[2 * a_w:3 * a_w]
    off = 3 * a_w
    bq_w, bq_b = w_in[:, off:off + B_OUT], b_in[off:off + B_OUT]
    bkv_w, bkv_b = w_in[:, off + B_OUT:off + B_OUT + 2 * B_KV_W], b_in[off + B_OUT:off + B_OUT + 2 * B_KV_W]
    gate_w, gate_b = w_in[:, off + B_W:], b_in[off + B_W:]
    perm = np.concatenate([np.r_[j * HEAD_DIM:(j + 1) * HEAD_DIM,
                                 (j + 4) * HEAD_DIM:(j + 5) * HEAD_DIM] for j in range(B_Q_HEADS // 2)])
    w_cols, b_cols = [], []
    for gi in range(len(DIL_GROUPS)):
        sl = slice(gi * A_OUT, (gi + 1) * A_OUT)
        w_cols += [aq[:, sl] * scale, ak[:, sl], av[:, sl]]
        b_cols += [baq[sl] * scale, bak[sl], bav[sl]]
    w_cols += [bq_w[:, perm] * scale, bkv_w]
    b_cols += [bq_b[perm] * scale, bkv_b]
    wqkv = jnp.concatenate(w_cols, axis=1).astype(_BF16)
    bqkv = jnp.concatenate(b_cols)[None, :].astype(_F32)
    wb_perm = w_branch_b[perm, :].astype(_BF16)

    row = lambda v: v[None, :].astype(_F32)
    x2 = x.reshape(tokens, d_model)
    params = pltpu.CompilerParams(dimension_semantics=("parallel",), vmem_limit_bytes=VMEM_LIMIT)

    tile = lambda w: pl.BlockSpec((tm, w), lambda i: (i, 0))
    d1, d2 = DIL_GROUPS[1][1], DIL_GROUPS[2][1]
    h1, u2, qkv0, qkv1, qkv2, qkvb = pl.pallas_call(
        _stage1_kernel,
        grid=(tokens // tm,),
        in_specs=[tile(d_model), _const_spec((1, d_model)),
                  _const_spec((d_model, d_ff)), _const_spec((d_model, d_ff)), _const_spec((d_ff, d_model)),
                  _const_spec((1, d_model)), _const_spec((d_model, 3 * GROUP_W + B_W)),
                  _const_spec((1, 3 * GROUP_W + B_W))],
        out_specs=[tile(d_model), tile(d_model), tile(GROUP_W),
                   pl.BlockSpec((1, d1, tm // d1, GROUP_W),
                                lambda i: (i // tiles_per_seq, 0, i % tiles_per_seq, 0)),
                   pl.BlockSpec((1, d2, tm // d2, GROUP_W),
                                lambda i: (i // tiles_per_seq, 0, i % tiles_per_seq, 0)),
                   tile(B_W)],
        out_shape=[jax.ShapeDtypeStruct((tokens, d_model), _F32),
                   jax.ShapeDtypeStruct((tokens, d_model), _BF16),
                   jax.ShapeDtypeStruct((tokens, GROUP_W), _BF16),
                   jax.ShapeDtypeStruct((batch, d1, seq // d1, GROUP_W), _BF16),
                   jax.ShapeDtypeStruct((batch, d2, seq // d2, GROUP_W), _BF16),
                   jax.ShapeDtypeStruct((tokens, B_W), _BF16)],
        scratch_shapes=[pltpu.VMEM((2 * GROUP_W // LANES, tm, LANES), _F32)],
        compiler_params=params,
        name="stage1_ffn_qkv",
    )(x2, row(ffn1_norm), ffn1_w_gate.astype(_BF16), ffn1_w_up.astype(_BF16), ffn1_w_down.astype(_BF16),
      row(mix_norm), wqkv, bqkv)

    bias = _band_bias(rel_bias)
    n_heads = bias.shape[0]
    seq_tile = lambda w: pl.BlockSpec((seq, w), lambda b: (b, 0))
    ya, yb = pl.pallas_call(
        _stage2_kernel,
        grid=(batch,),
        in_specs=[pl.BlockSpec(memory_space=pltpu.SMEM),
                  seq_tile(GROUP_W),
                  pl.BlockSpec((1, d1, seq // d1, GROUP_W), lambda b: (b, 0, 0, 0)),
                  pl.BlockSpec((1, d2, seq // d2, GROUP_W), lambda b: (b, 0, 0, 0)),
                  seq_tile(B_W),
                  _const_spec((n_heads, BLOCK, 2 * BLOCK))],
        out_specs=[seq_tile(A_OUT), seq_tile(B_OUT)],
        out_shape=[jax.ShapeDtypeStruct((tokens, A_OUT), _BF16),
                   jax.ShapeDtypeStruct((tokens, B_OUT), _BF16)],
        scratch_shapes=[pltpu.VMEM((A_OUT // LANES, seq, LANES), _F32) for _ in range(4)],
        compiler_params=params,
        name="stage2_attention",
    )(sinks.astype(_F32), qkv0, qkv1, qkv2, qkvb, bias)

    out = pl.pallas_call(
        _stage3_kernel,
        grid=(tokens // tm,),
        in_specs=[tile(d_model), tile(d_model), tile(A_OUT), tile(B_OUT),
                  _const_spec((d_model, 2 * d_model)), _const_spec((1, 2 * d_model)),
                  _const_spec((A_OUT, d_model)), _const_spec((B_OUT, d_model)), _const_spec((d_model, d_model)),
                  _const_spec((1, d_model)),
                  _const_spec((d_model, d_ff)), _const_spec((d_model, d_ff)), _const_spec((d_ff, d_model)),
                  _const_spec((1, d_model))],
        out_specs=tile(d_model),
        out_shape=jax.ShapeDtypeStruct((tokens, d_model), _F32),
        compiler_params=params,
        name="stage3_merge_ffn",
    )(h1, u2, ya, yb, gate_w.astype(_BF16), row(gate_b), w_branch_a.astype(_BF16), wb_perm,
      w_out.astype(_BF16), row(ffn2_norm), ffn2_w_gate.astype(_BF16), ffn2_w_up.astype(_BF16),
      ffn2_w_down.astype(_BF16), row(final_norm))
    return out.reshape(batch, seq, d_model)


def kernel(x, ffn1_norm, ffn1_w_gate, ffn1_w_up, ffn1_w_down, mix_norm, w_in, b_in, w_branch_a, w_branch_b,
           w_out, sinks, rel_bias, ffn2_norm, ffn2_w_gate, ffn2_w_up, ffn2_w_down, final_norm):
    assert ffn1_norm.shape[0] == 1, "single-layer kernel"
    return _layer(x, ffn1_norm[0], ffn1_w_gate[0], ffn1_w_up[0], ffn1_w_down[0], mix_norm[0], w_in[0], b_in[0],
                  w_branch_a[0], w_branch_b[0], w_out[0], sinks[0], rel_bias,
                  ffn2_norm[0], ffn2_w_gate[0], ffn2_w_up[0], ffn2_w_down[0], final_norm)
```

```python
import math

import numpy as np
import jax
import jax.numpy as jnp
from jax import lax
from jax.experimental import pallas as pl
from jax.experimental.pallas import tpu as pltpu

HEAD_DIM = 64
DIL_GROUPS = ((128, 1), (512, 4), (2048, 16))
A_HEADS_PER_GROUP = 4
A_HEADS = A_HEADS_PER_GROUP * len(DIL_GROUPS)
A_OUT = A_HEADS_PER_GROUP * HEAD_DIM
B_Q_HEADS = 8
B_KV_HEADS = 2
B_WINDOW = 128
B_OUT = B_Q_HEADS * HEAD_DIM
B_KV_W = B_KV_HEADS * HEAD_DIM
BLOCK = 128
N_BUCKETS = 32
MAX_DISTANCE = 2048
EPS = 1e-6
GROUP_W = 3 * A_OUT
B_W = B_OUT + 2 * B_KV_W
LANES = 128
MASKED = -1e30

TOKEN_TILE = 256
VMEM_LIMIT = 56 * 1024 * 1024

_BF16 = jnp.bfloat16
_F32 = jnp.float32


def _rms(x, g):
    return (x * lax.rsqrt(jnp.mean(x * x, axis=-1, keepdims=True) + EPS)) * g


def _dot(a, b):
    return jnp.dot(a, b, preferred_element_type=_F32)


def _swiglu(xb, wg_ref, wu_ref, wd_ref):
    g = _dot(xb, wg_ref[...])
    u = _dot(xb, wu_ref[...])
    a = (g * jax.nn.sigmoid(g)) * u
    return _dot(a.astype(_BF16), wd_ref[...])


def _stage1_kernel(x_ref, g1_ref, wg_ref, wu_ref, wd_ref, gmix_ref, wqkv_ref, bqkv_ref,
                   h1_ref, u2_ref, qkv0_ref, qkv1_ref, qkv2_ref, qkvb_ref, zs_ref):
    tm = x_ref.shape[0]
    x = x_ref[...]
    y = _swiglu(_rms(x, g1_ref[...]).astype(_BF16), wg_ref, wu_ref, wd_ref)
    h1 = x + 0.5 * y
    h1_ref[...] = h1
    u2 = _rms(h1, gmix_ref[...]).astype(_BF16)
    u2_ref[...] = u2
    z = _dot(u2, wqkv_ref[...]) + bqkv_ref[...]
    qkv0_ref[...] = z[:, :GROUP_W].astype(_BF16)
    qkvb_ref[...] = z[:, 3 * GROUP_W:].astype(_BF16)
    n_slab = GROUP_W // LANES
    for j in range(2 * n_slab):
        zs_ref[j] = z[:, GROUP_W + j * LANES:GROUP_W + (j + 1) * LANES]
    for gi, out_ref in ((1, qkv1_ref), (2, qkv2_ref)):
        d = DIL_GROUPS[gi][1]
        for j in range(n_slab):
            for r in range(d):
                rows = zs_ref[(gi - 1) * n_slab + j, pl.ds(r, tm // d, stride=d), :]
                out_ref[0, r, :, j * LANES:(j + 1) * LANES] = rows.astype(_BF16)


KEY_CHUNK = 32
LOOKAHEAD = 2
G2_CLASSES_PER_STEP = 4
NAT_BLOCKS_PER_STEP = 3
N_SLOTS = 18


def _scores(scr, q2a, q2b, half_a, half_b, kc, bias_a, bias_b, sink_row=None):
    s_ref, _, slot = scr
    kn = kc.shape[0]
    lane = lax.broadcasted_iota(jnp.int32, q2a.shape, 1)
    keep = lambda q2, half: jnp.where((lane < HEAD_DIM) if half == 0 else (lane >= HEAD_DIM), q2, jnp.zeros_like(q2))
    qm = jnp.concatenate([keep(q2a, half_a), keep(q2b, half_b)], axis=0)
    s = lax.dot_general(kc, qm, (((1,), (1,)), ((), ())), preferred_element_type=_F32)
    s = s + jnp.concatenate([bias_a, bias_b], axis=1)
    s_ref[slot, :kn, :] = s
    m = jnp.max(s, axis=0, keepdims=True)
    if sink_row is not None:
        m = jnp.maximum(m, sink_row)
    return m


def _softmax_pv(scr, m, vc, sink_row=None):
    s_ref, p_ref, slot = scr
    kn = vc.shape[0]
    l = jnp.zeros_like(m)
    for c in range(kn // KEY_CHUNK):
        rows = slice(c * KEY_CHUNK, (c + 1) * KEY_CHUNK)
        p = jnp.exp(s_ref[slot, rows, :] - m)
        l = l + jnp.sum(p, axis=0, keepdims=True)
        p_ref[slot, rows, :] = p.astype(_BF16)
    if sink_row is not None:
        l = l + jnp.exp(sink_row - m)
    o_t = lax.dot_general(vc, p_ref[slot, :kn, :], (((0,), (0,)), ((), ())), preferred_element_type=_F32)
    return o_t, l


def _pair_finish(o_t, m, l):
    lse = m + jnp.log(l)
    on = jnp.concatenate([o_t[:HEAD_DIM, :BLOCK] / l[:, :BLOCK], o_t[HEAD_DIM:, BLOCK:] / l[:, BLOCK:]], axis=0)
    lb = jnp.concatenate([jnp.broadcast_to(lse[:, :BLOCK], (HEAD_DIM, BLOCK)),
                          jnp.broadcast_to(lse[:, BLOCK:], (HEAD_DIM, BLOCK))], axis=0)
    return on.T, lb.T


def _run_units(units):
    state = {}
    for i in range(len(units) + LOOKAHEAD):
        if i < len(units):
            state[i] = units[i][0]()
        k = i - LOOKAHEAD
        if k >= 0:
            units[k][1](state.pop(k))


def _stage2_kernel(qkv0_ref, qkv1_ref, qkv2_ref, qkvb_ref, bias_ref, sink_ref,
                   ya_ref, yb_ref, o1_ref, l1_ref, o2_ref, l2_ref, s_ref, p_ref):
    n_pair = A_OUT // LANES
    seq = qkv0_ref.shape[0]

    def cols(base, j):
        return slice(base + j * LANES, base + (j + 1) * LANES)

    def bias_pair(h, first):
        if first:
            return bias_ref[h, BLOCK:, :], bias_ref[h + 1, BLOCK:, :]
        return bias_ref[h], bias_ref[h + 1]

    def pair_unit(slot, load, h, first, finish):
        scr = (s_ref, p_ref, slot)

        def first_pass():
            q2 = load(0, True)
            return _scores(scr, q2, q2, 0, 1, load(A_OUT, False), *bias_pair(h, first))

        def second_pass(m):
            o_t, l = _softmax_pv(scr, m, load(2 * A_OUT, False))
            finish(*_pair_finish(o_t, m, l))

        return first_pass, second_pass

    d2 = DIL_GROUPS[2][1]

    def g2_body(it, carry):
        units = []
        for c in range(G2_CLASSES_PER_STEP):
            r = it * G2_CLASSES_PER_STEP + c
            for j in range(n_pair):
                def load(base, is_q, r=r, j=j):
                    return qkv2_ref[0, r, :, cols(base, j)]

                def finish(out, lse2, r=r, j=j):
                    o2_ref[j, pl.ds(r, BLOCK, stride=d2), :] = out
                    l2_ref[j, pl.ds(r, BLOCK, stride=d2), :] = lse2

                units.append(pair_unit(c * n_pair + j, load, 8 + 2 * j, True, finish))
        _run_units(units)
        return carry

    lax.fori_loop(0, d2 // G2_CLASSES_PER_STEP, g2_body, 0)

    d1 = DIL_GROUPS[1][1]
    nb1 = seq // d1 // BLOCK

    def g1_body(r, carry):
        units = []
        for b in range(nb1):
            first = b == 0
            row0 = b * BLOCK
            k0 = 0 if first else (b - 1) * BLOCK
            kn = BLOCK if first else 2 * BLOCK
            for j in range(n_pair):
                def load(base, is_q, j=j, row0=row0, k0=k0, kn=kn):
                    return qkv1_ref[0, r, pl.ds(row0, BLOCK) if is_q else pl.ds(k0, kn), cols(base, j)]

                def finish(out, lse2, j=j, row0=row0):
                    start = row0 * d1 + r
                    o1_ref[j, pl.ds(start, BLOCK, stride=d1), :] = out
                    l1_ref[j, pl.ds(start, BLOCK, stride=d1), :] = lse2

                units.append(pair_unit(b * n_pair + j, load, 4 + 2 * j, first, finish))
        _run_units(units)
        return carry

    lax.fori_loop(0, d1, g1_body, 0)

    n_qp = B_OUT // LANES
    units_per_block = n_pair + n_qp

    def nat_units(b, first, slot0):
        row0 = 0 if first else pl.multiple_of(b * BLOCK, BLOCK)
        k0 = 0 if first else pl.multiple_of((b - 1) * BLOCK, BLOCK)
        kn = BLOCK if first else 2 * BLOCK
        rows = pl.ds(row0, BLOCK)
        keys = pl.ds(k0, kn)
        units = []
        for j in range(n_pair):
            def load(base, is_q, j=j):
                return qkv0_ref[rows if is_q else keys, cols(base, j)]

            def finish(o0, l0, j=j):
                o1 = o1_ref[j, rows, :]
                l1 = l1_ref[j, rows, :]
                o2 = o2_ref[j, rows, :]
                l2 = l2_ref[j, rows, :]
                top = jnp.maximum(jnp.maximum(l0, l1), l2)
                w0 = jnp.exp(l0 - top)
                w1 = jnp.exp(l1 - top)
                w2 = jnp.exp(l2 - top)
                y = (w0 * o0 + w1 * o1 + w2 * o2) / (w0 + w1 + w2)
                ya_ref[rows, cols(0, j)] = y.astype(_BF16)

            units.append(pair_unit(slot0 + j, load, 2 * j, first, finish))
        done = {}
        for half in range(2):
            for u in range(n_qp // 2):
                scr = (s_ref, p_ref, slot0 + n_pair + half * (n_qp // 2) + u)

                def first_pass(half=half, u=u, scr=scr):
                    ha = A_HEADS + n_qp * half + 2 * u
                    sink_row = sink_ref[half, :, 2 * u * BLOCK:(2 * u + 2) * BLOCK]
                    return _scores(scr, qkvb_ref[rows, cols(0, 2 * u)], qkvb_ref[rows, cols(0, 2 * u + 1)], half, half,
                                   qkvb_ref[keys, B_OUT:B_OUT + B_KV_W], *bias_pair(ha, first), sink_row=sink_row)

                def second_pass(m, half=half, u=u, scr=scr):
                    sink_row = sink_ref[half, :, 2 * u * BLOCK:(2 * u + 2) * BLOCK]
                    o_t, l = _softmax_pv(scr, m, qkvb_ref[keys, B_OUT + B_KV_W:B_OUT + 2 * B_KV_W], sink_row=sink_row)
                    done[half, u] = o_t[half * HEAD_DIM:(half + 1) * HEAD_DIM] / l
                    if half == 1:
                        for jj in range(2):
                            j = 2 * u + jj
                            tile_t = jnp.concatenate([done[0, u][:, jj * BLOCK:(jj + 1) * BLOCK],
                                                      done[1, u][:, jj * BLOCK:(jj + 1) * BLOCK]], axis=0)
                            yb_ref[rows, cols(0, j)] = tile_t.T.astype(_BF16)

                units.append((first_pass, second_pass))
        return units

    _run_units(nat_units(0, True, 0))

    def nat_body(it, carry):
        units = []
        for c in range(NAT_BLOCKS_PER_STEP):
            units += nat_units(1 + it * NAT_BLOCKS_PER_STEP + c, False, c * units_per_block)
        _run_units(units)
        return carry

    lax.fori_loop(0, (seq // BLOCK - 1) // NAT_BLOCKS_PER_STEP, nat_body, 0)


def _stage3_kernel(h1_ref, u2_ref, ya_ref, yb_ref, wgate_ref, bgate_ref, wa_ref, wb_ref, wo_ref,
                   g2_ref, wg_ref, wu_ref, wd_ref, gf_ref, out_ref):
    d_model = h1_ref.shape[1]
    gates = jax.nn.sigmoid(_dot(u2_ref[...], wgate_ref[...]) + bgate_ref[...])
    pa = _dot(ya_ref[...], wa_ref[...])
    pb = _dot(yb_ref[...], wb_ref[...])
    merged = gates[:, :d_model] * pa + gates[:, d_model:] * pb
    h2 = h1_ref[...] + _dot(merged.astype(_BF16), wo_ref[...])
    y = _swiglu(_rms(h2, g2_ref[...]).astype(_BF16), wg_ref, wu_ref, wd_ref)
    h3 = h2 + 0.5 * y
    out_ref[...] = _rms(h3, gf_ref[...])


def _t5_bucket_const(dist):
    max_exact = N_BUCKETS // 2
    n = jnp.maximum(dist, 0)
    nf = jnp.maximum(n, 1).astype(jnp.float32)
    large = max_exact + (jnp.log(nf / max_exact) / math.log(MAX_DISTANCE / max_exact)
                         * (N_BUCKETS - max_exact)).astype(jnp.int32)
    large = jnp.minimum(large, N_BUCKETS - 1)
    return jnp.where(n < max_exact, n, large)


def _bias_kernel(table_ref, idx_ref, out_ref):
    h = pl.program_id(0)
    idx = idx_ref[0]
    acc = jnp.full(idx.shape, MASKED, _F32)
    for b in range(N_BUCKETS):
        acc = jnp.where(idx == b, table_ref[b, h], acc)
    out_ref[0] = acc


def _band_bias(rel_bias):
    ki = jnp.arange(2 * BLOCK)[:, None]
    qi = jnp.arange(BLOCK)[None, :]
    dist = qi + BLOCK - ki
    variants = [(d, w // d) for w, d in DIL_GROUPS] + [(1, B_WINDOW - 1)]
    idx = jnp.stack([jnp.where((dist >= 0) & (dist <= max_steps), _t5_bucket_const(dist * stride), -1)
                     for stride, max_steps in variants]).astype(jnp.int32)
    n_heads = rel_bias.shape[1]
    last = len(variants) - 1
    return pl.pallas_call(
        _bias_kernel,
        grid=(n_heads,),
        in_specs=[pl.BlockSpec(memory_space=pltpu.SMEM),
                  pl.BlockSpec((1, 2 * BLOCK, BLOCK),
                               lambda h: (jnp.minimum(h // A_HEADS_PER_GROUP, last), 0, 0))],
        out_specs=pl.BlockSpec((1, 2 * BLOCK, BLOCK), lambda h: (h, 0, 0)),
        out_shape=jax.ShapeDtypeStruct((n_heads, 2 * BLOCK, BLOCK), _F32),
        name="band_bias",
    )(rel_bias.astype(_F32), idx)


def _const_spec(shape):
    nd = len(shape)
    return pl.BlockSpec(shape, lambda *_: (0,) * nd, pipeline_mode=pl.Buffered(1))


def _layer(x, ffn1_norm, ffn1_w_gate, ffn1_w_up, ffn1_w_down, mix_norm, w_in, b_in,
           w_branch_a, w_branch_b, w_out, sinks, rel_bias,
           ffn2_norm, ffn2_w_gate, ffn2_w_up, ffn2_w_down, final_norm):
    batch, seq, d_model = x.shape
    tokens = batch * seq
    tm = TOKEN_TILE
    tiles_per_seq = seq // tm
    d_ff = ffn1_w_gate.shape[1]
    a_w = A_HEADS * HEAD_DIM
    scale = HEAD_DIM ** -0.5

    aq, ak, av = w_in[:, :a_w], w_in[:, a_w:2 * a_w], w_in[:, 2 * a_w:3 * a_w]
    baq, bak, bav = b_in[:a_w], b_in[a_w:2 * a_w], b_in[2 * a_w:3 * a_w]
    off = 3 * a_w
    bq_w, bq_b = w_in[:, off:off + B_OUT], b_in[off:off + B_OUT]
    bkv_w, bkv_b = w_in[:, off + B_OUT:off + B_OUT + 2 * B_KV_W], b_in[off + B_OUT:off + B_OUT + 2 * B_KV_W]
    gate_w, gate_b = w_in[:, off + B_W:], b_in[off + B_W:]
    perm = np.concatenate([np.r_[j * HEAD_DIM:(j + 1) * HEAD_DIM,
                                 (j + 4) * HEAD_DIM:(j + 5) * HEAD_DIM] for j in range(B_Q_HEADS // 2)])
    w_cols, b_cols = [], []
    for gi in range(len(DIL_GROUPS)):
        sl = slice(gi * A_OUT, (gi + 1) * A_OUT)
        w_cols += [aq[:, sl] * scale, ak[:, sl], av[:, sl]]
        b_cols += [baq[sl] * scale, bak[sl], bav[sl]]
    w_cols += [bq_w[:, perm] * scale, bkv_w]
    b_cols += [bq_b[perm] * scale, bkv_b]
    wqkv = jnp.concatenate(w_cols, axis=1).astype(_BF16)
    bqkv = jnp.concatenate(b_cols)[None, :].astype(_F32)
    wb_perm = w_branch_b[perm, :].astype(_BF16)

    row = lambda v: v[None, :].astype(_F32)
    x2 = x.reshape(tokens, d_model)
    params = pltpu.CompilerParams(dimension_semantics=("parallel",), vmem_limit_bytes=VMEM_LIMIT)

    tile = lambda w: pl.BlockSpec((tm, w), lambda i: (i, 0))
    d1, d2 = DIL_GROUPS[1][1], DIL_GROUPS[2][1]
    h1, u2, qkv0, qkv1, qkv2, qkvb = pl.pallas_call(
        _stage1_kernel,
        grid=(tokens // tm,),
        in_specs=[tile(d_model), _const_spec((1, d_model)),
                  _const_spec((d_model, d_ff)), _const_spec((d_model, d_ff)), _const_spec((d_ff, d_model)),
                  _const_spec((1, d_model)), _const_spec((d_model, 3 * GROUP_W + B_W)),
                  _const_spec((1, 3 * GROUP_W + B_W))],
        out_specs=[tile(d_model), tile(d_model), tile(GROUP_W),
                   pl.BlockSpec((1, d1, tm // d1, GROUP_W),
                                lambda i: (i // tiles_per_seq, 0, i % tiles_per_seq, 0)),
                   pl.BlockSpec((1, d2, tm // d2, GROUP_W),
                                lambda i: (i // tiles_per_seq, 0, i % tiles_per_seq, 0)),
                   tile(B_W)],
        out_shape=[jax.ShapeDtypeStruct((tokens, d_model), _F32),
                   jax.ShapeDtypeStruct((tokens, d_model), _BF16),
                   jax.ShapeDtypeStruct((tokens, GROUP_W), _BF16),
                   jax.ShapeDtypeStruct((batch, d1, seq // d1, GROUP_W), _BF16),
                   jax.ShapeDtypeStruct((batch, d2, seq // d2, GROUP_W), _BF16),
                   jax.ShapeDtypeStruct((tokens, B_W), _BF16)],
        scratch_shapes=[pltpu.VMEM((2 * GROUP_W // LANES, tm, LANES), _F32)],
        compiler_params=params,
        name="stage1_ffn_qkv",
    )(x2, row(ffn1_norm), ffn1_w_gate.astype(_BF16), ffn1_w_up.astype(_BF16), ffn1_w_down.astype(_BF16),
      row(mix_norm), wqkv, bqkv)

    bias = _band_bias(rel_bias)
    n_heads = bias.shape[0]
    sink_rows = jnp.repeat(sinks.astype(_F32).reshape(B_KV_HEADS, B_Q_HEADS // B_KV_HEADS), BLOCK,
                           axis=1)[:, None, :]
    seq_tile = lambda w: pl.BlockSpec((seq, w), lambda b: (b, 0))
    ya, yb = pl.pallas_call(
        _stage2_kernel,
        grid=(batch,),
        in_specs=[seq_tile(GROUP_W),
                  pl.BlockSpec((1, d1, seq // d1, GROUP_W), lambda b: (b, 0, 0, 0)),
                  pl.BlockSpec((1, d2, seq // d2, GROUP_W), lambda b: (b, 0, 0, 0)),
                  seq_tile(B_W),
                  _const_spec((n_heads, 2 * BLOCK, BLOCK)),
                  _const_spec(sink_rows.shape)],
        out_specs=[seq_tile(A_OUT), seq_tile(B_OUT)],
        out_shape=[jax.ShapeDtypeStruct((tokens, A_OUT), _BF16),
                   jax.ShapeDtypeStruct((tokens, B_OUT), _BF16)],
        scratch_shapes=[pltpu.VMEM((A_OUT // LANES, seq, LANES), _F32) for _ in range(4)]
        + [pltpu.VMEM((N_SLOTS, 2 * BLOCK, 2 * BLOCK), _F32), pltpu.VMEM((N_SLOTS, 2 * BLOCK, 2 * BLOCK), _BF16)],
        compiler_params=params,
        name="stage2_attention",
    )(qkv0, qkv1, qkv2, qkvb, bias, sink_rows)

    out = pl.pallas_call(
        _stage3_kernel,
        grid=(tokens // tm,),
        in_specs=[tile(d_model), tile(d_model), tile(A_OUT), tile(B_OUT),
                  _const_spec((d_model, 2 * d_model)), _const_spec((1, 2 * d_model)),
                  _const_spec((A_OUT, d_model)), _const_spec((B_OUT, d_model)), _const_spec((d_model, d_model)),
                  _const_spec((1, d_model)),
                  _const_spec((d_model, d_ff)), _const_spec((d_model, d_ff)), _const_spec((d_ff, d_model)),
                  _const_spec((1, d_model))],
        out_specs=tile(d_model),
        out_shape=jax.ShapeDtypeStruct((tokens, d_model), _F32),
        compiler_params=params,
        name="stage3_merge_ffn",
    )(h1, u2, ya, yb, gate_w.astype(_BF16), row(gate_b), w_branch_a.astype(_BF16), wb_perm,
      w_out.astype(_BF16), row(ffn2_norm), ffn2_w_gate.astype(_BF16), ffn2_w_up.astype(_BF16),
      ffn2_w_down.astype(_BF16), row(final_norm))
    return out.reshape(batch, seq, d_model)


def kernel(x, ffn1_norm, ffn1_w_gate, ffn1_w_up, ffn1_w_down, mix_norm, w_in, b_in, w_branch_a, w_branch_b,
           w_out, sinks, rel_bias, ffn2_norm, ffn2_w_gate, ffn2_w_up, ffn2_w_down, final_norm):
    assert ffn1_norm.shape[0] == 1, "single-layer kernel"
    return _layer(x, ffn1_norm[0], ffn1_w_gate[0], ffn1_w_up[0], ffn1_w_down[0], mix_norm[0], w_in[0], b_in[0],
                  w_branch_a[0], w_branch_b[0], w_out[0], sinks[0], rel_bias,
                  ffn2_norm[0], ffn2_w_gate[0], ffn2_w_up[0], ffn2_w_down[0], final_norm)
```

```python
import math

import numpy as np
import jax
import jax.numpy as jnp
from jax import lax
from jax.experimental import pallas as pl
from jax.experimental.pallas import tpu as pltpu

HEAD_DIM = 64
DIL_GROUPS = ((128, 1), (512, 4), (2048, 16))
A_HEADS_PER_GROUP = 4
A_HEADS = A_HEADS_PER_GROUP * len(DIL_GROUPS)
A_OUT = A_HEADS_PER_GROUP * HEAD_DIM
B_Q_HEADS = 8
B_KV_HEADS = 2
B_WINDOW = 128
B_OUT = B_Q_HEADS * HEAD_DIM
B_KV_W = B_KV_HEADS * HEAD_DIM
BLOCK = 128
N_BUCKETS = 32
MAX_DISTANCE = 2048
EPS = 1e-6
GROUP_W = 3 * A_OUT
B_W = B_OUT + 2 * B_KV_W
LANES = 128
SUBLANES = 8
LOG2E = math.log2(math.e)
MASKED = -1e30

TOKEN_TILE = 256
VMEM_LIMIT = 56 * 1024 * 1024

_BF16 = jnp.bfloat16
_F32 = jnp.float32


def _rms(x, g):
    return (x * lax.rsqrt(jnp.mean(x * x, axis=-1, keepdims=True) + EPS)) * g


def _dot(a, b):
    return jnp.dot(a, b, preferred_element_type=_F32)


def _swiglu(xb, wg_ref, wu_ref, wd_ref):
    g = _dot(xb, wg_ref[...])
    u = _dot(xb, wu_ref[...])
    a = (g * jax.nn.sigmoid(g)) * u
    return _dot(a.astype(_BF16), wd_ref[...])


def _stage1_kernel(x_ref, g1_ref, wg_ref, wu_ref, wd_ref, gmix_ref, wqkv_ref, bqkv_ref,
                   h1_ref, u2_ref, qkv0_ref, qkv1_ref, qkv2_ref, qkvb_ref, zs_ref):
    tm = x_ref.shape[0]
    x = x_ref[...]
    y = _swiglu(_rms(x, g1_ref[...]).astype(_BF16), wg_ref, wu_ref, wd_ref)
    h1 = x + 0.5 * y
    h1_ref[...] = h1
    u2 = _rms(h1, gmix_ref[...]).astype(_BF16)
    u2_ref[...] = u2
    z = _dot(u2, wqkv_ref[...]) + bqkv_ref[...]
    qkv0_ref[...] = z[:, :GROUP_W].astype(_BF16)
    qkvb_ref[...] = z[:, 3 * GROUP_W:].astype(_BF16)
    n_slab = GROUP_W // LANES
    for j in range(2 * n_slab):
        zs_ref[j] = z[:, GROUP_W + j * LANES:GROUP_W + (j + 1) * LANES]
    for gi, out_ref in ((1, qkv1_ref), (2, qkv2_ref)):
        d = DIL_GROUPS[gi][1]
        for j in range(n_slab):
            for r in range(d):
                rows = zs_ref[(gi - 1) * n_slab + j, pl.ds(r, tm // d, stride=d), :]
                out_ref[0, r, :, j * LANES:(j + 1) * LANES] = rows.astype(_BF16)


KEY_CHUNK = 32
LOOKAHEAD = 2
G2_CLASSES_PER_STEP = 4
NAT_BLOCKS_PER_STEP = 5
N_SLOTS = 30


def _scores(scr, q2a, q2b, half_a, half_b, kc, bias_a, bias_b, sink_row=None):
    s_ref, _, slot = scr
    kn = kc.shape[0]
    lane = lax.broadcasted_iota(jnp.int32, q2a.shape, 1)
    keep = lambda q2, half: jnp.where((lane < HEAD_DIM) if half == 0 else (lane >= HEAD_DIM), q2, jnp.zeros_like(q2))
    qm = jnp.concatenate([keep(q2a, half_a), keep(q2b, half_b)], axis=0)
    s = lax.dot_general(kc, qm, (((1,), (1,)), ((), ())), preferred_element_type=_F32)
    s = s + jnp.concatenate([bias_a, bias_b], axis=1)
    s_ref[slot, :kn, :] = s
    m = jnp.max(s, axis=0, keepdims=True)
    if sink_row is not None:
        m = jnp.maximum(m, sink_row)
    return m


def _softmax_pv(scr, m, vc, sink_row=None):
    s_ref, p_ref, slot = scr
    kn = vc.shape[0]
    acc = jnp.zeros((SUBLANES, m.shape[1]), _F32)
    for c in range(kn // KEY_CHUNK):
        rows = slice(c * KEY_CHUNK, (c + 1) * KEY_CHUNK)
        p = jnp.exp2(s_ref[slot, rows, :] - m)
        for t in range(KEY_CHUNK // SUBLANES):
            acc = acc + p[t * SUBLANES:(t + 1) * SUBLANES]
        p_ref[slot, rows, :] = p.astype(_BF16)
    l = jnp.sum(acc, axis=0, keepdims=True)
    if sink_row is not None:
        l = l + jnp.exp2(sink_row - m)
    o_t = lax.dot_general(vc, p_ref[slot, :kn, :], (((0,), (0,)), ((), ())), preferred_element_type=_F32)
    return o_t, l


def _pair_finish(o_t, m, l):
    lse = (m + jnp.log2(l)) * (1.0 / LOG2E)
    on = jnp.concatenate([o_t[:HEAD_DIM, :BLOCK] / l[:, :BLOCK], o_t[HEAD_DIM:, BLOCK:] / l[:, BLOCK:]], axis=0)
    lb = jnp.concatenate([jnp.broadcast_to(lse[:, :BLOCK], (HEAD_DIM, BLOCK)),
                          jnp.broadcast_to(lse[:, BLOCK:], (HEAD_DIM, BLOCK))], axis=0)
    return on.T, lb.T


def _run_units(units):
    state = {}
    for i in range(len(units) + LOOKAHEAD):
        if i < len(units):
            state[i] = units[i][0]()
        k = i - LOOKAHEAD
        if k >= 0:
            units[k][1](state.pop(k))


def _stage2_kernel(qkv0_ref, qkv1_ref, qkv2_ref, qkvb_ref, bias_ref, sink_ref,
                   ya_ref, yb_ref, o1_ref, l1_ref, o2_ref, l2_ref, s_ref, p_ref):
    n_pair = A_OUT // LANES
    seq = qkv0_ref.shape[0]

    def cols(base, j):
        return slice(base + j * LANES, base + (j + 1) * LANES)

    def bias_pair(h, first):
        if first:
            return bias_ref[h, BLOCK:, :], bias_ref[h + 1, BLOCK:, :]
        return bias_ref[h], bias_ref[h + 1]

    def pair_unit(slot, load, h, first, finish):
        scr = (s_ref, p_ref, slot)

        def first_pass():
            q2 = load(0, True)
            return _scores(scr, q2, q2, 0, 1, load(A_OUT, False), *bias_pair(h, first))

        def second_pass(m):
            o_t, l = _softmax_pv(scr, m, load(2 * A_OUT, False))
            finish(*_pair_finish(o_t, m, l))

        return first_pass, second_pass

    d2 = DIL_GROUPS[2][1]

    def g2_body(it, carry):
        units = []
        for c in range(G2_CLASSES_PER_STEP):
            r = it * G2_CLASSES_PER_STEP + c
            for j in range(n_pair):
                def load(base, is_q, r=r, j=j):
                    return qkv2_ref[0, r, :, cols(base, j)]

                def finish(out, lse2, r=r, j=j):
                    o2_ref[j, pl.ds(r, BLOCK, stride=d2), :] = out
                    l2_ref[j, pl.ds(r, BLOCK, stride=d2), :] = lse2

                units.append(pair_unit(c * n_pair + j, load, 8 + 2 * j, True, finish))
        _run_units(units)
        return carry

    lax.fori_loop(0, d2 // G2_CLASSES_PER_STEP, g2_body, 0)

    d1 = DIL_GROUPS[1][1]
    nb1 = seq // d1 // BLOCK

    def g1_body(r, carry):
        units = []
        for b in range(nb1):
            first = b == 0
            row0 = b * BLOCK
            k0 = 0 if first else (b - 1) * BLOCK
            kn = BLOCK if first else 2 * BLOCK
            for j in range(n_pair):
                def load(base, is_q, j=j, row0=row0, k0=k0, kn=kn):
                    return qkv1_ref[0, r, pl.ds(row0, BLOCK) if is_q else pl.ds(k0, kn), cols(base, j)]

                def finish(out, lse2, j=j, row0=row0):
                    start = row0 * d1 + r
                    o1_ref[j, pl.ds(start, BLOCK, stride=d1), :] = out
                    l1_ref[j, pl.ds(start, BLOCK, stride=d1), :] = lse2

                units.append(pair_unit(b * n_pair + j, load, 4 + 2 * j, first, finish))
        _run_units(units)
        return carry

    lax.fori_loop(0, d1, g1_body, 0)

    n_qp = B_OUT // LANES
    units_per_block = n_pair + n_qp

    def nat_units(b, first, slot0):
        row0 = 0 if first else pl.multiple_of(b * BLOCK, BLOCK)
        k0 = 0 if first else pl.multiple_of((b - 1) * BLOCK, BLOCK)
        kn = BLOCK if first else 2 * BLOCK
        rows = pl.ds(row0, BLOCK)
        keys = pl.ds(k0, kn)
        units = []
        for j in range(n_pair):
            def load(base, is_q, j=j):
                return qkv0_ref[rows if is_q else keys, cols(base, j)]

            def finish(o0, l0, j=j):
                o1 = o1_ref[j, rows, :]
                l1 = l1_ref[j, rows, :]
                o2 = o2_ref[j, rows, :]
                l2 = l2_ref[j, rows, :]
                top = jnp.maximum(jnp.maximum(l0, l1), l2)
                w0 = jnp.exp(l0 - top)
                w1 = jnp.exp(l1 - top)
                w2 = jnp.exp(l2 - top)
                y = (w0 * o0 + w1 * o1 + w2 * o2) / (w0 + w1 + w2)
                ya_ref[rows, cols(0, j)] = y.astype(_BF16)

            units.append(pair_unit(slot0 + j, load, 2 * j, first, finish))
        done = {}
        for half in range(2):
            for u in range(n_qp // 2):
                scr = (s_ref, p_ref, slot0 + n_pair + half * (n_qp // 2) + u)

                def first_pass(half=half, u=u, scr=scr):
                    ha = A_HEADS + n_qp * half + 2 * u
                    sink_row = sink_ref[half, :, 2 * u * BLOCK:(2 * u + 2) * BLOCK]
                    return _scores(scr, qkvb_ref[rows, cols(0, 2 * u)], qkvb_ref[rows, cols(0, 2 * u + 1)], half, half,
                                   qkvb_ref[keys, B_OUT:B_OUT + B_KV_W], *bias_pair(ha, first), sink_row=sink_row)

                def second_pass(m, half=half, u=u, scr=scr):
                    sink_row = sink_ref[half, :, 2 * u * BLOCK:(2 * u + 2) * BLOCK]
                    o_t, l = _softmax_pv(scr, m, qkvb_ref[keys, B_OUT + B_KV_W:B_OUT + 2 * B_KV_W], sink_row=sink_row)
                    done[half, u] = o_t[half * HEAD_DIM:(half + 1) * HEAD_DIM] / l
                    if half == 1:
                        for jj in range(2):
                            j = 2 * u + jj
                            tile_t = jnp.concatenate([done[0, u][:, jj * BLOCK:(jj + 1) * BLOCK],
                                                      done[1, u][:, jj * BLOCK:(jj + 1) * BLOCK]], axis=0)
                            yb_ref[rows, cols(0, j)] = tile_t.T.astype(_BF16)

                units.append((first_pass, second_pass))
        return units

    _run_units(nat_units(0, True, 0))

    def nat_body(it, carry):
        units = []
        for c in range(NAT_BLOCKS_PER_STEP):
            units += nat_units(1 + it * NAT_BLOCKS_PER_STEP + c, False, c * units_per_block)
        _run_units(units)
        return carry

    lax.fori_loop(0, (seq // BLOCK - 1) // NAT_BLOCKS_PER_STEP, nat_body, 0)


def _stage3_kernel(h1_ref, u2_ref, ya_ref, yb_ref, wgate_ref, bgate_ref, wa_ref, wb_ref, wo_ref,
                   g2_ref, wg_ref, wu_ref, wd_ref, gf_ref, out_ref):
    d_model = h1_ref.shape[1]
    gates = jax.nn.sigmoid(_dot(u2_ref[...], wgate_ref[...]) + bgate_ref[...])
    pa = _dot(ya_ref[...], wa_ref[...])
    pb = _dot(yb_ref[...], wb_ref[...])
    merged = gates[:, :d_model] * pa + gates[:, d_model:] * pb
    h2 = h1_ref[...] + _dot(merged.astype(_BF16), wo_ref[...])
    y = _swiglu(_rms(h2, g2_ref[...]).astype(_BF16), wg_ref, wu_ref, wd_ref)
    h3 = h2 + 0.5 * y
    out_ref[...] = _rms(h3, gf_ref[...])


def _t5_bucket_const(dist):
    max_exact = N_BUCKETS // 2
    n = jnp.maximum(dist, 0)
    nf = jnp.maximum(n, 1).astype(jnp.float32)
    large = max_exact + (jnp.log(nf / max_exact) / math.log(MAX_DISTANCE / max_exact)
                         * (N_BUCKETS - max_exact)).astype(jnp.int32)
    large = jnp.minimum(large, N_BUCKETS - 1)
    return jnp.where(n < max_exact, n, large)


def _bias_kernel(table_ref, idx_ref, out_ref):
    h = pl.program_id(0)
    idx = idx_ref[0]
    acc = jnp.full(idx.shape, MASKED, _F32)
    for b in range(N_BUCKETS):
        acc = jnp.where(idx == b, table_ref[b, h] * LOG2E, acc)
    out_ref[0] = acc


def _band_bias(rel_bias):
    ki = jnp.arange(2 * BLOCK)[:, None]
    qi = jnp.arange(BLOCK)[None, :]
    dist = qi + BLOCK - ki
    variants = [(d, w // d) for w, d in DIL_GROUPS] + [(1, B_WINDOW - 1)]
    idx = jnp.stack([jnp.where((dist >= 0) & (dist <= max_steps), _t5_bucket_const(dist * stride), -1)
                     for stride, max_steps in variants]).astype(jnp.int32)
    n_heads = rel_bias.shape[1]
    last = len(variants) - 1
    return pl.pallas_call(
        _bias_kernel,
        grid=(n_heads,),
        in_specs=[pl.BlockSpec(memory_space=pltpu.SMEM),
                  pl.BlockSpec((1, 2 * BLOCK, BLOCK),
                               lambda h: (jnp.minimum(h // A_HEADS_PER_GROUP, last), 0, 0))],
        out_specs=pl.BlockSpec((1, 2 * BLOCK, BLOCK), lambda h: (h, 0, 0)),
        out_shape=jax.ShapeDtypeStruct((n_heads, 2 * BLOCK, BLOCK), _F32),
        name="band_bias",
    )(rel_bias.astype(_F32), idx)


def _const_spec(shape):
    nd = len(shape)
    return pl.BlockSpec(shape, lambda *_: (0,) * nd, pipeline_mode=pl.Buffered(1))


def _layer(x, ffn1_norm, ffn1_w_gate, ffn1_w_up, ffn1_w_down, mix_norm, w_in, b_in,
           w_branch_a, w_branch_b, w_out, sinks, rel_bias,
           ffn2_norm, ffn2_w_gate, ffn2_w_up, ffn2_w_down, final_norm):
    batch, seq, d_model = x.shape
    tokens = batch * seq
    tm = TOKEN_TILE
    tiles_per_seq = seq // tm
    d_ff = ffn1_w_gate.shape[1]
    a_w = A_HEADS * HEAD_DIM
    scale = HEAD_DIM ** -0.5 * LOG2E

    aq, ak, av = w_in[:, :a_w], w_in[:, a_w:2 * a_w], w_in[:, 2 * a_w:3 * a_w]
    baq, bak, bav = b_in[:a_w], b_in[a_w:2 * a_w], b_in[2 * a_w:3 * a_w]
    off = 3 * a_w
    bq_w, bq_b = w_in[:, off:off + B_OUT], b_in[off:off + B_OUT]
    bkv_w, bkv_b = w_in[:, off + B_OUT:off + B_OUT + 2 * B_KV_W], b_in[off + B_OUT:off + B_OUT + 2 * B_KV_W]
    gate_w, gate_b = w_in[:, off + B_W:], b_in[off + B_W:]
    perm = np.concatenate([np.r_[j * HEAD_DIM:(j + 1) * HEAD_DIM,
                                 (j + 4) * HEAD_DIM:(j + 5) * HEAD_DIM] for j in range(B_Q_HEADS // 2)])
    w_cols, b_cols = [], []
    for gi in range(len(DIL_GROUPS)):
        sl = slice(gi * A_OUT, (gi + 1) * A_OUT)
        w_cols += [aq[:, sl] * scale, ak[:, sl], av[:, sl]]
        b_cols += [baq[sl] * scale, bak[sl], bav[sl]]
    w_cols += [bq_w[:, perm] * scale, bkv_w]
    b_cols += [bq_b[perm] * scale, bkv_b]
    wqkv = jnp.concatenate(w_cols, axis=1).astype(_BF16)
    bqkv = jnp.concatenate(b_cols)[None, :].astype(_F32)
    wb_perm = w_branch_b[perm, :].astype(_BF16)

    row = lambda v: v[None, :].astype(_F32)
    x2 = x.reshape(tokens, d_model)
    params = pltpu.CompilerParams(dimension_semantics=("parallel",), vmem_limit_bytes=VMEM_LIMIT)

    tile = lambda w: pl.BlockSpec((tm, w), lambda i: (i, 0))
    d1, d2 = DIL_GROUPS[1][1], DIL_GROUPS[2][1]
    h1, u2, qkv0, qkv1, qkv2, qkvb = pl.pallas_call(
        _stage1_kernel,
        grid=(tokens // tm,),
        in_specs=[tile(d_model), _const_spec((1, d_model)),
                  _const_spec((d_model, d_ff)), _const_spec((d_model, d_ff)), _const_spec((d_ff, d_model)),
                  _const_spec((1, d_model)), _const_spec((d_model, 3 * GROUP_W + B_W)),
                  _const_spec((1, 3 * GROUP_W + B_W))],
        out_specs=[tile(d_model), tile(d_model), tile(GROUP_W),
                   pl.BlockSpec((1, d1, tm // d1, GROUP_W),
                                lambda i: (i // tiles_per_seq, 0, i % tiles_per_seq, 0)),
                   pl.BlockSpec((1, d2, tm // d2, GROUP_W),
                                lambda i: (i // tiles_per_seq, 0, i % tiles_per_seq, 0)),
                   tile(B_W)],
        out_shape=[jax.ShapeDtypeStruct((tokens, d_model), _F32),
                   jax.ShapeDtypeStruct((tokens, d_model), _BF16),
                   jax.ShapeDtypeStruct((tokens, GROUP_W), _BF16),
                   jax.ShapeDtypeStruct((batch, d1, seq // d1, GROUP_W), _BF16),
                   jax.ShapeDtypeStruct((batch, d2, seq // d2, GROUP_W), _BF16),
                   jax.ShapeDtypeStruct((tokens, B_W), _BF16)],
        scratch_shapes=[pltpu.VMEM((2 * GROUP_W // LANES, tm, LANES), _F32)],
        compiler_params=params,
        name="stage1_ffn_qkv",
    )(x2, row(ffn1_norm), ffn1_w_gate.astype(_BF16), ffn1_w_up.astype(_BF16), ffn1_w_down.astype(_BF16),
      row(mix_norm), wqkv, bqkv)

    bias = _band_bias(rel_bias)
    n_heads = bias.shape[0]
    sink_rows = jnp.repeat((sinks.astype(_F32) * LOG2E).reshape(B_KV_HEADS, B_Q_HEADS // B_KV_HEADS), BLOCK,
                           axis=1)[:, None, :]
    seq_tile = lambda w: pl.BlockSpec((seq, w), lambda b: (b, 0))
    ya, yb = pl.pallas_call(
        _stage2_kernel,
        grid=(batch,),
        in_specs=[seq_tile(GROUP_W),
                  pl.BlockSpec((1, d1, seq // d1, GROUP_W), lambda b: (b, 0, 0, 0)),
                  pl.BlockSpec((1, d2, seq // d2, GROUP_W), lambda b: (b, 0, 0, 0)),
                  seq_tile(B_W),
                  _const_spec((n_heads, 2 * BLOCK, BLOCK)),
                  _const_spec(sink_rows.shape)],
        out_specs=[seq_tile(A_OUT), seq_tile(B_OUT)],
        out_shape=[jax.ShapeDtypeStruct((tokens, A_OUT), _BF16),
                   jax.ShapeDtypeStruct((tokens, B_OUT), _BF16)],
        scratch_shapes=[pltpu.VMEM((A_OUT // LANES, seq, LANES), _F32) for _ in range(4)]
        + [pltpu.VMEM((N_SLOTS, 2 * BLOCK, 2 * BLOCK), _F32), pltpu.VMEM((N_SLOTS, 2 * BLOCK, 2 * BLOCK), _BF16)],
        compiler_params=params,
        name="stage2_attention",
    )(qkv0, qkv1, qkv2, qkvb, bias, sink_rows)

    out = pl.pallas_call(
        _stage3_kernel,
        grid=(tokens // tm,),
        in_specs=[tile(d_model), tile(d_model), tile(A_OUT), tile(B_OUT),
                  _const_spec((d_model, 2 * d_model)), _const_spec((1, 2 * d_model)),
                  _const_spec((A_OUT, d_model)), _const_spec((B_OUT, d_model)), _const_spec((d_model, d_model)),
                  _const_spec((1, d_model)),
                  _const_spec((d_model, d_ff)), _const_spec((d_model, d_ff)), _const_spec((d_ff, d_model)),
                  _const_spec((1, d_model))],
        out_specs=tile(d_model),
        out_shape=jax.ShapeDtypeStruct((tokens, d_model), _F32),
        compiler_params=params,
        name="stage3_merge_ffn",
    )(h1, u2, ya, yb, gate_w.astype(_BF16), row(gate_b), w_branch_a.astype(_BF16), wb_perm,
      w_out.astype(_BF16), row(ffn2_norm), ffn2_w_gate.astype(_BF16), ffn2_w_up.astype(_BF16),
      ffn2_w_down.astype(_BF16), row(final_norm))
    return out.reshape(batch, seq, d_model)


def kernel(x, ffn1_norm, ffn1_w_gate, ffn1_w_up, ffn1_w_down, mix_norm, w_in, b_in, w_branch_a, w_branch_b,
           w_out, sinks, rel_bias, ffn2_norm, ffn2_w_gate, ffn2_w_up, ffn2_w_down, final_norm):
    assert ffn1_norm.shape[0] == 1, "single-layer kernel"
    return _layer(x, ffn1_norm[0], ffn1_w_gate[0], ffn1_w_up[0], ffn1_w_down[0], mix_norm[0], w_in[0], b_in[0],
                  w_branch_a[0], w_branch_b[0], w_out[0], sinks[0], rel_bias,
                  ffn2_norm[0], ffn2_w_gate[0], ffn2_w_up[0], ffn2_w_down[0], final_norm)
```

```python
import math

import numpy as np
import jax
import jax.numpy as jnp
from jax import lax
from jax.experimental import pallas as pl
from jax.experimental.pallas import tpu as pltpu

HEAD_DIM = 64
DIL_GROUPS = ((128, 1), (512, 4), (2048, 16))
A_HEADS_PER_GROUP = 4
A_HEADS = A_HEADS_PER_GROUP * len(DIL_GROUPS)
A_OUT = A_HEADS_PER_GROUP * HEAD_DIM
B_Q_HEADS = 8
B_KV_HEADS = 2
B_WINDOW = 128
B_OUT = B_Q_HEADS * HEAD_DIM
B_KV_W = B_KV_HEADS * HEAD_DIM
BLOCK = 128
N_BUCKETS = 32
MAX_DISTANCE = 2048
EPS = 1e-6
GROUP_W = 3 * A_OUT
B_W = B_OUT + 2 * B_KV_W
LANES = 128
SUBLANES = 8
LOG2E = math.log2(math.e)
MASKED = -1e30

TOKEN_TILE = 512
SUB_TILE = 256
VMEM_LIMIT = 56 * 1024 * 1024

_BF16 = jnp.bfloat16
_F32 = jnp.float32


def _rms(x, g):
    return (x * lax.rsqrt(jnp.mean(x * x, axis=-1, keepdims=True) + EPS)) * g


def _dot(a, b):
    return jnp.dot(a, b, preferred_element_type=_F32)


def _silu_mul(g, u):
    return ((g * jax.nn.sigmoid(g)) * u).astype(_BF16)


def _swiglu_tiles(xbs, wg_ref, wu_ref, wd_ref):
    gu = [(_dot(xb, wg_ref[...]), _dot(xb, wu_ref[...])) for xb in xbs]
    return [_dot(_silu_mul(g, u), wd_ref[...]) for g, u in gu]


def _stage1_kernel(x_ref, g1_ref, wg_ref, wu_ref, wd_ref, gmix_ref, wqkv_ref, bqkv_ref,
                   h1_ref, u2_ref, qkv0_ref, qkv1_ref, qkv2_ref, qkvb_ref, zs_ref):
    subs = [slice(i * SUB_TILE, (i + 1) * SUB_TILE) for i in range(x_ref.shape[0] // SUB_TILE)]
    xs = [x_ref[sl, :] for sl in subs]
    ys = _swiglu_tiles([_rms(x, g1_ref[...]).astype(_BF16) for x in xs], wg_ref, wu_ref, wd_ref)
    u2s = []
    for sl, x, y in zip(subs, xs, ys):
        h1 = x + 0.5 * y
        h1_ref[sl, :] = h1
        u2 = _rms(h1, gmix_ref[...]).astype(_BF16)
        u2_ref[sl, :] = u2
        u2s.append(u2)
    zs = [_dot(u2, wqkv_ref[...]) + bqkv_ref[...] for u2 in u2s]
    n_slab = GROUP_W // LANES
    for i, (sl, z) in enumerate(zip(subs, zs)):
        qkv0_ref[sl, :] = z[:, :GROUP_W].astype(_BF16)
        qkvb_ref[sl, :] = z[:, 3 * GROUP_W:].astype(_BF16)
        for j in range(2 * n_slab):
            zs_ref[i, j] = z[:, GROUP_W + j * LANES:GROUP_W + (j + 1) * LANES]
        for gi, out_ref in ((1, qkv1_ref), (2, qkv2_ref)):
            d = DIL_GROUPS[gi][1]
            n = SUB_TILE // d
            for j in range(n_slab):
                for r in range(d):
                    rows = zs_ref[i, (gi - 1) * n_slab + j, pl.ds(r, n, stride=d), :]
                    out_ref[0, r, i * n:(i + 1) * n, j * LANES:(j + 1) * LANES] = rows.astype(_BF16)


KEY_CHUNK = 32
LOOKAHEAD = 2
G2_CLASSES_PER_STEP = 4
NAT_BLOCKS_PER_STEP = 5
N_SLOTS = 30


def _scores(scr, q2a, q2b, half_a, half_b, kc, bias_a, bias_b, sink_row=None):
    s_ref, _, slot = scr
    kn = kc.shape[0]
    lane = lax.broadcasted_iota(jnp.int32, q2a.shape, 1)
    keep = lambda q2, half: jnp.where((lane < HEAD_DIM) if half == 0 else (lane >= HEAD_DIM), q2, jnp.zeros_like(q2))
    qm = jnp.concatenate([keep(q2a, half_a), keep(q2b, half_b)], axis=0)
    s = lax.dot_general(kc, qm, (((1,), (1,)), ((), ())), preferred_element_type=_F32)
    s = s + jnp.concatenate([bias_a, bias_b], axis=1)
    s_ref[slot, :kn, :] = s
    m = jnp.max(s, axis=0, keepdims=True)
    if sink_row is not None:
        m = jnp.maximum(m, sink_row)
    return m


def _softmax_pv(scr, m, vc, sink_row=None):
    s_ref, p_ref, slot = scr
    kn = vc.shape[0]
    acc = jnp.zeros((SUBLANES, m.shape[1]), _F32)
    for c in range(kn // KEY_CHUNK):
        rows = slice(c * KEY_CHUNK, (c + 1) * KEY_CHUNK)
        p = jnp.exp2(s_ref[slot, rows, :] - m)
        for t in range(KEY_CHUNK // SUBLANES):
            acc = acc + p[t * SUBLANES:(t + 1) * SUBLANES]
        p_ref[slot, rows, :] = p.astype(_BF16)
    l = jnp.sum(acc, axis=0, keepdims=True)
    if sink_row is not None:
        l = l + jnp.exp2(sink_row - m)
    o_t = lax.dot_general(vc, p_ref[slot, :kn, :], (((0,), (0,)), ((), ())), preferred_element_type=_F32)
    return o_t, l


def _pair_finish(o_t, m, l):
    lse = (m + jnp.log2(l)) * (1.0 / LOG2E)
    on = jnp.concatenate([o_t[:HEAD_DIM, :BLOCK] / l[:, :BLOCK], o_t[HEAD_DIM:, BLOCK:] / l[:, BLOCK:]], axis=0)
    lb = jnp.concatenate([jnp.broadcast_to(lse[:, :BLOCK], (HEAD_DIM, BLOCK)),
                          jnp.broadcast_to(lse[:, BLOCK:], (HEAD_DIM, BLOCK))], axis=0)
    return on.T, lb.T


def _run_units(units):
    state = {}
    for i in range(len(units) + LOOKAHEAD):
        if i < len(units):
            state[i] = units[i][0]()
        k = i - LOOKAHEAD
        if k >= 0:
            units[k][1](state.pop(k))


def _stage2_kernel(qkv0_ref, qkv1_ref, qkv2_ref, qkvb_ref, bias_ref, sink_ref,
                   ya_ref, yb_ref, o1_ref, l1_ref, o2_ref, l2_ref, s_ref, p_ref):
    n_pair = A_OUT // LANES
    seq = qkv0_ref.shape[0]

    def cols(base, j):
        return slice(base + j * LANES, base + (j + 1) * LANES)

    def bias_pair(h, first):
        if first:
            return bias_ref[h, BLOCK:, :], bias_ref[h + 1, BLOCK:, :]
        return bias_ref[h], bias_ref[h + 1]

    def pair_unit(slot, load, h, first, finish):
        scr = (s_ref, p_ref, slot)

        def first_pass():
            q2 = load(0, True)
            return _scores(scr, q2, q2, 0, 1, load(A_OUT, False), *bias_pair(h, first))

        def second_pass(m):
            o_t, l = _softmax_pv(scr, m, load(2 * A_OUT, False))
            finish(*_pair_finish(o_t, m, l))

        return first_pass, second_pass

    d2 = DIL_GROUPS[2][1]

    def g2_body(it, carry):
        units = []
        for c in range(G2_CLASSES_PER_STEP):
            r = it * G2_CLASSES_PER_STEP + c
            for j in range(n_pair):
                def load(base, is_q, r=r, j=j):
                    return qkv2_ref[0, r, :, cols(base, j)]

                def finish(out, lse2, r=r, j=j):
                    o2_ref[j, pl.ds(r, BLOCK, stride=d2), :] = out
                    l2_ref[j, pl.ds(r, BLOCK, stride=d2), :] = lse2

                units.append(pair_unit(c * n_pair + j, load, 8 + 2 * j, True, finish))
        _run_units(units)
        return carry

    lax.fori_loop(0, d2 // G2_CLASSES_PER_STEP, g2_body, 0)

    d1 = DIL_GROUPS[1][1]
    nb1 = seq // d1 // BLOCK

    def g1_body(r, carry):
        units = []
        for b in range(nb1):
            first = b == 0
            row0 = b * BLOCK
            k0 = 0 if first else (b - 1) * BLOCK
            kn = BLOCK if first else 2 * BLOCK
            for j in range(n_pair):
                def load(base, is_q, j=j, row0=row0, k0=k0, kn=kn):
                    return qkv1_ref[0, r, pl.ds(row0, BLOCK) if is_q else pl.ds(k0, kn), cols(base, j)]

                def finish(out, lse2, j=j, row0=row0):
                    start = row0 * d1 + r
                    o1_ref[j, pl.ds(start, BLOCK, stride=d1), :] = out
                    l1_ref[j, pl.ds(start, BLOCK, stride=d1), :] = lse2

                units.append(pair_unit(b * n_pair + j, load, 4 + 2 * j, first, finish))
        _run_units(units)
        return carry

    lax.fori_loop(0, d1, g1_body, 0)

    n_qp = B_OUT // LANES
    units_per_block = n_pair + n_qp

    def nat_units(b, first, slot0):
        row0 = 0 if first else pl.multiple_of(b * BLOCK, BLOCK)
        k0 = 0 if first else pl.multiple_of((b - 1) * BLOCK, BLOCK)
        kn = BLOCK if first else 2 * BLOCK
        rows = pl.ds(row0, BLOCK)
        keys = pl.ds(k0, kn)
        units = []
        for j in range(n_pair):
            def load(base, is_q, j=j):
                return qkv0_ref[rows if is_q else keys, cols(base, j)]

            def finish(o0, l0, j=j):
                o1 = o1_ref[j, rows, :]
                l1 = l1_ref[j, rows, :]
                o2 = o2_ref[j, rows, :]
                l2 = l2_ref[j, rows, :]
                top = jnp.maximum(jnp.maximum(l0, l1), l2)
                w0 = jnp.exp(l0 - top)
                w1 = jnp.exp(l1 - top)
                w2 = jnp.exp(l2 - top)
                y = (w0 * o0 + w1 * o1 + w2 * o2) / (w0 + w1 + w2)
                ya_ref[rows, cols(0, j)] = y.astype(_BF16)

            units.append(pair_unit(slot0 + j, load, 2 * j, first, finish))
        done = {}
        for half in range(2):
            for u in range(n_qp // 2):
                scr = (s_ref, p_ref, slot0 + n_pair + half * (n_qp // 2) + u)

                def first_pass(half=half, u=u, scr=scr):
                    ha = A_HEADS + n_qp * half + 2 * u
                    sink_row = sink_ref[half, :, 2 * u * BLOCK:(2 * u + 2) * BLOCK]
                    return _scores(scr, qkvb_ref[rows, cols(0, 2 * u)], qkvb_ref[rows, cols(0, 2 * u + 1)], half, half,
                                   qkvb_ref[keys, B_OUT:B_OUT + B_KV_W], *bias_pair(ha, first), sink_row=sink_row)

                def second_pass(m, half=half, u=u, scr=scr):
                    sink_row = sink_ref[half, :, 2 * u * BLOCK:(2 * u + 2) * BLOCK]
                    o_t, l = _softmax_pv(scr, m, qkvb_ref[keys, B_OUT + B_KV_W:B_OUT + 2 * B_KV_W], sink_row=sink_row)
                    done[half, u] = o_t[half * HEAD_DIM:(half + 1) * HEAD_DIM] / l
                    if half == 1:
                        for jj in range(2):
                            j = 2 * u + jj
                            tile_t = jnp.concatenate([done[0, u][:, jj * BLOCK:(jj + 1) * BLOCK],
                                                      done[1, u][:, jj * BLOCK:(jj + 1) * BLOCK]], axis=0)
                            yb_ref[rows, cols(0, j)] = tile_t.T.astype(_BF16)

                units.append((first_pass, second_pass))
        return units

    _run_units(nat_units(0, True, 0))

    def nat_body(it, carry):
        units = []
        for c in range(NAT_BLOCKS_PER_STEP):
            units += nat_units(1 + it * NAT_BLOCKS_PER_STEP + c, False, c * units_per_block)
        _run_units(units)
        return carry

    lax.fori_loop(0, (seq // BLOCK - 1) // NAT_BLOCKS_PER_STEP, nat_body, 0)


def _stage3_kernel(h1_ref, u2_ref, ya_ref, yb_ref, wgate_ref, bgate_ref, wa_ref, wb_ref, wo_ref,
                   g2_ref, wg_ref, wu_ref, wd_ref, gf_ref, out_ref):
    d_model = h1_ref.shape[1]
    subs = [slice(i * SUB_TILE, (i + 1) * SUB_TILE) for i in range(h1_ref.shape[0] // SUB_TILE)]
    pre = [(_dot(u2_ref[sl, :], wgate_ref[...]), _dot(ya_ref[sl, :], wa_ref[...]), _dot(yb_ref[sl, :], wb_ref[...]))
           for sl in subs]
    merged = []
    for gl, pa, pb in pre:
        gates = jax.nn.sigmoid(gl + bgate_ref[...])
        merged.append((gates[:, :d_model] * pa + gates[:, d_model:] * pb).astype(_BF16))
    h2s = [h1_ref[sl, :] + _dot(mg, wo_ref[...]) for sl, mg in zip(subs, merged)]
    ys = _swiglu_tiles([_rms(h2, g2_ref[...]).astype(_BF16) for h2 in h2s], wg_ref, wu_ref, wd_ref)
    for sl, h2, y in zip(subs, h2s, ys):
        out_ref[sl, :] = _rms(h2 + 0.5 * y, gf_ref[...])


def _t5_bucket_const(dist):
    max_exact = N_BUCKETS // 2
    n = jnp.maximum(dist, 0)
    nf = jnp.maximum(n, 1).astype(jnp.float32)
    large = max_exact + (jnp.log(nf / max_exact) / math.log(MAX_DISTANCE / max_exact)
                         * (N_BUCKETS - max_exact)).astype(jnp.int32)
    large = jnp.minimum(large, N_BUCKETS - 1)
    return jnp.where(n < max_exact, n, large)


def _bias_kernel(table_ref, idx_ref, out_ref):
    h = pl.program_id(0)
    idx = idx_ref[0]
    acc = jnp.full(idx.shape, MASKED, _F32)
    for b in range(N_BUCKETS):
        acc = jnp.where(idx == b, table_ref[b, h] * LOG2E, acc)
    out_ref[0] = acc


def _band_bias(rel_bias):
    ki = jnp.arange(2 * BLOCK)[:, None]
    qi = jnp.arange(BLOCK)[None, :]
    dist = qi + BLOCK - ki
    variants = [(d, w // d) for w, d in DIL_GROUPS] + [(1, B_WINDOW - 1)]
    idx = jnp.stack([jnp.where((dist >= 0) & (dist <= max_steps), _t5_bucket_const(dist * stride), -1)
                     for stride, max_steps in variants]).astype(jnp.int32)
    n_heads = rel_bias.shape[1]
    last = len(variants) - 1
    return pl.pallas_call(
        _bias_kernel,
        grid=(n_heads,),
        in_specs=[pl.BlockSpec(memory_space=pltpu.SMEM),
                  pl.BlockSpec((1, 2 * BLOCK, BLOCK),
                               lambda h: (jnp.minimum(h // A_HEADS_PER_GROUP, last), 0, 0))],
        out_specs=pl.BlockSpec((1, 2 * BLOCK, BLOCK), lambda h: (h, 0, 0)),
        out_shape=jax.ShapeDtypeStruct((n_heads, 2 * BLOCK, BLOCK), _F32),
        name="band_bias",
    )(rel_bias.astype(_F32), idx)


def _const_spec(shape):
    nd = len(shape)
    return pl.BlockSpec(shape, lambda *_: (0,) * nd, pipeline_mode=pl.Buffered(1))


def _layer(x, ffn1_norm, ffn1_w_gate, ffn1_w_up, ffn1_w_down, mix_norm, w_in, b_in,
           w_branch_a, w_branch_b, w_out, sinks, rel_bias,
           ffn2_norm, ffn2_w_gate, ffn2_w_up, ffn2_w_down, final_norm):
    batch, seq, d_model = x.shape
    tokens = batch * seq
    tm = TOKEN_TILE
    tiles_per_seq = seq // tm
    d_ff = ffn1_w_gate.shape[1]
    a_w = A_HEADS * HEAD_DIM
    scale = HEAD_DIM ** -0.5 * LOG2E

    aq, ak, av = w_in[:, :a_w], w_in[:, a_w:2 * a_w], w_in[:, 2 * a_w:3 * a_w]
    baq, bak, bav = b_in[:a_w], b_in[a_w:2 * a_w], b_in[2 * a_w:3 * a_w]
    off = 3 * a_w
    bq_w, bq_b = w_in[:, off:off + B_OUT], b_in[off:off + B_OUT]
    bkv_w, bkv_b = w_in[:, off + B_OUT:off + B_OUT + 2 * B_KV_W], b_in[off + B_OUT:off + B_OUT + 2 * B_KV_W]
    gate_w, gate_b = w_in[:, off + B_W:], b_in[off + B_W:]
    perm = np.concatenate([np.r_[j * HEAD_DIM:(j + 1) * HEAD_DIM,
                                 (j + 4) * HEAD_DIM:(j + 5) * HEAD_DIM] for j in range(B_Q_HEADS // 2)])
    w_cols, b_cols = [], []
    for gi in range(len(DIL_GROUPS)):
        sl = slice(gi * A_OUT, (gi + 1) * A_OUT)
        w_cols += [aq[:, sl] * scale, ak[:, sl], av[:, sl]]
        b_cols += [baq[sl] * scale, bak[sl], bav[sl]]
    w_cols += [bq_w[:, perm] * scale, bkv_w]
    b_cols += [bq_b[perm] * scale, bkv_b]
    wqkv = jnp.concatenate(w_cols, axis=1).astype(_BF16)
    bqkv = jnp.concatenate(b_cols)[None, :].astype(_F32)
    wb_perm = w_branch_b[perm, :].astype(_BF16)

    row = lambda v: v[None, :].astype(_F32)
    x2 = x.reshape(tokens, d_model)
    params = pltpu.CompilerParams(dimension_semantics=("parallel",), vmem_limit_bytes=VMEM_LIMIT)

    tile = lambda w: pl.BlockSpec((tm, w), lambda i: (i, 0))
    d1, d2 = DIL_GROUPS[1][1], DIL_GROUPS[2][1]
    h1, u2, qkv0, qkv1, qkv2, qkvb = pl.pallas_call(
        _stage1_kernel,
        grid=(tokens // tm,),
        in_specs=[tile(d_model), _const_spec((1, d_model)),
                  _const_spec((d_model, d_ff)), _const_spec((d_model, d_ff)), _const_spec((d_ff, d_model)),
                  _const_spec((1, d_model)), _const_spec((d_model, 3 * GROUP_W + B_W)),
                  _const_spec((1, 3 * GROUP_W + B_W))],
        out_specs=[tile(d_model), tile(d_model), tile(GROUP_W),
                   pl.BlockSpec((1, d1, tm // d1, GROUP_W),
                                lambda i: (i // tiles_per_seq, 0, i % tiles_per_seq, 0)),
                   pl.BlockSpec((1, d2, tm // d2, GROUP_W),
                                lambda i: (i // tiles_per_seq, 0, i % tiles_per_seq, 0)),
                   tile(B_W)],
        out_shape=[jax.ShapeDtypeStruct((tokens, d_model), _F32),
                   jax.ShapeDtypeStruct((tokens, d_model), _BF16),
                   jax.ShapeDtypeStruct((tokens, GROUP_W), _BF16),
                   jax.ShapeDtypeStruct((batch, d1, seq // d1, GROUP_W), _BF16),
                   jax.ShapeDtypeStruct((batch, d2, seq // d2, GROUP_W), _BF16),
                   jax.ShapeDtypeStruct((tokens, B_W), _BF16)],
        scratch_shapes=[pltpu.VMEM((tm // SUB_TILE, 2 * GROUP_W // LANES, SUB_TILE, LANES), _F32)],
        compiler_params=params,
        name="stage1_ffn_qkv",
    )(x2, row(ffn1_norm), ffn1_w_gate.astype(_BF16), ffn1_w_up.astype(_BF16), ffn1_w_down.astype(_BF16),
      row(mix_norm), wqkv, bqkv)

    bias = _band_bias(rel_bias)
    n_heads = bias.shape[0]
    sink_rows = jnp.repeat((sinks.astype(_F32) * LOG2E).reshape(B_KV_HEADS, B_Q_HEADS // B_KV_HEADS), BLOCK,
                           axis=1)[:, None, :]
    seq_tile = lambda w: pl.BlockSpec((seq, w), lambda b: (b, 0))
    ya, yb = pl.pallas_call(
        _stage2_kernel,
        grid=(batch,),
        in_specs=[seq_tile(GROUP_W),
                  pl.BlockSpec((1, d1, seq // d1, GROUP_W), lambda b: (b, 0, 0, 0)),
                  pl.BlockSpec((1, d2, seq // d2, GROUP_W), lambda b: (b, 0, 0, 0)),
                  seq_tile(B_W),
                  _const_spec((n_heads, 2 * BLOCK, BLOCK)),
                  _const_spec(sink_rows.shape)],
        out_specs=[seq_tile(A_OUT), seq_tile(B_OUT)],
        out_shape=[jax.ShapeDtypeStruct((tokens, A_OUT), _BF16),
                   jax.ShapeDtypeStruct((tokens, B_OUT), _BF16)],
        scratch_shapes=[pltpu.VMEM((A_OUT // LANES, seq, LANES), _F32) for _ in range(4)]
        + [pltpu.VMEM((N_SLOTS, 2 * BLOCK, 2 * BLOCK), _F32), pltpu.VMEM((N_SLOTS, 2 * BLOCK, 2 * BLOCK), _BF16)],
        compiler_params=params,
        name="stage2_attention",
    )(qkv0, qkv1, qkv2, qkvb, bias, sink_rows)

    out = pl.pallas_call(
        _stage3_kernel,
        grid=(tokens // tm,),
        in_specs=[tile(d_model), tile(d_model), tile(A_OUT), tile(B_OUT),
                  _const_spec((d_model, 2 * d_model)), _const_spec((1, 2 * d_model)),
                  _const_spec((A_OUT, d_model)), _const_spec((B_OUT, d_model)), _const_spec((d_model, d_model)),
                  _const_spec((1, d_model)),
                  _const_spec((d_model, d_ff)), _const_spec((d_model, d_ff)), _const_spec((d_ff, d_model)),
                  _const_spec((1, d_model))],
        out_specs=tile(d_model),
        out_shape=jax.ShapeDtypeStruct((tokens, d_model), _F32),
        compiler_params=params,
        name="stage3_merge_ffn",
    )(h1, u2, ya, yb, gate_w.astype(_BF16), row(gate_b), w_branch_a.astype(_BF16), wb_perm,
      w_out.astype(_BF16), row(ffn2_norm), ffn2_w_gate.astype(_BF16), ffn2_w_up.astype(_BF16),
      ffn2_w_down.astype(_BF16), row(final_norm))
    return out.reshape(batch, seq, d_model)


def kernel(x, ffn1_norm, ffn1_w_gate, ffn1_w_up, ffn1_w_down, mix_norm, w_in, b_in, w_branch_a, w_branch_b,
           w_out, sinks, rel_bias, ffn2_norm, ffn2_w_gate, ffn2_w_up, ffn2_w_down, final_norm):
    assert ffn1_norm.shape[0] == 1, "single-layer kernel"
    return _layer(x, ffn1_norm[0], ffn1_w_gate[0], ffn1_w_up[0], ffn1_w_down[0], mix_norm[0], w_in[0], b_in[0],
                  w_branch_a[0], w_branch_b[0], w_out[0], sinks[0], rel_bias,
                  ffn2_norm[0], ffn2_w_gate[0], ffn2_w_up[0], ffn2_w_down[0], final_norm)
```

```python
import math

import numpy as np
import jax
import jax.numpy as jnp
from jax import lax
from jax.experimental import pallas as pl
from jax.experimental.pallas import tpu as pltpu

HEAD_DIM = 64
DIL_GROUPS = ((128, 1), (512, 4), (2048, 16))
A_HEADS_PER_GROUP = 4
A_HEADS = A_HEADS_PER_GROUP * len(DIL_GROUPS)
A_OUT = A_HEADS_PER_GROUP * HEAD_DIM
B_Q_HEADS = 8
B_KV_HEADS = 2
B_WINDOW = 128
B_OUT = B_Q_HEADS * HEAD_DIM
B_KV_W = B_KV_HEADS * HEAD_DIM
BLOCK = 128
N_BUCKETS = 32
MAX_DISTANCE = 2048
EPS = 1e-6
GROUP_W = 3 * A_OUT
B_W = B_OUT + 2 * B_KV_W
LANES = 128
SUBLANES = 8
LOG2E = math.log2(math.e)
MASKED = -1e30

TOKEN_TILE = 512
SUB_TILE = 256
VMEM_LIMIT = 56 * 1024 * 1024

_BF16 = jnp.bfloat16
_F32 = jnp.float32


def _rms(x, g):
    return (x * lax.rsqrt(jnp.mean(x * x, axis=-1, keepdims=True) + EPS)) * g


def _dot(a, b):
    return jnp.dot(a, b, preferred_element_type=_F32)


def _silu_mul(g, u):
    return ((g * jax.nn.sigmoid(g)) * u).astype(_BF16)


def _swiglu_tiles(xbs, wg_ref, wu_ref, wd_ref):
    gu = [(_dot(xb, wg_ref[...]), _dot(xb, wu_ref[...])) for xb in xbs]
    return [_dot(_silu_mul(g, u), wd_ref[...]) for g, u in gu]


def _stage1_kernel(x_ref, g1_ref, wg_ref, wu_ref, wd_ref, gmix_ref, wqkv_ref, bqkv_ref,
                   h1_ref, u2_ref, qkv0_ref, qkv1_ref, qkv2_ref, qkvb_ref, zs_ref):
    subs = [slice(i * SUB_TILE, (i + 1) * SUB_TILE) for i in range(x_ref.shape[0] // SUB_TILE)]
    xs = [x_ref[sl, :] for sl in subs]
    ys = _swiglu_tiles([_rms(x, g1_ref[...]).astype(_BF16) for x in xs], wg_ref, wu_ref, wd_ref)
    u2s = []
    for sl, x, y in zip(subs, xs, ys):
        h1 = x + 0.5 * y
        h1_ref[sl, :] = h1
        u2 = _rms(h1, gmix_ref[...]).astype(_BF16)
        u2_ref[sl, :] = u2
        u2s.append(u2)
    zs = [_dot(u2, wqkv_ref[...]) + bqkv_ref[...] for u2 in u2s]
    n_slab = GROUP_W // LANES
    for i, (sl, z) in enumerate(zip(subs, zs)):
        qkv0_ref[sl, :] = z[:, :GROUP_W].astype(_BF16)
        qkvb_ref[sl, :] = z[:, 3 * GROUP_W:].astype(_BF16)
        for j in range(2 * n_slab):
            zs_ref[i, j] = z[:, GROUP_W + j * LANES:GROUP_W + (j + 1) * LANES]
        for gi, out_ref in ((1, qkv1_ref), (2, qkv2_ref)):
            d = DIL_GROUPS[gi][1]
            n = SUB_TILE // d
            for j in range(n_slab):
                for r in range(d):
                    rows = zs_ref[i, (gi - 1) * n_slab + j, pl.ds(r, n, stride=d), :]
                    out_ref[0, r, i * n:(i + 1) * n, j * LANES:(j + 1) * LANES] = rows.astype(_BF16)


KEY_CHUNK = 32
LOOKAHEAD = 4
G2_CLASSES_PER_STEP = 4
NAT_BLOCKS_PER_STEP = 5
N_SLOTS = 30


def _scores(scr, q2a, q2b, half_a, half_b, kc, bias_a, bias_b, sink_row=None):
    s_ref, _, slot = scr
    kn = kc.shape[0]
    lane = lax.broadcasted_iota(jnp.int32, q2a.shape, 1)
    keep = lambda q2, half: jnp.where((lane < HEAD_DIM) if half == 0 else (lane >= HEAD_DIM), q2, jnp.zeros_like(q2))
    qm = jnp.concatenate([keep(q2a, half_a), keep(q2b, half_b)], axis=0)
    s = lax.dot_general(kc, qm, (((1,), (1,)), ((), ())), preferred_element_type=_F32)
    s = s + jnp.concatenate([bias_a, bias_b], axis=1)
    s_ref[slot, :kn, :] = s
    m = jnp.max(s, axis=0, keepdims=True)
    if sink_row is not None:
        m = jnp.maximum(m, sink_row)
    return m


def _softmax_pv(scr, m, vc, sink_row=None):
    s_ref, p_ref, slot = scr
    kn = vc.shape[0]
    acc = jnp.zeros((SUBLANES, m.shape[1]), _F32)
    for c in range(kn // KEY_CHUNK):
        rows = slice(c * KEY_CHUNK, (c + 1) * KEY_CHUNK)
        p = jnp.exp2(s_ref[slot, rows, :] - m)
        for t in range(KEY_CHUNK // SUBLANES):
            acc = acc + p[t * SUBLANES:(t + 1) * SUBLANES]
        p_ref[slot, rows, :] = p.astype(_BF16)
    l = jnp.sum(acc, axis=0, keepdims=True)
    if sink_row is not None:
        l = l + jnp.exp2(sink_row - m)
    o_t = lax.dot_general(vc, p_ref[slot, :kn, :], (((0,), (0,)), ((), ())), preferred_element_type=_F32)
    return o_t, l


def _pair_finish(o_t, m, l):
    lse = (m + jnp.log2(l)) * (1.0 / LOG2E)
    on = jnp.concatenate([o_t[:HEAD_DIM, :BLOCK] / l[:, :BLOCK], o_t[HEAD_DIM:, BLOCK:] / l[:, BLOCK:]], axis=0)
    lb = jnp.concatenate([jnp.broadcast_to(lse[:, :BLOCK], (HEAD_DIM, BLOCK)),
                          jnp.broadcast_to(lse[:, BLOCK:], (HEAD_DIM, BLOCK))], axis=0)
    return on.T, lb.T


def _run_units(units):
    state = {}
    for i in range(len(units) + LOOKAHEAD):
        if i < len(units):
            state[i] = units[i][0]()
        k = i - LOOKAHEAD
        if k >= 0:
            units[k][1](state.pop(k))


def _stage2_kernel(qkv0_ref, qkv1_ref, qkv2_ref, qkvb_ref, bias_ref, sink_ref,
                   ya_ref, yb_ref, o1_ref, l1_ref, o2_ref, l2_ref, s_ref, p_ref):
    n_pair = A_OUT // LANES
    seq = qkv0_ref.shape[0]

    def cols(base, j):
        return slice(base + j * LANES, base + (j + 1) * LANES)

    def bias_pair(h, first):
        if first:
            return bias_ref[h, BLOCK:, :], bias_ref[h + 1, BLOCK:, :]
        return bias_ref[h], bias_ref[h + 1]

    def pair_unit(slot, load, h, first, finish):
        scr = (s_ref, p_ref, slot)

        def first_pass():
            q2 = load(0, True)
            return _scores(scr, q2, q2, 0, 1, load(A_OUT, False), *bias_pair(h, first))

        def second_pass(m):
            o_t, l = _softmax_pv(scr, m, load(2 * A_OUT, False))
            finish(o_t, m, l)

        return first_pass, second_pass

    d2 = DIL_GROUPS[2][1]

    def g2_body(it, carry):
        units = []
        for c in range(G2_CLASSES_PER_STEP):
            r = it * G2_CLASSES_PER_STEP + c
            for j in range(n_pair):
                def load(base, is_q, r=r, j=j):
                    return qkv2_ref[0, r, :, cols(base, j)]

                def finish(o_t, m, l, r=r, j=j):
                    out, lse2 = _pair_finish(o_t, m, l)
                    o2_ref[j, pl.ds(r, BLOCK, stride=d2), :] = out
                    l2_ref[j, pl.ds(r, BLOCK, stride=d2), :] = lse2

                units.append(pair_unit(c * n_pair + j, load, 8 + 2 * j, True, finish))
        _run_units(units)
        return carry

    lax.fori_loop(0, d2 // G2_CLASSES_PER_STEP, g2_body, 0)

    d1 = DIL_GROUPS[1][1]
    nb1 = seq // d1 // BLOCK

    def g1_body(r, carry):
        units = []
        for b in range(nb1):
            first = b == 0
            row0 = b * BLOCK
            k0 = 0 if first else (b - 1) * BLOCK
            kn = BLOCK if first else 2 * BLOCK
            for j in range(n_pair):
                def load(base, is_q, j=j, row0=row0, k0=k0, kn=kn):
                    return qkv1_ref[0, r, pl.ds(row0, BLOCK) if is_q else pl.ds(k0, kn), cols(base, j)]

                def finish(o_t, m, l, j=j, row0=row0):
                    out, lse2 = _pair_finish(o_t, m, l)
                    start = row0 * d1 + r
                    o1_ref[j, pl.ds(start, BLOCK, stride=d1), :] = out
                    l1_ref[j, pl.ds(start, BLOCK, stride=d1), :] = lse2

                units.append(pair_unit(b * n_pair + j, load, 4 + 2 * j, first, finish))
        _run_units(units)
        return carry

    lax.fori_loop(0, d1, g1_body, 0)

    n_qp = B_OUT // LANES
    units_per_block = n_pair + n_qp

    def nat_units(b, first, slot0):
        row0 = 0 if first else pl.multiple_of(b * BLOCK, BLOCK)
        k0 = 0 if first else pl.multiple_of((b - 1) * BLOCK, BLOCK)
        kn = BLOCK if first else 2 * BLOCK
        rows = pl.ds(row0, BLOCK)
        keys = pl.ds(k0, kn)
        units = []
        for j in range(n_pair):
            def load(base, is_q, j=j):
                return qkv0_ref[rows if is_q else keys, cols(base, j)]

            def finish(o_t, m, l, j=j):
                lse0 = (m + jnp.log2(l)) * (1.0 / LOG2E)
                o1_t, l1_t = o1_ref[j, rows, :].T, l1_ref[j, rows, :].T
                o2_t, l2_t = o2_ref[j, rows, :].T, l2_ref[j, rows, :].T
                parts = []
                for hh in range(2):
                    hd = slice(hh * HEAD_DIM, (hh + 1) * HEAD_DIM)
                    tok = slice(hh * BLOCK, (hh + 1) * BLOCK)
                    a0 = lse0[:, tok]
                    a1 = l1_t[hh * HEAD_DIM:hh * HEAD_DIM + 1, :]
                    a2 = l2_t[hh * HEAD_DIM:hh * HEAD_DIM + 1, :]
                    top = jnp.maximum(jnp.maximum(a0, a1), a2)
                    w0, w1, w2 = jnp.exp(a0 - top), jnp.exp(a1 - top), jnp.exp(a2 - top)
                    inv = 1.0 / (w0 + w1 + w2)
                    parts.append(o_t[hd, tok] * (w0 * inv / l[:, tok]) + o1_t[hd, :] * (w1 * inv)
                                 + o2_t[hd, :] * (w2 * inv))
                ya_ref[rows, cols(0, j)] = jnp.concatenate(parts, axis=0).T.astype(_BF16)

            units.append(pair_unit(slot0 + j, load, 2 * j, first, finish))
        done = {}
        for half in range(2):
            for u in range(n_qp // 2):
                scr = (s_ref, p_ref, slot0 + n_pair + half * (n_qp // 2) + u)

                def first_pass(half=half, u=u, scr=scr):
                    ha = A_HEADS + n_qp * half + 2 * u
                    sink_row = sink_ref[half, :, 2 * u * BLOCK:(2 * u + 2) * BLOCK]
                    return _scores(scr, qkvb_ref[rows, cols(0, 2 * u)], qkvb_ref[rows, cols(0, 2 * u + 1)], half, half,
                                   qkvb_ref[keys, B_OUT:B_OUT + B_KV_W], *bias_pair(ha, first), sink_row=sink_row)

                def second_pass(m, half=half, u=u, scr=scr):
                    sink_row = sink_ref[half, :, 2 * u * BLOCK:(2 * u + 2) * BLOCK]
                    o_t, l = _softmax_pv(scr, m, qkvb_ref[keys, B_OUT + B_KV_W:B_OUT + 2 * B_KV_W], sink_row=sink_row)
                    done[half, u] = o_t[half * HEAD_DIM:(half + 1) * HEAD_DIM] / l
                    if half == 1:
                        for jj in range(2):
                            j = 2 * u + jj
                            tile_t = jnp.concatenate([done[0, u][:, jj * BLOCK:(jj + 1) * BLOCK],
                                                      done[1, u][:, jj * BLOCK:(jj + 1) * BLOCK]], axis=0)
                            yb_ref[rows, cols(0, j)] = tile_t.T.astype(_BF16)

                units.append((first_pass, second_pass))
        return units

    _run_units(nat_units(0, True, 0))

    def nat_body(it, carry):
        units = []
        for c in range(NAT_BLOCKS_PER_STEP):
            units += nat_units(1 + it * NAT_BLOCKS_PER_STEP + c, False, c * units_per_block)
        _run_units(units)
        return carry

    lax.fori_loop(0, (seq // BLOCK - 1) // NAT_BLOCKS_PER_STEP, nat_body, 0)


def _stage3_kernel(h1_ref, u2_ref, ya_ref, yb_ref, wgate_ref, bgate_ref, wa_ref, wb_ref, wo_ref,
                   g2_ref, wg_ref, wu_ref, wd_ref, gf_ref, out_ref):
    d_model = h1_ref.shape[1]
    subs = [slice(i * SUB_TILE, (i + 1) * SUB_TILE) for i in range(h1_ref.shape[0] // SUB_TILE)]
    pre = [(_dot(u2_ref[sl, :], wgate_ref[...]), _dot(ya_ref[sl, :], wa_ref[...]), _dot(yb_ref[sl, :], wb_ref[...]))
           for sl in subs]
    merged = []
    for gl, pa, pb in pre:
        gates = jax.nn.sigmoid(gl + bgate_ref[...])
        merged.append((gates[:, :d_model] * pa + gates[:, d_model:] * pb).astype(_BF16))
    h2s = [h1_ref[sl, :] + _dot(mg, wo_ref[...]) for sl, mg in zip(subs, merged)]
    ys = _swiglu_tiles([_rms(h2, g2_ref[...]).astype(_BF16) for h2 in h2s], wg_ref, wu_ref, wd_ref)
    for sl, h2, y in zip(subs, h2s, ys):
        out_ref[sl, :] = _rms(h2 + 0.5 * y, gf_ref[...])


def _t5_bucket_const(dist):
    max_exact = N_BUCKETS // 2
    n = jnp.maximum(dist, 0)
    nf = jnp.maximum(n, 1).astype(jnp.float32)
    large = max_exact + (jnp.log(nf / max_exact) / math.log(MAX_DISTANCE / max_exact)
                         * (N_BUCKETS - max_exact)).astype(jnp.int32)
    large = jnp.minimum(large, N_BUCKETS - 1)
    return jnp.where(n < max_exact, n, large)


def _bias_kernel(table_ref, idx_ref, out_ref):
    h = pl.program_id(0)
    idx = idx_ref[0]
    acc = jnp.full(idx.shape, MASKED, _F32)
    for b in range(N_BUCKETS):
        acc = jnp.where(idx == b, table_ref[b, h] * LOG2E, acc)
    out_ref[0] = acc


def _band_bias(rel_bias):
    ki = jnp.arange(2 * BLOCK)[:, None]
    qi = jnp.arange(BLOCK)[None, :]
    dist = qi + BLOCK - ki
    variants = [(d, w // d) for w, d in DIL_GROUPS] + [(1, B_WINDOW - 1)]
    idx = jnp.stack([jnp.where((dist >= 0) & (dist <= max_steps), _t5_bucket_const(dist * stride), -1)
                     for stride, max_steps in variants]).astype(jnp.int32)
    n_heads = rel_bias.shape[1]
    last = len(variants) - 1
    return pl.pallas_call(
        _bias_kernel,
        grid=(n_heads,),
        in_specs=[pl.BlockSpec(memory_space=pltpu.SMEM),
                  pl.BlockSpec((1, 2 * BLOCK, BLOCK),
                               lambda h: (jnp.minimum(h // A_HEADS_PER_GROUP, last), 0, 0))],
        out_specs=pl.BlockSpec((1, 2 * BLOCK, BLOCK), lambda h: (h, 0, 0)),
        out_shape=jax.ShapeDtypeStruct((n_heads, 2 * BLOCK, BLOCK), _F32),
        name="band_bias",
    )(rel_bias.astype(_F32), idx)


def _const_spec(shape):
    nd = len(shape)
    return pl.BlockSpec(shape, lambda *_: (0,) * nd, pipeline_mode=pl.Buffered(1))


def _layer(x, ffn1_norm, ffn1_w_gate, ffn1_w_up, ffn1_w_down, mix_norm, w_in, b_in,
           w_branch_a, w_branch_b, w_out, sinks, rel_bias,
           ffn2_norm, ffn2_w_gate, ffn2_w_up, ffn2_w_down, final_norm):
    batch, seq, d_model = x.shape
    tokens = batch * seq
    tm = TOKEN_TILE
    tiles_per_seq = seq // tm
    d_ff = ffn1_w_gate.shape[1]
    a_w = A_HEADS * HEAD_DIM
    scale = HEAD_DIM ** -0.5 * LOG2E

    aq, ak, av = w_in[:, :a_w], w_in[:, a_w:2 * a_w], w_in[:, 2 * a_w:3 * a_w]
    baq, bak, bav = b_in[:a_w], b_in[a_w:2 * a_w], b_in[2 * a_w:3 * a_w]
    off = 3 * a_w
    bq_w, bq_b = w_in[:, off:off + B_OUT], b_in[off:off + B_OUT]
    bkv_w, bkv_b = w_in[:, off + B_OUT:off + B_OUT + 2 * B_KV_W], b_in[off + B_OUT:off + B_OUT + 2 * B_KV_W]
    gate_w, gate_b = w_in[:, off + B_W:], b_in[off + B_W:]
    perm = np.concatenate([np.r_[j * HEAD_DIM:(j + 1) * HEAD_DIM,
                                 (j + 4) * HEAD_DIM:(j + 5) * HEAD_DIM] for j in range(B_Q_HEADS // 2)])
    w_cols, b_cols = [], []
    for gi in range(len(DIL_GROUPS)):
        sl = slice(gi * A_OUT, (gi + 1) * A_OUT)
        w_cols += [aq[:, sl] * scale, ak[:, sl], av[:, sl]]
        b_cols += [baq[sl] * scale, bak[sl], bav[sl]]
    w_cols += [bq_w[:, perm] * scale, bkv_w]
    b_cols += [bq_b[perm] * scale, bkv_b]
    wqkv = jnp.concatenate(w_cols, axis=1).astype(_BF16)
    bqkv = jnp.concatenate(b_cols)[None, :].astype(_F32)
    wb_perm = w_branch_b[perm, :].astype(_BF16)

    row = lambda v: v[None, :].astype(_F32)
    x2 = x.reshape(tokens, d_model)
    params = pltpu.CompilerParams(dimension_semantics=("parallel",), vmem_limit_bytes=VMEM_LIMIT)

    tile = lambda w: pl.BlockSpec((tm, w), lambda i: (i, 0))
    d1, d2 = DIL_GROUPS[1][1], DIL_GROUPS[2][1]
    h1, u2, qkv0, qkv1, qkv2, qkvb = pl.pallas_call(
        _stage1_kernel,
        grid=(tokens // tm,),
        in_specs=[tile(d_model), _const_spec((1, d_model)),
                  _const_spec((d_model, d_ff)), _const_spec((d_model, d_ff)), _const_spec((d_ff, d_model)),
                  _const_spec((1, d_model)), _const_spec((d_model, 3 * GROUP_W + B_W)),
                  _const_spec((1, 3 * GROUP_W + B_W))],
        out_specs=[tile(d_model), tile(d_model), tile(GROUP_W),
                   pl.BlockSpec((1, d1, tm // d1, GROUP_W),
                                lambda i: (i // tiles_per_seq, 0, i % tiles_per_seq, 0)),
                   pl.BlockSpec((1, d2, tm // d2, GROUP_W),
                                lambda i: (i // tiles_per_seq, 0, i % tiles_per_seq, 0)),
                   tile(B_W)],
        out_shape=[jax.ShapeDtypeStruct((tokens, d_model), _F32),
                   jax.ShapeDtypeStruct((tokens, d_model), _BF16),
                   jax.ShapeDtypeStruct((tokens, GROUP_W), _BF16),
                   jax.ShapeDtypeStruct((batch, d1, seq // d1, GROUP_W), _BF16),
                   jax.ShapeDtypeStruct((batch, d2, seq // d2, GROUP_W), _BF16),
                   jax.ShapeDtypeStruct((tokens, B_W), _BF16)],
        scratch_shapes=[pltpu.VMEM((tm // SUB_TILE, 2 * GROUP_W // LANES, SUB_TILE, LANES), _F32)],
        compiler_params=params,
        name="stage1_ffn_qkv",
    )(x2, row(ffn1_norm), ffn1_w_gate.astype(_BF16), ffn1_w_up.astype(_BF16), ffn1_w_down.astype(_BF16),
      row(mix_norm), wqkv, bqkv)

    bias = _band_bias(rel_bias)
    n_heads = bias.shape[0]
    sink_rows = jnp.repeat((sinks.astype(_F32) * LOG2E).reshape(B_KV_HEADS, B_Q_HEADS // B_KV_HEADS), BLOCK,
                           axis=1)[:, None, :]
    seq_tile = lambda w: pl.BlockSpec((seq, w), lambda b: (b, 0))
    ya, yb = pl.pallas_call(
        _stage2_kernel,
        grid=(batch,),
        in_specs=[seq_tile(GROUP_W),
                  pl.BlockSpec((1, d1, seq // d1, GROUP_W), lambda b: (b, 0, 0, 0)),
                  pl.BlockSpec((1, d2, seq // d2, GROUP_W), lambda b: (b, 0, 0, 0)),
                  seq_tile(B_W),
                  _const_spec((n_heads, 2 * BLOCK, BLOCK)),
                  _const_spec(sink_rows.shape)],
        out_specs=[seq_tile(A_OUT), seq_tile(B_OUT)],
        out_shape=[jax.ShapeDtypeStruct((tokens, A_OUT), _BF16),
                   jax.ShapeDtypeStruct((tokens, B_OUT), _BF16)],
        scratch_shapes=[pltpu.VMEM((A_OUT // LANES, seq, LANES), _F32) for _ in range(4)]
        + [pltpu.VMEM((N_SLOTS, 2 * BLOCK, 2 * BLOCK), _F32), pltpu.VMEM((N_SLOTS, 2 * BLOCK, 2 * BLOCK), _BF16)],
        compiler_params=params,
        name="stage2_attention",
    )(qkv0, qkv1, qkv2, qkvb, bias, sink_rows)

    out = pl.pallas_call(
        _stage3_kernel,
        grid=(tokens // tm,),
        in_specs=[tile(d_model), tile(d_model), tile(A_OUT), tile(B_OUT),
                  _const_spec((d_model, 2 * d_model)), _const_spec((1, 2 * d_model)),
                  _const_spec((A_OUT, d_model)), _const_spec((B_OUT, d_model)), _const_spec((d_model, d_model)),
                  _const_spec((1, d_model)),
                  _const_spec((d_model, d_ff)), _const_spec((d_model, d_ff)), _const_spec((d_ff, d_model)),
                  _const_spec((1, d_model))],
        out_specs=tile(d_model),
        out_shape=jax.ShapeDtypeStruct((tokens, d_model), _F32),
        compiler_params=params,
        name="stage3_merge_ffn",
    )(h1, u2, ya, yb, gate_w.astype(_BF16), row(gate_b), w_branch_a.astype(_BF16), wb_perm,
      w_out.astype(_BF16), row(ffn2_norm), ffn2_w_gate.astype(_BF16), ffn2_w_up.astype(_BF16),
      ffn2_w_down.astype(_BF16), row(final_norm))
    return out.reshape(batch, seq, d_model)


def kernel(x, ffn1_norm, ffn1_w_gate, ffn1_w_up, ffn1_w_down, mix_norm, w_in, b_in, w_branch_a, w_branch_b,
           w_out, sinks, rel_bias, ffn2_norm, ffn2_w_gate, ffn2_w_up, ffn2_w_down, final_norm):
    assert ffn1_norm.shape[0] == 1, "single-layer kernel"
    return _layer(x, ffn1_norm[0], ffn1_w_gate[0], ffn1_w_up[0], ffn1_w_down[0], mix_norm[0], w_in[0], b_in[0],
                  w_branch_a[0], w_branch_b[0], w_out[0], sinks[0], rel_bias,
                  ffn2_norm[0], ffn2_w_gate[0], ffn2_w_up[0], ffn2_w_down[0], final_norm)
```

```python
import functools
import math

import numpy as np
import jax
import jax.numpy as jnp
from jax import lax
from jax.experimental import pallas as pl
from jax.experimental.pallas import tpu as pltpu

HEAD_DIM = 64
DIL_GROUPS = ((128, 1), (512, 4), (2048, 16))
A_HEADS_PER_GROUP = 4
A_HEADS = A_HEADS_PER_GROUP * len(DIL_GROUPS)
A_OUT = A_HEADS_PER_GROUP * HEAD_DIM
B_Q_HEADS = 8
B_KV_HEADS = 2
B_WINDOW = 128
B_OUT = B_Q_HEADS * HEAD_DIM
B_KV_W = B_KV_HEADS * HEAD_DIM
BLOCK = 128
N_BUCKETS = 32
MAX_DISTANCE = 2048
EPS = 1e-6
GROUP_W = 3 * A_OUT
B_W = B_OUT + 2 * B_KV_W
LANES = 128
SUBLANES = 8
MXU_COLS = 256
LOG2E = math.log2(math.e)
MASKED = -1e30

TOKEN_TILE = 512
SUB_TILE = 256
VMEM_LIMIT = 60 * 1024 * 1024

_BF16 = jnp.bfloat16
_F32 = jnp.float32


def _rms(x, g):
    return (x * lax.rsqrt(jnp.mean(x * x, axis=-1, keepdims=True) + EPS)) * g


def _dot(a, b):
    return jnp.dot(a, b, preferred_element_type=_F32)


def _gate_up(xb, wgu_ref, chunks=None, gap=None):
    acts = []
    c0 = 0
    for w in chunks or (wgu_ref.shape[1],):
        if gap is not None:
            gap()
        gu = _dot(xb, wgu_ref[:, c0:c0 + w])
        c0 += w
        for c in range(w // (2 * MXU_COLS)):
            g = gu[:, 2 * c * MXU_COLS:(2 * c + 1) * MXU_COLS]
            u = gu[:, (2 * c + 1) * MXU_COLS:(2 * c + 2) * MXU_COLS]
            acts.append(((g * jax.nn.sigmoid(g)) * u).astype(_BF16))
    return jnp.concatenate(acts, axis=1)


def _dot_chunks(a, w_ref, chunks, gap):
    outs = []
    c0 = 0
    for w in chunks:
        gap()
        outs.append(_dot(a, w_ref[:, c0:c0 + w]))
        c0 += w
    return jnp.concatenate(outs, axis=1)


KEY_CHUNK = 32
LOOKAHEAD = 4
G2_CLASSES_PER_STEP = 4
N_SLOTS = 8
GU_CHUNKS = (1536, 1536, 1536, 1024)
D_CHUNKS = (512, 512)
Z_CHUNKS = (768, 768, 768, 768)
UNITS_PER_GU_GAP, UNITS_PER_D_GAP, UNITS_PER_Z_GAP = 1, 2, 1


def _scores(scr, q2a, q2b, half_a, half_b, kc, bias_a, bias_b, sink_row=None):
    s_ref, _, slot = scr
    kn = kc.shape[0]
    lane = lax.broadcasted_iota(jnp.int32, q2a.shape, 1)
    keep = lambda q2, half: jnp.where((lane < HEAD_DIM) if half == 0 else (lane >= HEAD_DIM), q2, jnp.zeros_like(q2))
    qm = jnp.concatenate([keep(q2a, half_a), keep(q2b, half_b)], axis=0)
    s = lax.dot_general(kc, qm, (((1,), (1,)), ((), ())), preferred_element_type=_F32)
    s = s + jnp.concatenate([bias_a, bias_b], axis=1)
    s_ref[slot, :kn, :] = s
    m = jnp.max(s, axis=0, keepdims=True)
    if sink_row is not None:
        m = jnp.maximum(m, sink_row)
    return m


def _softmax_pv(scr, m, vc, sink_row=None):
    s_ref, p_ref, slot = scr
    kn = vc.shape[0]
    acc = jnp.zeros((SUBLANES, m.shape[1]), _F32)
    for c in range(kn // KEY_CHUNK):
        rows = slice(c * KEY_CHUNK, (c + 1) * KEY_CHUNK)
        p = jnp.exp2(s_ref[slot, rows, :] - m)
        for t in range(KEY_CHUNK // SUBLANES):
            acc = acc + p[t * SUBLANES:(t + 1) * SUBLANES]
        p_ref[slot, rows, :] = p.astype(_BF16)
    l = jnp.sum(acc, axis=0, keepdims=True)
    if sink_row is not None:
        l = l + jnp.exp2(sink_row - m)
    o_t = lax.dot_general(vc, p_ref[slot, :kn, :], (((0,), (0,)), ((), ())), preferred_element_type=_F32)
    return o_t, l


def _pair_finish(o_t, m, l):
    lse = (m + jnp.log2(l)) * (1.0 / LOG2E)
    on = jnp.concatenate([o_t[:HEAD_DIM, :BLOCK] / l[:, :BLOCK], o_t[HEAD_DIM:, BLOCK:] / l[:, BLOCK:]], axis=0)
    lb = jnp.concatenate([jnp.broadcast_to(lse[:, :BLOCK], (HEAD_DIM, BLOCK)),
                          jnp.broadcast_to(lse[:, BLOCK:], (HEAD_DIM, BLOCK))], axis=0)
    return on.T, lb.T


def _run_units(units):
    state = {}
    for i in range(len(units) + LOOKAHEAD):
        if i < len(units):
            state[i] = units[i][0]()
        k = i - LOOKAHEAD
        if k >= 0:
            units[k][1](state.pop(k))


class _UnitGroups:
    def __init__(self, units):
        self.units, self.next, self.pending = units, 0, []

    def gap(self, n_new):
        for k, m in self.pending:
            self.units[k][1](m)
        self.pending = []
        for k in range(self.next, min(self.next + n_new, len(self.units))):
            self.pending.append((k, self.units[k][0]()))
        self.next += n_new


def _stage1_kernel(x_ref, g1_ref, wgu_ref, wd_ref, gmix_ref, wqkv_ref, bqkv_ref, bias_ref, sink_ref,
                   h1_ref, u2_ref, qkv1_ref, qkv2_ref, yb_ref, o0_ref, l0_ref,
                   zs_ref, cur0_ref, curb_ref, prev0_ref, prevb_ref, s_ref, p_ref, *, tiles_per_seq):
    i = pl.program_id(0)
    par = lax.rem(i, 2)
    opar = 1 - par
    opens_seq = lax.rem(i + tiles_per_seq - 1, tiles_per_seq) == 0
    tm = x_ref.shape[0]
    n_pair = A_OUT // LANES
    n_qp = B_OUT // LANES

    @pl.when(i == 0)
    def _():
        cur0_ref[...] = jnp.zeros_like(cur0_ref)
        curb_ref[...] = jnp.zeros_like(curb_ref)
        prev0_ref[...] = jnp.zeros_like(prev0_ref)
        prevb_ref[...] = jnp.zeros_like(prevb_ref)

    def cols(base, j):
        return slice(base + j * LANES, base + (j + 1) * LANES)

    def nat_units(c, slot0):
        rows = slice(c * BLOCK, (c + 1) * BLOCK)
        variant = jnp.where(opens_seq, 1, 0) if c == 0 else 0

        def keys(cur_ref, prev_ref, cs):
            if c == 0:
                return jnp.concatenate([prev_ref[opar, :, cs], cur_ref[opar, rows, cs]], axis=0)
            return cur_ref[opar, (c - 1) * BLOCK:(c + 1) * BLOCK, cs]

        units = []
        for j in range(n_pair):
            scr = (s_ref, p_ref, (slot0 + j) % N_SLOTS)

            def first_pass(j=j, scr=scr):
                q2 = cur0_ref[opar, rows, cols(0, j)]
                return _scores(scr, q2, q2, 0, 1, keys(cur0_ref, prev0_ref, cols(A_OUT, j)),
                               bias_ref[variant, 2 * j], bias_ref[variant, 2 * j + 1])

            def second_pass(m, j=j, scr=scr):
                o_t, l = _softmax_pv(scr, m, keys(cur0_ref, prev0_ref, cols(2 * A_OUT, j)))
                o0_ref[j, rows, :], l0_ref[j, rows, :] = _pair_finish(o_t, m, l)

            units.append((first_pass, second_pass))
        done = {}
        for half in range(2):
            for u in range(n_qp // 2):
                scr = (s_ref, p_ref, (slot0 + n_pair + half * (n_qp // 2) + u) % N_SLOTS)

                def first_pass(half=half, u=u, scr=scr):
                    ha = A_HEADS_PER_GROUP + n_qp * half + 2 * u
                    sink_row = sink_ref[half, :, 2 * u * BLOCK:(2 * u + 2) * BLOCK]
                    return _scores(scr, curb_ref[opar, rows, cols(0, 2 * u)], curb_ref[opar, rows, cols(0, 2 * u + 1)],
                                   half, half, keys(curb_ref, prevb_ref, slice(B_OUT, B_OUT + B_KV_W)),
                                   bias_ref[variant, ha], bias_ref[variant, ha + 1], sink_row=sink_row)

                def second_pass(m, half=half, u=u, scr=scr):
                    sink_row = sink_ref[half, :, 2 * u * BLOCK:(2 * u + 2) * BLOCK]
                    vc = keys(curb_ref, prevb_ref, slice(B_OUT + B_KV_W, B_OUT + 2 * B_KV_W))
                    o_t, l = _softmax_pv(scr, m, vc, sink_row=sink_row)
                    done[half, u] = o_t[half * HEAD_DIM:(half + 1) * HEAD_DIM] / l
                    if half == 1:
                        for jj in range(2):
                            j = 2 * u + jj
                            tile_t = jnp.concatenate([done[0, u][:, jj * BLOCK:(jj + 1) * BLOCK],
                                                      done[1, u][:, jj * BLOCK:(jj + 1) * BLOCK]], axis=0)
                            yb_ref[rows, cols(0, j)] = tile_t.T.astype(_BF16)

                units.append((first_pass, second_pass))
        return units

    units = []
    for c in range(tm // BLOCK):
        units += nat_units(c, c * (n_pair + n_qp))
    att = _UnitGroups(units)

    subs = [slice(k * SUB_TILE, (k + 1) * SUB_TILE) for k in range(tm // SUB_TILE)]
    xs = [x_ref[sl, :] for sl in subs]
    xbs = [_rms(x, g1_ref[...]).astype(_BF16) for x in xs]
    acts = [_gate_up(xb, wgu_ref, GU_CHUNKS, lambda: att.gap(UNITS_PER_GU_GAP)) for xb in xbs]
    ys = [_dot_chunks(a, wd_ref, D_CHUNKS, lambda: att.gap(UNITS_PER_D_GAP)) for a in acts]
    u2s = []
    for sl, x, y in zip(subs, xs, ys):
        h1 = x + 0.5 * y
        h1_ref[sl, :] = h1
        u2 = _rms(h1, gmix_ref[...]).astype(_BF16)
        u2_ref[sl, :] = u2
        u2s.append(u2)
    zs = [_dot_chunks(u2, wqkv_ref, Z_CHUNKS, lambda: att.gap(UNITS_PER_Z_GAP)) + bqkv_ref[...] for u2 in u2s]
    att.gap(0)
    assert att.next >= len(units)
    prev0_ref[par] = cur0_ref[opar, tm - BLOCK:, :]
    prevb_ref[par] = curb_ref[opar, tm - BLOCK:, :]
    n_slab = GROUP_W // LANES
    for k, (sl, z) in enumerate(zip(subs, zs)):
        cur0_ref[par, sl, :] = z[:, :GROUP_W].astype(_BF16)
        curb_ref[par, sl, :] = z[:, 3 * GROUP_W:].astype(_BF16)
        for j in range(2 * n_slab):
            zs_ref[k, j] = z[:, GROUP_W + j * LANES:GROUP_W + (j + 1) * LANES]
        for gi, out_ref in ((1, qkv1_ref), (2, qkv2_ref)):
            d = DIL_GROUPS[gi][1]
            n = SUB_TILE // d
            for j in range(n_slab):
                for r in range(d):
                    rows = zs_ref[k, (gi - 1) * n_slab + j, pl.ds(r, n, stride=d), :]
                    out_ref[0, r, k * n:(k + 1) * n, j * LANES:(j + 1) * LANES] = rows.astype(_BF16)


def _stage2_kernel(qkv1_ref, qkv2_ref, o0_ref, l0_ref, bias_ref,
                   ya_ref, y_ref, o2_ref, l2_ref, s_ref, p_ref):
    n_pair = A_OUT // LANES
    seq = ya_ref.shape[0]

    def cols(base, j):
        return slice(base + j * LANES, base + (j + 1) * LANES)

    def bias_pair(h, first):
        if first:
            return bias_ref[h, BLOCK:, :], bias_ref[h + 1, BLOCK:, :]
        return bias_ref[h], bias_ref[h + 1]

    def pair_unit(slot, load, h, first, finish):
        scr = (s_ref, p_ref, slot)

        def first_pass():
            q2 = load(0, True)
            return _scores(scr, q2, q2, 0, 1, load(A_OUT, False), *bias_pair(h, first))

        def second_pass(m):
            o_t, l = _softmax_pv(scr, m, load(2 * A_OUT, False))
            finish(*_pair_finish(o_t, m, l))

        return first_pass, second_pass

    d2 = DIL_GROUPS[2][1]

    def g2_body(it, carry):
        units = []
        for c in range(G2_CLASSES_PER_STEP):
            r = it * G2_CLASSES_PER_STEP + c
            for j in range(n_pair):
                def load(base, is_q, r=r, j=j):
                    return qkv2_ref[0, r, :, cols(base, j)]

                def finish(out, lse2, r=r, j=j):
                    o2_ref[j, pl.ds(r, BLOCK, stride=d2), :] = out
                    l2_ref[j, pl.ds(r, BLOCK, stride=d2), :] = lse2

                units.append(pair_unit(c * n_pair + j, load, 8 + 2 * j, True, finish))
        _run_units(units)
        return carry

    lax.fori_loop(0, d2 // G2_CLASSES_PER_STEP, g2_body, 0)

    d1 = DIL_GROUPS[1][1]
    nb1 = seq // d1 // BLOCK

    def g1_body(r, carry):
        units = []
        for b in range(nb1):
            first = b == 0
            row0 = b * BLOCK
            k0 = 0 if first else (b - 1) * BLOCK
            kn = BLOCK if first else 2 * BLOCK
            for j in range(n_pair):
                def load(base, is_q, j=j, row0=row0, k0=k0, kn=kn):
                    return qkv1_ref[0, r, pl.ds(row0, BLOCK) if is_q else pl.ds(k0, kn), cols(base, j)]

                def finish(o1, l1, j=j, row0=row0):
                    tok = pl.ds(row0 * d1 + r, BLOCK, stride=d1)
                    o0, l0 = o0_ref[j, tok, :], l0_ref[j, tok, :]
                    o2, l2 = o2_ref[j, tok, :], l2_ref[j, tok, :]
                    top = jnp.maximum(jnp.maximum(l0, l1), l2)
                    w0, w1, w2 = jnp.exp(l0 - top), jnp.exp(l1 - top), jnp.exp(l2 - top)
                    y_ref[j, tok, :] = (w0 * o0 + w1 * o1 + w2 * o2) / (w0 + w1 + w2)

                units.append(pair_unit(b * n_pair + j, load, 4 + 2 * j, first, finish))
        _run_units(units)
        return carry

    lax.fori_loop(0, d1, g1_body, 0)
    for j in range(n_pair):
        ya_ref[:, cols(0, j)] = y_ref[j].astype(_BF16)


def _stage3_kernel(h1_ref, u2_ref, ya_ref, yb_ref, wgate_ref, bgate_ref, wa_ref, wb_ref, wo_ref,
                   g2_ref, wgu_ref, wd_ref, gf_ref, out_ref):
    d_model = h1_ref.shape[1]
    subs = [slice(i * SUB_TILE, (i + 1) * SUB_TILE) for i in range(h1_ref.shape[0] // SUB_TILE)]
    pre = [(_dot(u2_ref[sl, :], wgate_ref[...]), _dot(ya_ref[sl, :], wa_ref[...]), _dot(yb_ref[sl, :], wb_ref[...]))
           for sl in subs]
    merged = []
    for gl, pa, pb in pre:
        gates = jax.nn.sigmoid(gl + bgate_ref[...])
        merged.append((gates[:, :d_model] * pa + gates[:, d_model:] * pb).astype(_BF16))
    h2s = [h1_ref[sl, :] + _dot(mg, wo_ref[...]) for sl, mg in zip(subs, merged)]
    acts = [_gate_up(_rms(h2, g2_ref[...]).astype(_BF16), wgu_ref) for h2 in h2s]
    ys = [_dot(a, wd_ref[...]) for a in acts]
    for sl, h2, y in zip(subs, h2s, ys):
        out_ref[sl, :] = _rms(h2 + 0.5 * y, gf_ref[...])


def _t5_bucket_const(dist):
    max_exact = N_BUCKETS // 2
    n = jnp.maximum(dist, 0)
    nf = jnp.maximum(n, 1).astype(jnp.float32)
    large = max_exact + (jnp.log(nf / max_exact) / math.log(MAX_DISTANCE / max_exact)
                         * (N_BUCKETS - max_exact)).astype(jnp.int32)
    large = jnp.minimum(large, N_BUCKETS - 1)
    return jnp.where(n < max_exact, n, large)


def _bias_kernel(table_ref, idx_ref, out_ref):
    v = pl.program_id(0)
    h = pl.program_id(1)
    idx = idx_ref[0]
    acc = jnp.full(idx.shape, MASKED, _F32)
    for b in range(N_BUCKETS):
        acc = jnp.where(idx == b, table_ref[b, h] * LOG2E, acc)
    prev_key = lax.broadcasted_iota(jnp.int32, idx.shape, 0) < BLOCK
    out_ref[0, 0] = jnp.where(jnp.logical_and(prev_key, v == 1), MASKED, acc)


def _band_bias(rel_bias):
    ki = jnp.arange(2 * BLOCK)[:, None]
    qi = jnp.arange(BLOCK)[None, :]
    dist = qi + BLOCK - ki
    variants = [(d, w // d) for w, d in DIL_GROUPS] + [(1, B_WINDOW - 1)]
    idx = jnp.stack([jnp.where((dist >= 0) & (dist <= max_steps), _t5_bucket_const(dist * stride), -1)
                     for stride, max_steps in variants]).astype(jnp.int32)
    n_heads = rel_bias.shape[1]
    last = len(variants) - 1
    return pl.pallas_call(
        _bias_kernel,
        grid=(2, n_heads),
        in_specs=[pl.BlockSpec(memory_space=pltpu.SMEM),
                  pl.BlockSpec((1, 2 * BLOCK, BLOCK),
                               lambda v, h: (jnp.minimum(h // A_HEADS_PER_GROUP, last), 0, 0))],
        out_specs=pl.BlockSpec((1, 1, 2 * BLOCK, BLOCK), lambda v, h: (v, h, 0, 0)),
        out_shape=jax.ShapeDtypeStruct((2, n_heads, 2 * BLOCK, BLOCK), _F32),
        name="band_bias",
    )(rel_bias.astype(_F32), idx)


def _const_spec(shape):
    nd = len(shape)
    return pl.BlockSpec(shape, lambda *_: (0,) * nd, pipeline_mode=pl.Buffered(1))


def _interleave_gate_up(w_gate, w_up):
    d_model, d_ff = w_gate.shape
    gu = jnp.stack([w_gate.reshape(d_model, d_ff // MXU_COLS, MXU_COLS),
                    w_up.reshape(d_model, d_ff // MXU_COLS, MXU_COLS)], axis=2)
    return gu.reshape(d_model, 2 * d_ff).astype(_BF16)


def _layer(x, ffn1_norm, ffn1_w_gate, ffn1_w_up, ffn1_w_down, mix_norm, w_in, b_in,
           w_branch_a, w_branch_b, w_out, sinks, rel_bias,
           ffn2_norm, ffn2_w_gate, ffn2_w_up, ffn2_w_down, final_norm):
    batch, seq, d_model = x.shape
    tokens = batch * seq
    tm = TOKEN_TILE
    tiles_per_seq = seq // tm
    n_tiles = tokens // tm
    d_ff = ffn1_w_gate.shape[1]
    a_w = A_HEADS * HEAD_DIM
    scale = HEAD_DIM ** -0.5 * LOG2E

    aq, ak, av = w_in[:, :a_w], w_in[:, a_w:2 * a_w], w_in[:, 2 * a_w:3 * a_w]
    baq, bak, bav = b_in[:a_w], b_in[a_w:2 * a_w], b_in[2 * a_w:3 * a_w]
    off = 3 * a_w
    bq_w, bq_b = w_in[:, off:off + B_OUT], b_in[off:off + B_OUT]
    bkv_w, bkv_b = w_in[:, off + B_OUT:off + B_OUT + 2 * B_KV_W], b_in[off + B_OUT:off + B_OUT + 2 * B_KV_W]
    gate_w, gate_b = w_in[:, off + B_W:], b_in[off + B_W:]
    perm = np.concatenate([np.r_[j * HEAD_DIM:(j + 1) * HEAD_DIM,
                                 (j + 4) * HEAD_DIM:(j + 5) * HEAD_DIM] for j in range(B_Q_HEADS // 2)])
    w_cols, b_cols = [], []
    for gi in range(len(DIL_GROUPS)):
        sl = slice(gi * A_OUT, (gi + 1) * A_OUT)
        w_cols += [aq[:, sl] * scale, ak[:, sl], av[:, sl]]
        b_cols += [baq[sl] * scale, bak[sl], bav[sl]]
    w_cols += [bq_w[:, perm] * scale, bkv_w]
    b_cols += [bq_b[perm] * scale, bkv_b]
    wqkv = jnp.concatenate(w_cols, axis=1).astype(_BF16)
    bqkv = jnp.concatenate(b_cols)[None, :].astype(_F32)
    wb_perm = w_branch_b[perm, :].astype(_BF16)

    row = lambda v: v[None, :].astype(_F32)
    x2 = x.reshape(tokens, d_model)

    bias = _band_bias(rel_bias)
    bias_nat = jnp.concatenate([bias[:, :A_HEADS_PER_GROUP], bias[:, A_HEADS:]], axis=1)
    sink_rows = jnp.repeat((sinks.astype(_F32) * LOG2E).reshape(B_KV_HEADS, B_Q_HEADS // B_KV_HEADS), BLOCK,
                           axis=1)[:, None, :]

    cur = lambda i: jnp.minimum(i, n_tiles - 1)
    prv = lambda i: jnp.maximum(i - 1, 0)
    d1, d2 = DIL_GROUPS[1][1], DIL_GROUPS[2][1]
    n_sub = tm // SUB_TILE
    h1, u2, qkv1, qkv2, yb, o0, l0 = pl.pallas_call(
        functools.partial(_stage1_kernel, tiles_per_seq=tiles_per_seq),
        grid=(n_tiles + 1,),
        in_specs=[pl.BlockSpec((tm, d_model), lambda i: (cur(i), 0)), _const_spec((1, d_model)),
                  _const_spec((d_model, 2 * d_ff)), _const_spec((d_ff, d_model)),
                  _const_spec((1, d_model)), _const_spec((d_model, 3 * GROUP_W + B_W)),
                  _const_spec((1, 3 * GROUP_W + B_W)),
                  _const_spec(bias_nat.shape), _const_spec(sink_rows.shape)],
        out_specs=[pl.BlockSpec((tm, d_model), lambda i: (cur(i), 0)),
                   pl.BlockSpec((tm, d_model), lambda i: (cur(i), 0)),
                   pl.BlockSpec((1, d1, tm // d1, GROUP_W),
                                lambda i: (cur(i) // tiles_per_seq, 0, cur(i) % tiles_per_seq, 0)),
                   pl.BlockSpec((1, d2, tm // d2, GROUP_W),
                                lambda i: (cur(i) // tiles_per_seq, 0, cur(i) % tiles_per_seq, 0)),
                   pl.BlockSpec((tm, B_OUT), lambda i: (prv(i), 0)),
                   pl.BlockSpec((A_OUT // LANES, tm, LANES), lambda i: (0, prv(i), 0)),
                   pl.BlockSpec((A_OUT // LANES, tm, LANES), lambda i: (0, prv(i), 0))],
        out_shape=[jax.ShapeDtypeStruct((tokens, d_model), _F32),
                   jax.ShapeDtypeStruct((tokens, d_model), _BF16),
                   jax.ShapeDtypeStruct((batch, d1, seq // d1, GROUP_W), _BF16),
                   jax.ShapeDtypeStruct((batch, d2, seq // d2, GROUP_W), _BF16),
                   jax.ShapeDtypeStruct((tokens, B_OUT), _BF16),
                   jax.ShapeDtypeStruct((A_OUT // LANES, tokens, LANES), _F32),
                   jax.ShapeDtypeStruct((A_OUT // LANES, tokens, LANES), _F32)],
        scratch_shapes=[pltpu.VMEM((n_sub, 2 * GROUP_W // LANES, SUB_TILE, LANES), _F32),
                        pltpu.VMEM((2, tm, GROUP_W), _BF16), pltpu.VMEM((2, tm, B_W), _BF16),
                        pltpu.VMEM((2, BLOCK, GROUP_W), _BF16), pltpu.VMEM((2, BLOCK, B_W), _BF16),
                        pltpu.VMEM((N_SLOTS, 2 * BLOCK, 2 * BLOCK), _F32),
                        pltpu.VMEM((N_SLOTS, 2 * BLOCK, 2 * BLOCK), _BF16)],
        compiler_params=pltpu.CompilerParams(dimension_semantics=("arbitrary",), vmem_limit_bytes=VMEM_LIMIT),
        name="stage1_ffn_qkv_attn",
    )(x2, row(ffn1_norm), _interleave_gate_up(ffn1_w_gate, ffn1_w_up), ffn1_w_down.astype(_BF16),
      row(mix_norm), wqkv, bqkv, bias_nat, sink_rows)

    params = pltpu.CompilerParams(dimension_semantics=("parallel",), vmem_limit_bytes=VMEM_LIMIT)

    n_heads = bias.shape[1]
    ya = pl.pallas_call(
        _stage2_kernel,
        grid=(batch,),
        in_specs=[pl.BlockSpec((1, d1, seq // d1, GROUP_W), lambda b: (b, 0, 0, 0)),
                  pl.BlockSpec((1, d2, seq // d2, GROUP_W), lambda b: (b, 0, 0, 0)),
                  pl.BlockSpec((A_OUT // LANES, seq, LANES), lambda b: (0, b, 0)),
                  pl.BlockSpec((A_OUT // LANES, seq, LANES), lambda b: (0, b, 0)),
                  pl.BlockSpec((None, n_heads, 2 * BLOCK, BLOCK), lambda b: (0, 0, 0, 0),
                               pipeline_mode=pl.Buffered(1))],
        out_specs=pl.BlockSpec((seq, A_OUT), lambda b: (b, 0)),
        out_shape=jax.ShapeDtypeStruct((tokens, A_OUT), _BF16),
        scratch_shapes=[pltpu.VMEM((A_OUT // LANES, seq, LANES), _F32) for _ in range(3)]
        + [pltpu.VMEM((N_SLOTS, 2 * BLOCK, 2 * BLOCK), _F32), pltpu.VMEM((N_SLOTS, 2 * BLOCK, 2 * BLOCK), _BF16)],
        compiler_params=params,
        name="stage2_dilated_attention",
    )(qkv1, qkv2, o0, l0, bias)

    tile = lambda w: pl.BlockSpec((tm, w), lambda i: (i, 0))
    out = pl.pallas_call(
        _stage3_kernel,
        grid=(n_tiles,),
        in_specs=[tile(d_model), tile(d_model), tile(A_OUT), tile(B_OUT),
                  _const_spec((d_model, 2 * d_model)), _const_spec((1, 2 * d_model)),
                  _const_spec((A_OUT, d_model)), _const_spec((B_OUT, d_model)), _const_spec((d_model, d_model)),
                  _const_spec((1, d_model)),
                  _const_spec((d_model, 2 * d_ff)), _const_spec((d_ff, d_model)),
                  _const_spec((1, d_model))],
        out_specs=tile(d_model),
        out_shape=jax.ShapeDtypeStruct((tokens, d_model), _F32),
        compiler_params=params,
        name="stage3_merge_ffn",
    )(h1, u2, ya, yb, gate_w.astype(_BF16), row(gate_b), w_branch_a.astype(_BF16), wb_perm,
      w_out.astype(_BF16), row(ffn2_norm), _interleave_gate_up(ffn2_w_gate, ffn2_w_up),
      ffn2_w_down.astype(_BF16), row(final_norm))
    return out.reshape(batch, seq, d_model)


def kernel(x, ffn1_norm, ffn1_w_gate, ffn1_w_up, ffn1_w_down, mix_norm, w_in, b_in, w_branch_a, w_branch_b,
           w_out, sinks, rel_bias, ffn2_norm, ffn2_w_gate, ffn2_w_up, ffn2_w_down, final_norm):
    assert ffn1_norm.shape[0] == 1, "single-layer kernel"
    return _layer(x, ffn1_norm[0], ffn1_w_gate[0], ffn1_w_up[0], ffn1_w_down[0], mix_norm[0], w_in[0], b_in[0],
                  w_branch_a[0], w_branch_b[0], w_out[0], sinks[0], rel_bias,
                  ffn2_norm[0], ffn2_w_gate[0], ffn2_w_up[0], ffn2_w_down[0], final_norm)
```

```python
import functools
import math

import numpy as np
import jax
import jax.numpy as jnp
from jax import lax
from jax.experimental import pallas as pl
from jax.experimental.pallas import tpu as pltpu

HEAD_DIM = 64
DIL_GROUPS = ((128, 1), (512, 4), (2048, 16))
A_HEADS_PER_GROUP = 4
A_HEADS = A_HEADS_PER_GROUP * len(DIL_GROUPS)
A_OUT = A_HEADS_PER_GROUP * HEAD_DIM
B_Q_HEADS = 8
B_KV_HEADS = 2
B_WINDOW = 128
B_OUT = B_Q_HEADS * HEAD_DIM
B_KV_W = B_KV_HEADS * HEAD_DIM
BLOCK = 128
N_BUCKETS = 32
MAX_DISTANCE = 2048
EPS = 1e-6
GROUP_W = 3 * A_OUT
B_W = B_OUT + 2 * B_KV_W
LANES = 128
SUBLANES = 8
MXU_COLS = 256
FF_CHUNKS = (768, 768, 768, 512)
LOG2E = math.log2(math.e)
MASKED = -1e30

TOKEN_TILE = 512
SUB_TILE = 256
VMEM_LIMIT = 60 * 1024 * 1024

_BF16 = jnp.bfloat16
_F32 = jnp.float32


def _rms(x, g):
    return (x * lax.rsqrt(jnp.mean(x * x, axis=-1, keepdims=True) + EPS)) * g


def _dot(a, b):
    return jnp.dot(a, b, preferred_element_type=_F32)


def _gate_up(xb, wg_ref, wu_ref, gap=None):
    acts = []
    c0 = 0
    for w in FF_CHUNKS:
        if gap is not None:
            gap()
        g = _dot(xb, wg_ref[:, c0:c0 + w])
        u = _dot(xb, wu_ref[:, c0:c0 + w])
        c0 += w
        acts.append(((g * jax.nn.sigmoid(g)) * u).astype(_BF16))
    return jnp.concatenate(acts, axis=1)


def _dot_chunks(a, w_ref, chunks, gap):
    outs = []
    c0 = 0
    for w in chunks:
        gap()
        outs.append(_dot(a, w_ref[:, c0:c0 + w]))
        c0 += w
    return jnp.concatenate(outs, axis=1)


KEY_CHUNK = 32
LOOKAHEAD = 4
G2_CLASSES_PER_STEP = 4
N_SLOTS = 8
D_CHUNKS = (512, 512)
Z_CHUNKS = (768, 768, 768, 768)
UNITS_PER_GU_GAP, UNITS_PER_D_GAP, UNITS_PER_Z_GAP = 1, 2, 1


def _scores(scr, q2a, q2b, half_a, half_b, kc, bias_a, bias_b, sink_row=None):
    s_ref, _, slot = scr
    kn = kc.shape[0]
    lane = lax.broadcasted_iota(jnp.int32, q2a.shape, 1)
    keep = lambda q2, half: jnp.where((lane < HEAD_DIM) if half == 0 else (lane >= HEAD_DIM), q2, jnp.zeros_like(q2))
    qm = jnp.concatenate([keep(q2a, half_a), keep(q2b, half_b)], axis=0)
    s = lax.dot_general(kc, qm, (((1,), (1,)), ((), ())), preferred_element_type=_F32)
    s = s + jnp.concatenate([bias_a, bias_b], axis=1)
    s_ref[slot, :kn, :] = s
    m = jnp.max(s, axis=0, keepdims=True)
    if sink_row is not None:
        m = jnp.maximum(m, sink_row)
    return m


def _softmax_pv(scr, m, vc, sink_row=None):
    s_ref, p_ref, slot = scr
    kn = vc.shape[0]
    acc = jnp.zeros((SUBLANES, m.shape[1]), _F32)
    for c in range(kn // KEY_CHUNK):
        rows = slice(c * KEY_CHUNK, (c + 1) * KEY_CHUNK)
        p = jnp.exp2(s_ref[slot, rows, :] - m)
        for t in range(KEY_CHUNK // SUBLANES):
            acc = acc + p[t * SUBLANES:(t + 1) * SUBLANES]
        p_ref[slot, rows, :] = p.astype(_BF16)
    l = jnp.sum(acc, axis=0, keepdims=True)
    if sink_row is not None:
        l = l + jnp.exp2(sink_row - m)
    o_t = lax.dot_general(vc, p_ref[slot, :kn, :], (((0,), (0,)), ((), ())), preferred_element_type=_F32)
    return o_t, l


def _pair_finish(o_t, m, l):
    lse = (m + jnp.log2(l)) * (1.0 / LOG2E)
    on = jnp.concatenate([o_t[:HEAD_DIM, :BLOCK] / l[:, :BLOCK], o_t[HEAD_DIM:, BLOCK:] / l[:, BLOCK:]], axis=0)
    lb = jnp.concatenate([jnp.broadcast_to(lse[:, :BLOCK], (HEAD_DIM, BLOCK)),
                          jnp.broadcast_to(lse[:, BLOCK:], (HEAD_DIM, BLOCK))], axis=0)
    return on.T, lb.T


def _run_units(units):
    state = {}
    for i in range(len(units) + LOOKAHEAD):
        if i < len(units):
            state[i] = units[i][0]()
        k = i - LOOKAHEAD
        if k >= 0:
            units[k][1](state.pop(k))


class _UnitGroups:
    def __init__(self, units):
        self.units, self.next, self.pending = units, 0, []

    def gap(self, n_new):
        for k, m in self.pending:
            self.units[k][1](m)
        self.pending = []
        for k in range(self.next, min(self.next + n_new, len(self.units))):
            self.pending.append((k, self.units[k][0]()))
        self.next += n_new


def _stage1_kernel(x_ref, g1_ref, wg_ref, wu_ref, wd_ref, gmix_ref, wqkv_ref, bqkv_ref, bias_ref, sink_ref,
                   h1_ref, u2_ref, qkv1_ref, qkv2_ref, yb_ref, o0_ref, l0_ref,
                   zs_ref, cur0_ref, curb_ref, prev0_ref, prevb_ref, s_ref, p_ref, *, tiles_per_seq):
    i = pl.program_id(0)
    par = lax.rem(i, 2)
    opar = 1 - par
    opens_seq = lax.rem(i + tiles_per_seq - 1, tiles_per_seq) == 0
    tm = x_ref.shape[0]
    n_pair = A_OUT // LANES
    n_qp = B_OUT // LANES

    @pl.when(i == 0)
    def _():
        cur0_ref[...] = jnp.zeros_like(cur0_ref)
        curb_ref[...] = jnp.zeros_like(curb_ref)
        prev0_ref[...] = jnp.zeros_like(prev0_ref)
        prevb_ref[...] = jnp.zeros_like(prevb_ref)

    def cols(base, j):
        return slice(base + j * LANES, base + (j + 1) * LANES)

    def nat_units(c, slot0):
        rows = slice(c * BLOCK, (c + 1) * BLOCK)
        variant = jnp.where(opens_seq, 1, 0) if c == 0 else 0

        def keys(cur_ref, prev_ref, cs):
            if c == 0:
                return jnp.concatenate([prev_ref[opar, :, cs], cur_ref[opar, rows, cs]], axis=0)
            return cur_ref[opar, (c - 1) * BLOCK:(c + 1) * BLOCK, cs]

        units = []
        for j in range(n_pair):
            scr = (s_ref, p_ref, (slot0 + j) % N_SLOTS)

            def first_pass(j=j, scr=scr):
                q2 = cur0_ref[opar, rows, cols(0, j)]
                return _scores(scr, q2, q2, 0, 1, keys(cur0_ref, prev0_ref, cols(A_OUT, j)),
                               bias_ref[variant, 2 * j], bias_ref[variant, 2 * j + 1])

            def second_pass(m, j=j, scr=scr):
                o_t, l = _softmax_pv(scr, m, keys(cur0_ref, prev0_ref, cols(2 * A_OUT, j)))
                o0_ref[j, rows, :], l0_ref[j, rows, :] = _pair_finish(o_t, m, l)

            units.append((first_pass, second_pass))
        done = {}
        for half in range(2):
            for u in range(n_qp // 2):
                scr = (s_ref, p_ref, (slot0 + n_pair + half * (n_qp // 2) + u) % N_SLOTS)

                def first_pass(half=half, u=u, scr=scr):
                    ha = A_HEADS_PER_GROUP + n_qp * half + 2 * u
                    sink_row = sink_ref[half, :, 2 * u * BLOCK:(2 * u + 2) * BLOCK]
                    return _scores(scr, curb_ref[opar, rows, cols(0, 2 * u)], curb_ref[opar, rows, cols(0, 2 * u + 1)],
                                   half, half, keys(curb_ref, prevb_ref, slice(B_OUT, B_OUT + B_KV_W)),
                                   bias_ref[variant, ha], bias_ref[variant, ha + 1], sink_row=sink_row)

                def second_pass(m, half=half, u=u, scr=scr):
                    sink_row = sink_ref[half, :, 2 * u * BLOCK:(2 * u + 2) * BLOCK]
                    vc = keys(curb_ref, prevb_ref, slice(B_OUT + B_KV_W, B_OUT + 2 * B_KV_W))
                    o_t, l = _softmax_pv(scr, m, vc, sink_row=sink_row)
                    done[half, u] = o_t[half * HEAD_DIM:(half + 1) * HEAD_DIM] / l
                    if half == 1:
                        for jj in range(2):
                            j = 2 * u + jj
                            tile_t = jnp.concatenate([done[0, u][:, jj * BLOCK:(jj + 1) * BLOCK],
                                                      done[1, u][:, jj * BLOCK:(jj + 1) * BLOCK]], axis=0)
                            yb_ref[rows, cols(0, j)] = tile_t.T.astype(_BF16)

                units.append((first_pass, second_pass))
        return units

    units = []
    for c in range(tm // BLOCK):
        units += nat_units(c, c * (n_pair + n_qp))
    att = _UnitGroups(units)

    subs = [slice(k * SUB_TILE, (k + 1) * SUB_TILE) for k in range(tm // SUB_TILE)]
    xs = [x_ref[sl, :] for sl in subs]
    xbs = [_rms(x, g1_ref[...]).astype(_BF16) for x in xs]
    acts = [_gate_up(xb, wg_ref, wu_ref, lambda: att.gap(UNITS_PER_GU_GAP)) for xb in xbs]
    ys = [_dot_chunks(a, wd_ref, D_CHUNKS, lambda: att.gap(UNITS_PER_D_GAP)) for a in acts]
    u2s = []
    for sl, x, y in zip(subs, xs, ys):
        h1 = x + 0.5 * y
        h1_ref[sl, :] = h1
        u2 = _rms(h1, gmix_ref[...]).astype(_BF16)
        u2_ref[sl, :] = u2
        u2s.append(u2)
    zs = [_dot_chunks(u2, wqkv_ref, Z_CHUNKS, lambda: att.gap(UNITS_PER_Z_GAP)) + bqkv_ref[...] for u2 in u2s]
    att.gap(0)
    assert att.next >= len(units)
    prev0_ref[par] = cur0_ref[opar, tm - BLOCK:, :]
    prevb_ref[par] = curb_ref[opar, tm - BLOCK:, :]
    n_slab = GROUP_W // LANES
    for k, (sl, z) in enumerate(zip(subs, zs)):
        cur0_ref[par, sl, :] = z[:, :GROUP_W].astype(_BF16)
        curb_ref[par, sl, :] = z[:, 3 * GROUP_W:].astype(_BF16)
        for j in range(2 * n_slab):
            zs_ref[k, j] = z[:, GROUP_W + j * LANES:GROUP_W + (j + 1) * LANES]
        for gi, out_ref in ((1, qkv1_ref), (2, qkv2_ref)):
            d = DIL_GROUPS[gi][1]
            n = SUB_TILE // d
            for j in range(n_slab):
                for r in range(d):
                    rows = zs_ref[k, (gi - 1) * n_slab + j, pl.ds(r, n, stride=d), :]
                    out_ref[0, r, k * n:(k + 1) * n, j * LANES:(j + 1) * LANES] = rows.astype(_BF16)


def _stage2_kernel(qkv1_ref, qkv2_ref, o0_ref, l0_ref, bias_ref,
                   ya_ref, y_ref, o2_ref, l2_ref, s_ref, p_ref):
    n_pair = A_OUT // LANES
    seq = ya_ref.shape[0]

    def cols(base, j):
        return slice(base + j * LANES, base + (j + 1) * LANES)

    def bias_pair(h, first):
        if first:
            return bias_ref[h, BLOCK:, :], bias_ref[h + 1, BLOCK:, :]
        return bias_ref[h], bias_ref[h + 1]

    def pair_unit(slot, load, h, first, finish):
        scr = (s_ref, p_ref, slot)

        def first_pass():
            q2 = load(0, True)
            return _scores(scr, q2, q2, 0, 1, load(A_OUT, False), *bias_pair(h, first))

        def second_pass(m):
            o_t, l = _softmax_pv(scr, m, load(2 * A_OUT, False))
            finish(*_pair_finish(o_t, m, l))

        return first_pass, second_pass

    d2 = DIL_GROUPS[2][1]

    def g2_body(it, carry):
        units = []
        for c in range(G2_CLASSES_PER_STEP):
            r = it * G2_CLASSES_PER_STEP + c
            for j in range(n_pair):
                def load(base, is_q, r=r, j=j):
                    return qkv2_ref[0, r, :, cols(base, j)]

                def finish(out, lse2, r=r, j=j):
                    o2_ref[j, pl.ds(r, BLOCK, stride=d2), :] = out
                    l2_ref[j, pl.ds(r, BLOCK, stride=d2), :] = lse2

                units.append(pair_unit(c * n_pair + j, load, 8 + 2 * j, True, finish))
        _run_units(units)
        return carry

    lax.fori_loop(0, d2 // G2_CLASSES_PER_STEP, g2_body, 0)

    d1 = DIL_GROUPS[1][1]
    nb1 = seq // d1 // BLOCK

    def g1_body(r, carry):
        units = []
        for b in range(nb1):
            first = b == 0
            row0 = b * BLOCK
            k0 = 0 if first else (b - 1) * BLOCK
            kn = BLOCK if first else 2 * BLOCK
            for j in range(n_pair):
                def load(base, is_q, j=j, row0=row0, k0=k0, kn=kn):
                    return qkv1_ref[0, r, pl.ds(row0, BLOCK) if is_q else pl.ds(k0, kn), cols(base, j)]

                def finish(o1, l1, j=j, row0=row0):
                    tok = pl.ds(row0 * d1 + r, BLOCK, stride=d1)
                    o0, l0 = o0_ref[j, tok, :], l0_ref[j, tok, :]
                    o2, l2 = o2_ref[j, tok, :], l2_ref[j, tok, :]
                    top = jnp.maximum(jnp.maximum(l0, l1), l2)
                    w0, w1, w2 = jnp.exp(l0 - top), jnp.exp(l1 - top), jnp.exp(l2 - top)
                    y_ref[j, tok, :] = (w0 * o0 + w1 * o1 + w2 * o2) / (w0 + w1 + w2)

                units.append(pair_unit(b * n_pair + j, load, 4 + 2 * j, first, finish))
        _run_units(units)
        return carry

    lax.fori_loop(0, d1, g1_body, 0)
    for j in range(n_pair):
        ya_ref[:, cols(0, j)] = y_ref[j].astype(_BF16)


def _stage3_kernel(h1_ref, u2_ref, ya_ref, yb_ref, wgate_ref, bgate_ref, wa_ref, wb_ref, wo_ref,
                   g2_ref, wg_ref, wu_ref, wd_ref, gf_ref, out_ref):
    d_model = h1_ref.shape[1]
    subs = [slice(i * SUB_TILE, (i + 1) * SUB_TILE) for i in range(h1_ref.shape[0] // SUB_TILE)]
    pre = [(_dot(u2_ref[sl, :], wgate_ref[...]), _dot(ya_ref[sl, :], wa_ref[...]), _dot(yb_ref[sl, :], wb_ref[...]))
           for sl in subs]
    merged = []
    for gl, pa, pb in pre:
        gates = jax.nn.sigmoid(gl + bgate_ref[...])
        merged.append((gates[:, :d_model] * pa + gates[:, d_model:] * pb).astype(_BF16))
    h2s = [h1_ref[sl, :] + _dot(mg, wo_ref[...]) for sl, mg in zip(subs, merged)]
    acts = [_gate_up(_rms(h2, g2_ref[...]).astype(_BF16), wg_ref, wu_ref) for h2 in h2s]
    ys = [_dot(a, wd_ref[...]) for a in acts]
    for sl, h2, y in zip(subs, h2s, ys):
        out_ref[sl, :] = _rms(h2 + 0.5 * y, gf_ref[...])


def _t5_bucket_const(dist):
    max_exact = N_BUCKETS // 2
    n = jnp.maximum(dist, 0)
    nf = jnp.maximum(n, 1).astype(jnp.float32)
    large = max_exact + (jnp.log(nf / max_exact) / math.log(MAX_DISTANCE / max_exact)
                         * (N_BUCKETS - max_exact)).astype(jnp.int32)
    large = jnp.minimum(large, N_BUCKETS - 1)
    return jnp.where(n < max_exact, n, large)


def _bias_kernel(table_ref, idx_ref, out_ref):
    h = pl.program_id(0)
    idx = idx_ref[0]
    acc = jnp.full(idx.shape, MASKED, _F32)
    for b in range(N_BUCKETS):
        acc = jnp.where(idx == b, table_ref[b, h] * LOG2E, acc)
    out_ref[0, 0] = acc
    prev_key = lax.broadcasted_iota(jnp.int32, idx.shape, 0) < BLOCK
    out_ref[1, 0] = jnp.where(prev_key, MASKED, acc)


def _band_bias(rel_bias):
    ki = jnp.arange(2 * BLOCK)[:, None]
    qi = jnp.arange(BLOCK)[None, :]
    dist = qi + BLOCK - ki
    variants = [(d, w // d) for w, d in DIL_GROUPS] + [(1, B_WINDOW - 1)]
    idx = jnp.stack([jnp.where((dist >= 0) & (dist <= max_steps), _t5_bucket_const(dist * stride), -1)
                     for stride, max_steps in variants]).astype(jnp.int32)
    n_heads = rel_bias.shape[1]
    last = len(variants) - 1
    return pl.pallas_call(
        _bias_kernel,
        grid=(n_heads,),
        in_specs=[pl.BlockSpec(memory_space=pltpu.SMEM),
                  pl.BlockSpec((1, 2 * BLOCK, BLOCK),
                               lambda h: (jnp.minimum(h // A_HEADS_PER_GROUP, last), 0, 0))],
        out_specs=pl.BlockSpec((2, 1, 2 * BLOCK, BLOCK), lambda h: (0, h, 0, 0)),
        out_shape=jax.ShapeDtypeStruct((2, n_heads, 2 * BLOCK, BLOCK), _F32),
        name="band_bias",
    )(rel_bias.astype(_F32), idx)


def _const_spec(shape):
    nd = len(shape)
    return pl.BlockSpec(shape, lambda *_: (0,) * nd, pipeline_mode=pl.Buffered(1))


def _layer(x, ffn1_norm, ffn1_w_gate, ffn1_w_up, ffn1_w_down, mix_norm, w_in, b_in,
           w_branch_a, w_branch_b, w_out, sinks, rel_bias,
           ffn2_norm, ffn2_w_gate, ffn2_w_up, ffn2_w_down, final_norm):
    batch, seq, d_model = x.shape
    tokens = batch * seq
    tm = TOKEN_TILE
    tiles_per_seq = seq // tm
    n_tiles = tokens // tm
    d_ff = ffn1_w_gate.shape[1]
    a_w = A_HEADS * HEAD_DIM
    scale = HEAD_DIM ** -0.5 * LOG2E

    aq, ak, av = w_in[:, :a_w], w_in[:, a_w:2 * a_w], w_in[:, 2 * a_w:3 * a_w]
    baq, bak, bav = b_in[:a_w], b_in[a_w:2 * a_w], b_in[2 * a_w:3 * a_w]
    off = 3 * a_w
    bq_w, bq_b = w_in[:, off:off + B_OUT], b_in[off:off + B_OUT]
    bkv_w, bkv_b = w_in[:, off + B_OUT:off + B_OUT + 2 * B_KV_W], b_in[off + B_OUT:off + B_OUT + 2 * B_KV_W]
    gate_w, gate_b = w_in[:, off + B_W:], b_in[off + B_W:]
    perm = np.concatenate([np.r_[j * HEAD_DIM:(j + 1) * HEAD_DIM,
                                 (j + 4) * HEAD_DIM:(j + 5) * HEAD_DIM] for j in range(B_Q_HEADS // 2)])
    w_cols, b_cols = [], []
    for gi in range(len(DIL_GROUPS)):
        sl = slice(gi * A_OUT, (gi + 1) * A_OUT)
        w_cols += [aq[:, sl] * scale, ak[:, sl], av[:, sl]]
        b_cols += [baq[sl] * scale, bak[sl], bav[sl]]
    w_cols += [bq_w[:, perm] * scale, bkv_w]
    b_cols += [bq_b[perm] * scale, bkv_b]
    wqkv = jnp.concatenate(w_cols, axis=1).astype(_BF16)
    bqkv = jnp.concatenate(b_cols)[None, :].astype(_F32)
    wb_perm = w_branch_b[perm, :].astype(_BF16)

    row = lambda v: v[None, :].astype(_F32)
    x2 = x.reshape(tokens, d_model)

    bias = _band_bias(rel_bias)
    bias_nat = jnp.concatenate([bias[:, :A_HEADS_PER_GROUP], bias[:, A_HEADS:]], axis=1)
    sink_rows = jnp.repeat((sinks.astype(_F32) * LOG2E).reshape(B_KV_HEADS, B_Q_HEADS // B_KV_HEADS), BLOCK,
                           axis=1)[:, None, :]

    cur = lambda i: jnp.minimum(i, n_tiles - 1)
    prv = lambda i: jnp.maximum(i - 1, 0)
    d1, d2 = DIL_GROUPS[1][1], DIL_GROUPS[2][1]
    n_sub = tm // SUB_TILE
    h1, u2, qkv1, qkv2, yb, o0, l0 = pl.pallas_call(
        functools.partial(_stage1_kernel, tiles_per_seq=tiles_per_seq),
        grid=(n_tiles + 1,),
        in_specs=[pl.BlockSpec((tm, d_model), lambda i: (cur(i), 0)), _const_spec((1, d_model)),
                  _const_spec((d_model, d_ff)), _const_spec((d_model, d_ff)), _const_spec((d_ff, d_model)),
                  _const_spec((1, d_model)), _const_spec((d_model, 3 * GROUP_W + B_W)),
                  _const_spec((1, 3 * GROUP_W + B_W)),
                  _const_spec(bias_nat.shape), _const_spec(sink_rows.shape)],
        out_specs=[pl.BlockSpec((tm, d_model), lambda i: (cur(i), 0)),
                   pl.BlockSpec((tm, d_model), lambda i: (cur(i), 0)),
                   pl.BlockSpec((1, d1, tm // d1, GROUP_W),
                                lambda i: (cur(i) // tiles_per_seq, 0, cur(i) % tiles_per_seq, 0)),
                   pl.BlockSpec((1, d2, tm // d2, GROUP_W),
                                lambda i: (cur(i) // tiles_per_seq, 0, cur(i) % tiles_per_seq, 0)),
                   pl.BlockSpec((tm, B_OUT), lambda i: (prv(i), 0)),
                   pl.BlockSpec((A_OUT // LANES, tm, LANES), lambda i: (0, prv(i), 0)),
                   pl.BlockSpec((A_OUT // LANES, tm, LANES), lambda i: (0, prv(i), 0))],
        out_shape=[jax.ShapeDtypeStruct((tokens, d_model), _F32),
                   jax.ShapeDtypeStruct((tokens, d_model), _BF16),
                   jax.ShapeDtypeStruct((batch, d1, seq // d1, GROUP_W), _BF16),
                   jax.ShapeDtypeStruct((batch, d2, seq // d2, GROUP_W), _BF16),
                   jax.ShapeDtypeStruct((tokens, B_OUT), _BF16),
                   jax.ShapeDtypeStruct((A_OUT // LANES, tokens, LANES), _F32),
                   jax.ShapeDtypeStruct((A_OUT // LANES, tokens, LANES), _F32)],
        scratch_shapes=[pltpu.VMEM((n_sub, 2 * GROUP_W // LANES, SUB_TILE, LANES), _F32),
                        pltpu.VMEM((2, tm, GROUP_W), _BF16), pltpu.VMEM((2, tm, B_W), _BF16),
                        pltpu.VMEM((2, BLOCK, GROUP_W), _BF16), pltpu.VMEM((2, BLOCK, B_W), _BF16),
                        pltpu.VMEM((N_SLOTS, 2 * BLOCK, 2 * BLOCK), _F32),
                        pltpu.VMEM((N_SLOTS, 2 * BLOCK, 2 * BLOCK), _BF16)],
        compiler_params=pltpu.CompilerParams(dimension_semantics=("arbitrary",), vmem_limit_bytes=VMEM_LIMIT),
        name="stage1_ffn_qkv_attn",
    )(x2, row(ffn1_norm), ffn1_w_gate.astype(_BF16), ffn1_w_up.astype(_BF16), ffn1_w_down.astype(_BF16),
      row(mix_norm), wqkv, bqkv, bias_nat, sink_rows)

    params = pltpu.CompilerParams(dimension_semantics=("parallel",), vmem_limit_bytes=VMEM_LIMIT)

    n_heads = bias.shape[1]
    ya = pl.pallas_call(
        _stage2_kernel,
        grid=(batch,),
        in_specs=[pl.BlockSpec((1, d1, seq // d1, GROUP_W), lambda b: (b, 0, 0, 0)),
                  pl.BlockSpec((1, d2, seq // d2, GROUP_W), lambda b: (b, 0, 0, 0)),
                  pl.BlockSpec((A_OUT // LANES, seq, LANES), lambda b: (0, b, 0)),
                  pl.BlockSpec((A_OUT // LANES, seq, LANES), lambda b: (0, b, 0)),
                  pl.BlockSpec((None, n_heads, 2 * BLOCK, BLOCK), lambda b: (0, 0, 0, 0),
                               pipeline_mode=pl.Buffered(1))],
        out_specs=pl.BlockSpec((seq, A_OUT), lambda b: (b, 0)),
        out_shape=jax.ShapeDtypeStruct((tokens, A_OUT), _BF16),
        scratch_shapes=[pltpu.VMEM((A_OUT // LANES, seq, LANES), _F32) for _ in range(3)]
        + [pltpu.VMEM((N_SLOTS, 2 * BLOCK, 2 * BLOCK), _F32), pltpu.VMEM((N_SLOTS, 2 * BLOCK, 2 * BLOCK), _BF16)],
        compiler_params=params,
        name="stage2_dilated_attention",
    )(qkv1, qkv2, o0, l0, bias)

    tile = lambda w: pl.BlockSpec((tm, w), lambda i: (i, 0))
    out = pl.pallas_call(
        _stage3_kernel,
        grid=(n_tiles,),
        in_specs=[tile(d_model), tile(d_model), tile(A_OUT), tile(B_OUT),
                  _const_spec((d_model, 2 * d_model)), _const_spec((1, 2 * d_model)),
                  _const_spec((A_OUT, d_model)), _const_spec((B_OUT, d_model)), _const_spec((d_model, d_model)),
                  _const_spec((1, d_model)),
                  _const_spec((d_model, d_ff)), _const_spec((d_model, d_ff)), _const_spec((d_ff, d_model)),
                  _const_spec((1, d_model))],
        out_specs=tile(d_model),
        out_shape=jax.ShapeDtypeStruct((tokens, d_model), _F32),
        compiler_params=params,
        name="stage3_merge_ffn",
    )(h1, u2, ya, yb, gate_w.astype(_BF16), row(gate_b), w_branch_a.astype(_BF16), wb_perm,
      w_out.astype(_BF16), row(ffn2_norm), ffn2_w_gate.astype(_BF16), ffn2_w_up.astype(_BF16),
      ffn2_w_down.astype(_BF16), row(final_norm))
    return out.reshape(batch, seq, d_model)


def kernel(x, ffn1_norm, ffn1_w_gate, ffn1_w_up, ffn1_w_down, mix_norm, w_in, b_in, w_branch_a, w_branch_b,
           w_out, sinks, rel_bias, ffn2_norm, ffn2_w_gate, ffn2_w_up, ffn2_w_down, final_norm):
    assert ffn1_norm.shape[0] == 1, "single-layer kernel"
    return _layer(x, ffn1_norm[0], ffn1_w_gate[0], ffn1_w_up[0], ffn1_w_down[0], mix_norm[0], w_in[0], b_in[0],
                  w_branch_a[0], w_branch_b[0], w_out[0], sinks[0], rel_bias,
                  ffn2_norm[0], ffn2_w_gate[0], ffn2_w_up[0], ffn2_w_down[0], final_norm)
```

```python
import functools
import math

import numpy as np
import jax
import jax.numpy as jnp
from jax import lax
from jax.experimental import pallas as pl
from jax.experimental.pallas import tpu as pltpu

HEAD_DIM = 64
DIL_GROUPS = ((128, 1), (512, 4), (2048, 16))
A_HEADS_PER_GROUP = 4
A_HEADS = A_HEADS_PER_GROUP * len(DIL_GROUPS)
A_OUT = A_HEADS_PER_GROUP * HEAD_DIM
B_Q_HEADS = 8
B_KV_HEADS = 2
B_WINDOW = 128
B_OUT = B_Q_HEADS * HEAD_DIM
B_KV_W = B_KV_HEADS * HEAD_DIM
BLOCK = 128
N_BUCKETS = 32
MAX_DISTANCE = 2048
EPS = 1e-6
GROUP_W = 3 * A_OUT
B_W = B_OUT + 2 * B_KV_W
LANES = 128
SUBLANES = 8
MXU_COLS = 256
FF_CHUNKS = (768, 768, 768, 512)
LOG2E = math.log2(math.e)
MASKED = -1e30

TOKEN_TILE = 512
SUB_TILE = 256
VMEM_LIMIT = 60 * 1024 * 1024

_BF16 = jnp.bfloat16
_F32 = jnp.float32


def _rms(x, g):
    return (x * lax.rsqrt(jnp.mean(x * x, axis=-1, keepdims=True) + EPS)) * g


def _dot(a, b):
    return jnp.dot(a, b, preferred_element_type=_F32)


def _gate_up(xb, wg_ref, wu_ref, gap=None):
    acts = []
    c0 = 0
    for w in FF_CHUNKS:
        if gap is not None:
            gap()
        g = _dot(xb, wg_ref[:, c0:c0 + w])
        u = _dot(xb, wu_ref[:, c0:c0 + w])
        c0 += w
        acts.append(((g * jax.nn.sigmoid(g)) * u).astype(_BF16))
    return jnp.concatenate(acts, axis=1)


def _dot_chunks(a, w_ref, chunks, gap):
    outs = []
    c0 = 0
    for w in chunks:
        gap()
        outs.append(_dot(a, w_ref[:, c0:c0 + w]))
        c0 += w
    return jnp.concatenate(outs, axis=1)


KEY_CHUNK = 32
LOOKAHEAD = 4
G2_CLASSES_PER_STEP = 8
G1_CLASSES_PER_STEP = 2
N_SLOTS = 8
D_CHUNKS = (512, 512)
UNIT_SCHEDULE = (((2, 0),) + ((1, 1),) * 7 + ((2, 2),) * 4
                 + ((2, 1), (2, 1), (2, 1), (1, 1), (0, 1), (0, 2), (0, 2), (0, 1)))
Z_ORDER = (1, 2, 0, 3)


def _scores(scr, q2a, q2b, half_a, half_b, kc, bias_a, bias_b, sink_row=None):
    s_ref, _, slot = scr
    kn = kc.shape[0]
    lane = lax.broadcasted_iota(jnp.int32, q2a.shape, 1)
    keep = lambda q2, half: jnp.where((lane < HEAD_DIM) if half == 0 else (lane >= HEAD_DIM), q2, jnp.zeros_like(q2))
    qm = jnp.concatenate([keep(q2a, half_a), keep(q2b, half_b)], axis=0)
    s = lax.dot_general(kc, qm, (((1,), (1,)), ((), ())), preferred_element_type=_F32)
    s = s + jnp.concatenate([bias_a, bias_b], axis=1)
    s_ref[slot, :kn, :] = s
    m = jnp.max(s, axis=0, keepdims=True)
    if sink_row is not None:
        m = jnp.maximum(m, sink_row)
    return m


def _softmax_pv(scr, m, vc, sink_row=None):
    s_ref, p_ref, slot = scr
    kn = vc.shape[0]
    acc = jnp.zeros((SUBLANES, m.shape[1]), _F32)
    for c in range(kn // KEY_CHUNK):
        rows = slice(c * KEY_CHUNK, (c + 1) * KEY_CHUNK)
        p = jnp.exp2(s_ref[slot, rows, :] - m)
        for t in range(KEY_CHUNK // SUBLANES):
            acc = acc + p[t * SUBLANES:(t + 1) * SUBLANES]
        p_ref[slot, rows, :] = p.astype(_BF16)
    l = jnp.sum(acc, axis=0, keepdims=True)
    if sink_row is not None:
        l = l + jnp.exp2(sink_row - m)
    o_t = lax.dot_general(vc, p_ref[slot, :kn, :], (((0,), (0,)), ((), ())), preferred_element_type=_F32)
    return o_t, l


def _pair_finish(o_t, m, l):
    lse = (m + jnp.log2(l)) * (1.0 / LOG2E)
    on = jnp.concatenate([o_t[:HEAD_DIM, :BLOCK] / l[:, :BLOCK], o_t[HEAD_DIM:, BLOCK:] / l[:, BLOCK:]], axis=0)
    lb = jnp.concatenate([jnp.broadcast_to(lse[:, :BLOCK], (HEAD_DIM, BLOCK)),
                          jnp.broadcast_to(lse[:, BLOCK:], (HEAD_DIM, BLOCK))], axis=0)
    return on.T, lb.T


def _run_units(units):
    state = {}
    for i in range(len(units) + LOOKAHEAD):
        if i < len(units):
            state[i] = units[i][0]()
        k = i - LOOKAHEAD
        if k >= 0:
            units[k][1](state.pop(k))


class _UnitQueue:
    def __init__(self, units, schedule):
        self.units, self.next, self.pending, self.schedule = units, 0, [], iter(schedule)

    def _step(self, n_start, n_finish):
        for _ in range(min(n_finish, len(self.pending))):
            k, m = self.pending.pop(0)
            self.units[k][1](m)
        for k in range(self.next, min(self.next + n_start, len(self.units))):
            self.pending.append((k, self.units[k][0]()))
        self.next = min(self.next + n_start, len(self.units))

    def gap(self):
        self._step(*next(self.schedule))

    def flush(self):
        assert self.next == len(self.units) and next(self.schedule, None) is None
        self._step(0, len(self.pending))


def _stage1_kernel(x_ref, g1_ref, wg_ref, wu_ref, wd_ref, gmix_ref, wqkv_ref, bqkv_ref, bias_ref, sink_ref,
                   h1_ref, u2_ref, qkv1_ref, qkv2_ref, yb_ref, o0_ref, l0_ref,
                   zs_ref, cur0_ref, curb_ref, prev0_ref, prevb_ref, s_ref, p_ref, *, tiles_per_seq):
    i = pl.program_id(0)
    par = lax.rem(i, 2)
    opar = 1 - par
    opens_seq = lax.rem(i + tiles_per_seq - 1, tiles_per_seq) == 0
    tm = x_ref.shape[0]
    n_pair = A_OUT // LANES
    n_qp = B_OUT // LANES

    @pl.when(i == 0)
    def _():
        cur0_ref[...] = jnp.zeros_like(cur0_ref)
        curb_ref[...] = jnp.zeros_like(curb_ref)
        prev0_ref[...] = jnp.zeros_like(prev0_ref)
        prevb_ref[...] = jnp.zeros_like(prevb_ref)

    def cols(base, j):
        return slice(base + j * LANES, base + (j + 1) * LANES)

    def nat_units(c, slot0):
        rows = slice(c * BLOCK, (c + 1) * BLOCK)
        variant = jnp.where(opens_seq, 1, 0) if c == 0 else 0

        def keys(cur_ref, prev_ref, cs):
            if c == 0:
                return jnp.concatenate([prev_ref[opar, :, cs], cur_ref[opar, rows, cs]], axis=0)
            return cur_ref[opar, (c - 1) * BLOCK:(c + 1) * BLOCK, cs]

        units = []
        for j in range(n_pair):
            scr = (s_ref, p_ref, (slot0 + j) % N_SLOTS)

            def first_pass(j=j, scr=scr):
                q2 = cur0_ref[opar, rows, cols(0, j)]
                return _scores(scr, q2, q2, 0, 1, keys(cur0_ref, prev0_ref, cols(A_OUT, j)),
                               bias_ref[variant, 2 * j], bias_ref[variant, 2 * j + 1])

            def second_pass(m, j=j, scr=scr):
                o_t, l = _softmax_pv(scr, m, keys(cur0_ref, prev0_ref, cols(2 * A_OUT, j)))
                o0_ref[j, rows, :], l0_ref[j, rows, :] = _pair_finish(o_t, m, l)

            units.append((first_pass, second_pass))
        done = {}
        for half in range(2):
            for u in range(n_qp // 2):
                scr = (s_ref, p_ref, (slot0 + n_pair + half * (n_qp // 2) + u) % N_SLOTS)

                def first_pass(half=half, u=u, scr=scr):
                    ha = A_HEADS_PER_GROUP + n_qp * half + 2 * u
                    sink_row = sink_ref[half, :, 2 * u * BLOCK:(2 * u + 2) * BLOCK]
                    return _scores(scr, curb_ref[opar, rows, cols(0, 2 * u)], curb_ref[opar, rows, cols(0, 2 * u + 1)],
                                   half, half, keys(curb_ref, prevb_ref, slice(B_OUT, B_OUT + B_KV_W)),
                                   bias_ref[variant, ha], bias_ref[variant, ha + 1], sink_row=sink_row)

                def second_pass(m, half=half, u=u, scr=scr):
                    sink_row = sink_ref[half, :, 2 * u * BLOCK:(2 * u + 2) * BLOCK]
                    vc = keys(curb_ref, prevb_ref, slice(B_OUT + B_KV_W, B_OUT + 2 * B_KV_W))
                    o_t, l = _softmax_pv(scr, m, vc, sink_row=sink_row)
                    done[half, u] = o_t[half * HEAD_DIM:(half + 1) * HEAD_DIM] / l
                    if half == 1:
                        for jj in range(2):
                            j = 2 * u + jj
                            tile_t = jnp.concatenate([done[0, u][:, jj * BLOCK:(jj + 1) * BLOCK],
                                                      done[1, u][:, jj * BLOCK:(jj + 1) * BLOCK]], axis=0)
                            yb_ref[rows, cols(0, j)] = tile_t.T.astype(_BF16)

                units.append((first_pass, second_pass))
        return units

    units = []
    for c in range(tm // BLOCK):
        units += nat_units(c, c * (n_pair + n_qp))
    att = _UnitQueue(units, UNIT_SCHEDULE)

    subs = [slice(k * SUB_TILE, (k + 1) * SUB_TILE) for k in range(tm // SUB_TILE)]
    xs = [x_ref[sl, :] for sl in subs]
    xbs = [_rms(x, g1_ref[...]).astype(_BF16) for x in xs]
    acts = [_gate_up(xb, wg_ref, wu_ref, att.gap) for xb in xbs]
    ys = [_dot_chunks(a, wd_ref, D_CHUNKS, att.gap) for a in acts]
    u2s = []
    for sl, x, y in zip(subs, xs, ys):
        h1 = x + 0.5 * y
        h1_ref[sl, :] = h1
        u2 = _rms(h1, gmix_ref[...]).astype(_BF16)
        u2_ref[sl, :] = u2
        u2s.append(u2)
    prev0_ref[par] = cur0_ref[opar, tm - BLOCK:, :]
    prevb_ref[par] = curb_ref[opar, tm - BLOCK:, :]
    n_slab = GROUP_W // LANES
    for k, (sl, u2) in enumerate(zip(subs, u2s)):
        for q in Z_ORDER:
            cs = slice(q * GROUP_W, (q + 1) * GROUP_W)
            att.gap()
            z = _dot(u2, wqkv_ref[:, cs]) + bqkv_ref[:, cs]
            if q == 0:
                cur0_ref[par, sl, :] = z.astype(_BF16)
            elif q == len(DIL_GROUPS):
                curb_ref[par, sl, :] = z.astype(_BF16)
            else:
                out_ref = (qkv1_ref, qkv2_ref)[q - 1]
                d = DIL_GROUPS[q][1]
                n = SUB_TILE // d
                for j in range(n_slab):
                    zs_ref[k, (q - 1) * n_slab + j] = z[:, j * LANES:(j + 1) * LANES]
                for j in range(n_slab):
                    for r in range(d):
                        rows = zs_ref[k, (q - 1) * n_slab + j, pl.ds(r, n, stride=d), :]
                        out_ref[0, r, k * n:(k + 1) * n, j * LANES:(j + 1) * LANES] = rows.astype(_BF16)
    att.flush()


def _stage2_kernel(qkv1_ref, qkv2_ref, o0_ref, l0_ref, bias_ref,
                   ya_ref, y_ref, o2_ref, l2_ref, s_ref, p_ref):
    n_pair = A_OUT // LANES
    seq = ya_ref.shape[0]

    def cols(base, j):
        return slice(base + j * LANES, base + (j + 1) * LANES)

    def bias_pair(h, first):
        if first:
            return bias_ref[h, BLOCK:, :], bias_ref[h + 1, BLOCK:, :]
        return bias_ref[h], bias_ref[h + 1]

    def pair_unit(slot, load, h, first, finish):
        scr = (s_ref, p_ref, slot)

        def first_pass():
            q2 = load(0, True)
            return _scores(scr, q2, q2, 0, 1, load(A_OUT, False), *bias_pair(h, first))

        def second_pass(m):
            o_t, l = _softmax_pv(scr, m, load(2 * A_OUT, False))
            finish(*_pair_finish(o_t, m, l))

        return first_pass, second_pass

    d2 = DIL_GROUPS[2][1]

    def g2_body(it, carry):
        units = []
        for c in range(G2_CLASSES_PER_STEP):
            r = it * G2_CLASSES_PER_STEP + c
            for j in range(n_pair):
                def load(base, is_q, r=r, j=j):
                    return qkv2_ref[0, r, :, cols(base, j)]

                def finish(out, lse2, r=r, j=j):
                    o2_ref[j, pl.ds(r, BLOCK, stride=d2), :] = out
                    l2_ref[j, pl.ds(r, BLOCK, stride=d2), :] = lse2

                units.append(pair_unit((c * n_pair + j) % N_SLOTS, load, 8 + 2 * j, True, finish))
        _run_units(units)
        return carry

    lax.fori_loop(0, d2 // G2_CLASSES_PER_STEP, g2_body, 0)

    d1 = DIL_GROUPS[1][1]
    nb1 = seq // d1 // BLOCK

    def g1_body(it, carry):
        units = []
        for c in range(G1_CLASSES_PER_STEP):
            r = it * G1_CLASSES_PER_STEP + c
            for b in range(nb1):
                first = b == 0
                row0 = b * BLOCK
                k0 = 0 if first else (b - 1) * BLOCK
                kn = BLOCK if first else 2 * BLOCK
                for j in range(n_pair):
                    def load(base, is_q, r=r, j=j, row0=row0, k0=k0, kn=kn):
                        return qkv1_ref[0, r, pl.ds(row0, BLOCK) if is_q else pl.ds(k0, kn), cols(base, j)]

                    def finish(o1, l1, r=r, j=j, row0=row0):
                        tok = pl.ds(row0 * d1 + r, BLOCK, stride=d1)
                        o0, l0 = o0_ref[j, tok, :], l0_ref[j, tok, :]
                        o2, l2 = o2_ref[j, tok, :], l2_ref[j, tok, :]
                        top = jnp.maximum(jnp.maximum(l0, l1), l2)
                        w0, w1, w2 = jnp.exp(l0 - top), jnp.exp(l1 - top), jnp.exp(l2 - top)
                        y_ref[j, tok, :] = (w0 * o0 + w1 * o1 + w2 * o2) / (w0 + w1 + w2)

                    units.append(pair_unit(len(units) % N_SLOTS, load, 4 + 2 * j, first, finish))
        _run_units(units)
        return carry

    lax.fori_loop(0, d1 // G1_CLASSES_PER_STEP, g1_body, 0)
    for j in range(n_pair):
        ya_ref[:, cols(0, j)] = y_ref[j].astype(_BF16)


def _stage3_kernel(h1_ref, u2_ref, ya_ref, yb_ref, wgate_ref, bgate_ref, wa_ref, wb_ref, wo_ref,
                   g2_ref, wg_ref, wu_ref, wd_ref, gf_ref, out_ref):
    d_model = h1_ref.shape[1]
    subs = [slice(i * SUB_TILE, (i + 1) * SUB_TILE) for i in range(h1_ref.shape[0] // SUB_TILE)]
    pre = [(_dot(u2_ref[sl, :], wgate_ref[...]), _dot(ya_ref[sl, :], wa_ref[...]), _dot(yb_ref[sl, :], wb_ref[...]))
           for sl in subs]
    merged = []
    for gl, pa, pb in pre:
        gates = jax.nn.sigmoid(gl + bgate_ref[...])
        merged.append((gates[:, :d_model] * pa + gates[:, d_model:] * pb).astype(_BF16))
    h2s = [h1_ref[sl, :] + _dot(mg, wo_ref[...]) for sl, mg in zip(subs, merged)]
    acts = [_gate_up(_rms(h2, g2_ref[...]).astype(_BF16), wg_ref, wu_ref) for h2 in h2s]
    ys = [_dot(a, wd_ref[...]) for a in acts]
    for sl, h2, y in zip(subs, h2s, ys):
        out_ref[sl, :] = _rms(h2 + 0.5 * y, gf_ref[...])


def _t5_bucket_const(dist):
    max_exact = N_BUCKETS // 2
    n = jnp.maximum(dist, 0)
    nf = jnp.maximum(n, 1).astype(jnp.float32)
    large = max_exact + (jnp.log(nf / max_exact) / math.log(MAX_DISTANCE / max_exact)
                         * (N_BUCKETS - max_exact)).astype(jnp.int32)
    large = jnp.minimum(large, N_BUCKETS - 1)
    return jnp.where(n < max_exact, n, large)


def _bias_kernel(table_ref, idx_ref, out_ref):
    h = pl.program_id(0)
    idx = idx_ref[0]
    acc = jnp.full(idx.shape, MASKED, _F32)
    for b in range(N_BUCKETS):
        acc = jnp.where(idx == b, table_ref[b, h] * LOG2E, acc)
    out_ref[0, 0] = acc
    prev_key = lax.broadcasted_iota(jnp.int32, idx.shape, 0) < BLOCK
    out_ref[1, 0] = jnp.where(prev_key, MASKED, acc)


def _band_bias(rel_bias):
    ki = jnp.arange(2 * BLOCK)[:, None]
    qi = jnp.arange(BLOCK)[None, :]
    dist = qi + BLOCK - ki
    variants = [(d, w // d) for w, d in DIL_GROUPS] + [(1, B_WINDOW - 1)]
    idx = jnp.stack([jnp.where((dist >= 0) & (dist <= max_steps), _t5_bucket_const(dist * stride), -1)
                     for stride, max_steps in variants]).astype(jnp.int32)
    n_heads = rel_bias.shape[1]
    last = len(variants) - 1
    return pl.pallas_call(
        _bias_kernel,
        grid=(n_heads,),
        in_specs=[pl.BlockSpec(memory_space=pltpu.SMEM),
                  pl.BlockSpec((1, 2 * BLOCK, BLOCK),
                               lambda h: (jnp.minimum(h // A_HEADS_PER_GROUP, last), 0, 0))],
        out_specs=pl.BlockSpec((2, 1, 2 * BLOCK, BLOCK), lambda h: (0, h, 0, 0)),
        out_shape=jax.ShapeDtypeStruct((2, n_heads, 2 * BLOCK, BLOCK), _F32),
        name="band_bias",
    )(rel_bias.astype(_F32), idx)


def _const_spec(shape):
    nd = len(shape)
    return pl.BlockSpec(shape, lambda *_: (0,) * nd, pipeline_mode=pl.Buffered(1))


def _layer(x, ffn1_norm, ffn1_w_gate, ffn1_w_up, ffn1_w_down, mix_norm, w_in, b_in,
           w_branch_a, w_branch_b, w_out, sinks, rel_bias,
           ffn2_norm, ffn2_w_gate, ffn2_w_up, ffn2_w_down, final_norm):
    batch, seq, d_model = x.shape
    tokens = batch * seq
    tm = TOKEN_TILE
    tiles_per_seq = seq // tm
    n_tiles = tokens // tm
    d_ff = ffn1_w_gate.shape[1]
    a_w = A_HEADS * HEAD_DIM
    scale = HEAD_DIM ** -0.5 * LOG2E

    aq, ak, av = w_in[:, :a_w], w_in[:, a_w:2 * a_w], w_in[:, 2 * a_w:3 * a_w]
    baq, bak, bav = b_in[:a_w], b_in[a_w:2 * a_w], b_in[2 * a_w:3 * a_w]
    off = 3 * a_w
    bq_w, bq_b = w_in[:, off:off + B_OUT], b_in[off:off + B_OUT]
    bkv_w, bkv_b = w_in[:, off + B_OUT:off + B_OUT + 2 * B_KV_W], b_in[off + B_OUT:off + B_OUT + 2 * B_KV_W]
    gate_w, gate_b = w_in[:, off + B_W:], b_in[off + B_W:]
    perm = np.concatenate([np.r_[j * HEAD_DIM:(j + 1) * HEAD_DIM,
                                 (j + 4) * HEAD_DIM:(j + 5) * HEAD_DIM] for j in range(B_Q_HEADS // 2)])
    w_cols, b_cols = [], []
    for gi in range(len(DIL_GROUPS)):
        sl = slice(gi * A_OUT, (gi + 1) * A_OUT)
        w_cols += [aq[:, sl] * scale, ak[:, sl], av[:, sl]]
        b_cols += [baq[sl] * scale, bak[sl], bav[sl]]
    w_cols += [bq_w[:, perm] * scale, bkv_w]
    b_cols += [bq_b[perm] * scale, bkv_b]
    wqkv = jnp.concatenate(w_cols, axis=1).astype(_BF16)
    bqkv = jnp.concatenate(b_cols)[None, :].astype(_F32)
    wb_perm = w_branch_b[perm, :].astype(_BF16)

    row = lambda v: v[None, :].astype(_F32)
    x2 = x.reshape(tokens, d_model)

    bias = _band_bias(rel_bias)
    bias_nat = jnp.concatenate([bias[:, :A_HEADS_PER_GROUP], bias[:, A_HEADS:]], axis=1)
    sink_rows = jnp.repeat((sinks.astype(_F32) * LOG2E).reshape(B_KV_HEADS, B_Q_HEADS // B_KV_HEADS), BLOCK,
                           axis=1)[:, None, :]

    cur = lambda i: jnp.minimum(i, n_tiles - 1)
    prv = lambda i: jnp.maximum(i - 1, 0)
    d1, d2 = DIL_GROUPS[1][1], DIL_GROUPS[2][1]
    n_sub = tm // SUB_TILE
    h1, u2, qkv1, qkv2, yb, o0, l0 = pl.pallas_call(
        functools.partial(_stage1_kernel, tiles_per_seq=tiles_per_seq),
        grid=(n_tiles + 1,),
        in_specs=[pl.BlockSpec((tm, d_model), lambda i: (cur(i), 0)), _const_spec((1, d_model)),
                  _const_spec((d_model, d_ff)), _const_spec((d_model, d_ff)), _const_spec((d_ff, d_model)),
                  _const_spec((1, d_model)), _const_spec((d_model, 3 * GROUP_W + B_W)),
                  _const_spec((1, 3 * GROUP_W + B_W)),
                  _const_spec(bias_nat.shape), _const_spec(sink_rows.shape)],
        out_specs=[pl.BlockSpec((tm, d_model), lambda i: (cur(i), 0)),
                   pl.BlockSpec((tm, d_model), lambda i: (cur(i), 0)),
                   pl.BlockSpec((1, d1, tm // d1, GROUP_W),
                                lambda i: (cur(i) // tiles_per_seq, 0, cur(i) % tiles_per_seq, 0)),
                   pl.BlockSpec((1, d2, tm // d2, GROUP_W),
                                lambda i: (cur(i) // tiles_per_seq, 0, cur(i) % tiles_per_seq, 0)),
                   pl.BlockSpec((tm, B_OUT), lambda i: (prv(i), 0)),
                   pl.BlockSpec((A_OUT // LANES, tm, LANES), lambda i: (0, prv(i), 0)),
                   pl.BlockSpec((A_OUT // LANES, tm, LANES), lambda i: (0, prv(i), 0))],
        out_shape=[jax.ShapeDtypeStruct((tokens, d_model), _F32),
                   jax.ShapeDtypeStruct((tokens, d_model), _BF16),
                   jax.ShapeDtypeStruct((batch, d1, seq // d1, GROUP_W), _BF16),
                   jax.ShapeDtypeStruct((batch, d2, seq // d2, GROUP_W), _BF16),
                   jax.ShapeDtypeStruct((tokens, B_OUT), _BF16),
                   jax.ShapeDtypeStruct((A_OUT // LANES, tokens, LANES), _F32),
                   jax.ShapeDtypeStruct((A_OUT // LANES, tokens, LANES), _F32)],
        scratch_shapes=[pltpu.VMEM((n_sub, 2 * GROUP_W // LANES, SUB_TILE, LANES), _F32),
                        pltpu.VMEM((2, tm, GROUP_W), _BF16), pltpu.VMEM((2, tm, B_W), _BF16),
                        pltpu.VMEM((2, BLOCK, GROUP_W), _BF16), pltpu.VMEM((2, BLOCK, B_W), _BF16),
                        pltpu.VMEM((N_SLOTS, 2 * BLOCK, 2 * BLOCK), _F32),
                        pltpu.VMEM((N_SLOTS, 2 * BLOCK, 2 * BLOCK), _BF16)],
        compiler_params=pltpu.CompilerParams(dimension_semantics=("arbitrary",), vmem_limit_bytes=VMEM_LIMIT),
        name="stage1_ffn_qkv_attn",
    )(x2, row(ffn1_norm), ffn1_w_gate.astype(_BF16), ffn1_w_up.astype(_BF16), ffn1_w_down.astype(_BF16),
      row(mix_norm), wqkv, bqkv, bias_nat, sink_rows)

    params = pltpu.CompilerParams(dimension_semantics=("parallel",), vmem_limit_bytes=VMEM_LIMIT)

    n_heads = bias.shape[1]
    ya = pl.pallas_call(
        _stage2_kernel,
        grid=(batch,),
        in_specs=[pl.BlockSpec((1, d1, seq // d1, GROUP_W), lambda b: (b, 0, 0, 0)),
                  pl.BlockSpec((1, d2, seq // d2, GROUP_W), lambda b: (b, 0, 0, 0)),
                  pl.BlockSpec((A_OUT // LANES, seq, LANES), lambda b: (0, b, 0)),
                  pl.BlockSpec((A_OUT // LANES, seq, LANES), lambda b: (0, b, 0)),
                  pl.BlockSpec((None, n_heads, 2 * BLOCK, BLOCK), lambda b: (0, 0, 0, 0),
                               pipeline_mode=pl.Buffered(1))],
        out_specs=pl.BlockSpec((seq, A_OUT), lambda b: (b, 0)),
        out_shape=jax.ShapeDtypeStruct((tokens, A_OUT), _BF16),
        scratch_shapes=[pltpu.VMEM((A_OUT // LANES, seq, LANES), _F32) for _ in range(3)]
        + [pltpu.VMEM((N_SLOTS, 2 * BLOCK, 2 * BLOCK), _F32), pltpu.VMEM((N_SLOTS, 2 * BLOCK, 2 * BLOCK), _BF16)],
        compiler_params=params,
        name="stage2_dilated_attention",
    )(qkv1, qkv2, o0, l0, bias)

    tile = lambda w: pl.BlockSpec((tm, w), lambda i: (i, 0))
    out = pl.pallas_call(
        _stage3_kernel,
        grid=(n_tiles,),
        in_specs=[tile(d_model), tile(d_model), tile(A_OUT), tile(B_OUT),
                  _const_spec((d_model, 2 * d_model)), _const_spec((1, 2 * d_model)),
                  _const_spec((A_OUT, d_model)), _const_spec((B_OUT, d_model)), _const_spec((d_model, d_model)),
                  _const_spec((1, d_model)),
                  _const_spec((d_model, d_ff)), _const_spec((d_model, d_ff)), _const_spec((d_ff, d_model)),
                  _const_spec((1, d_model))],
        out_specs=tile(d_model),
        out_shape=jax.ShapeDtypeStruct((tokens, d_model), _F32),
        compiler_params=params,
        name="stage3_merge_ffn",
    )(h1, u2, ya, yb, gate_w.astype(_BF16), row(gate_b), w_branch_a.astype(_BF16), wb_perm,
      w_out.astype(_BF16), row(ffn2_norm), ffn2_w_gate.astype(_BF16), ffn2_w_up.astype(_BF16),
      ffn2_w_down.astype(_BF16), row(final_norm))
    return out.reshape(batch, seq, d_model)


def kernel(x, ffn1_norm, ffn1_w_gate, ffn1_w_up, ffn1_w_down, mix_norm, w_in, b_in, w_branch_a, w_branch_b,
           w_out, sinks, rel_bias, ffn2_norm, ffn2_w_gate, ffn2_w_up, ffn2_w_down, final_norm):
    assert ffn1_norm.shape[0] == 1, "single-layer kernel"
    return _layer(x, ffn1_norm[0], ffn1_w_gate[0], ffn1_w_up[0], ffn1_w_down[0], mix_norm[0], w_in[0], b_in[0],
                  w_branch_a[0], w_branch_b[0], w_out[0], sinks[0], rel_bias,
                  ffn2_norm[0], ffn2_w_gate[0], ffn2_w_up[0], ffn2_w_down[0], final_norm)
```

```python
import functools
import math

import numpy as np
import jax
import jax.numpy as jnp
from jax import lax
from jax.experimental import pallas as pl
from jax.experimental.pallas import tpu as pltpu

HEAD_DIM = 64
DIL_GROUPS = ((128, 1), (512, 4), (2048, 16))
A_HEADS_PER_GROUP = 4
A_HEADS = A_HEADS_PER_GROUP * len(DIL_GROUPS)
A_OUT = A_HEADS_PER_GROUP * HEAD_DIM
B_Q_HEADS = 8
B_KV_HEADS = 2
B_WINDOW = 128
B_OUT = B_Q_HEADS * HEAD_DIM
B_KV_W = B_KV_HEADS * HEAD_DIM
BLOCK = 128
N_BUCKETS = 32
MAX_DISTANCE = 2048
EPS = 1e-6
GROUP_W = 3 * A_OUT
B_W = B_OUT + 2 * B_KV_W
LANES = 128
SUBLANES = 8
MXU_COLS = 256
FF_CHUNKS = (768, 768, 768, 512)
LOG2E = math.log2(math.e)
MASKED = -1e30

TOKEN_TILE = 512
SUB_TILE = 256
VMEM_LIMIT = 60 * 1024 * 1024

_BF16 = jnp.bfloat16
_F32 = jnp.float32


def _rms(x, g):
    return (x * lax.rsqrt(jnp.mean(x * x, axis=-1, keepdims=True) + EPS)) * g


def _dot(a, b):
    return jnp.dot(a, b, preferred_element_type=_F32)


def _gate_up(xb, wg_ref, wu_ref, gap=None):
    acts = []
    c0 = 0
    for w in FF_CHUNKS:
        if gap is not None:
            gap()
        g = _dot(xb, wg_ref[:, c0:c0 + w])
        u = _dot(xb, wu_ref[:, c0:c0 + w])
        c0 += w
        acts.append(((g * jax.nn.sigmoid(g)) * u).astype(_BF16))
    return jnp.concatenate(acts, axis=1)


def _dot_chunks(a, w_ref, chunks, gap):
    outs = []
    c0 = 0
    for w in chunks:
        gap()
        outs.append(_dot(a, w_ref[:, c0:c0 + w]))
        c0 += w
    return jnp.concatenate(outs, axis=1)


KEY_CHUNK = 32
LOOKAHEAD = 4
G2_CLASSES_PER_STEP = 8
G1_CLASSES_PER_STEP = 2
N_SLOTS = 8
D_CHUNKS = (512, 512)
UNIT_SCHEDULE = (((2, 0),) + ((1, 1),) * 7 + ((2, 2),) * 4
                 + ((2, 1), (2, 1), (2, 1), (1, 1), (0, 1), (0, 2), (0, 2), (0, 1)))
Z_ORDER = (1, 2, 0, 3)


def _scores(scr, q2a, q2b, half_a, half_b, kc, bias_a, bias_b, sink_row=None):
    s_ref, _, slot = scr
    kn = kc.shape[0]
    lane = lax.broadcasted_iota(jnp.int32, q2a.shape, 1)
    keep = lambda q2, half: jnp.where((lane < HEAD_DIM) if half == 0 else (lane >= HEAD_DIM), q2, jnp.zeros_like(q2))
    qm = jnp.concatenate([keep(q2a, half_a), keep(q2b, half_b)], axis=0)
    s = lax.dot_general(kc, qm, (((1,), (1,)), ((), ())), preferred_element_type=_F32)
    s = s + jnp.concatenate([bias_a, bias_b], axis=1)
    s_ref[slot, :kn, :] = s
    m = jnp.max(s, axis=0, keepdims=True)
    if sink_row is not None:
        m = jnp.maximum(m, sink_row)
    return m


def _softmax_pv(scr, m, vc, sink_row=None):
    s_ref, p_ref, slot = scr
    kn = vc.shape[0]
    acc = jnp.zeros((SUBLANES, m.shape[1]), _F32)
    for c in range(kn // KEY_CHUNK):
        rows = slice(c * KEY_CHUNK, (c + 1) * KEY_CHUNK)
        p = jnp.exp2(s_ref[slot, rows, :] - m)
        for t in range(KEY_CHUNK // SUBLANES):
            acc = acc + p[t * SUBLANES:(t + 1) * SUBLANES]
        p_ref[slot, rows, :] = p.astype(_BF16)
    l = jnp.sum(acc, axis=0, keepdims=True)
    if sink_row is not None:
        l = l + jnp.exp2(sink_row - m)
    o_t = lax.dot_general(vc, p_ref[slot, :kn, :], (((0,), (0,)), ((), ())), preferred_element_type=_F32)
    return o_t, l


def _pair_finish(o_t, m, l):
    lse = (m + jnp.log2(l)) * (1.0 / LOG2E)
    on = jnp.concatenate([o_t[:HEAD_DIM, :BLOCK] / l[:, :BLOCK], o_t[HEAD_DIM:, BLOCK:] / l[:, BLOCK:]], axis=0)
    lb = jnp.concatenate([jnp.broadcast_to(lse[:, :BLOCK], (HEAD_DIM, BLOCK)),
                          jnp.broadcast_to(lse[:, BLOCK:], (HEAD_DIM, BLOCK))], axis=0)
    return on.T, lb.T


def _run_units(units):
    state = {}
    for i in range(len(units) + LOOKAHEAD):
        if i < len(units):
            state[i] = units[i][0]()
        k = i - LOOKAHEAD
        if k >= 0:
            units[k][1](state.pop(k))


class _UnitQueue:
    def __init__(self, units, schedule):
        self.units, self.next, self.pending, self.schedule = units, 0, [], iter(schedule)

    def _step(self, n_start, n_finish):
        for _ in range(min(n_finish, len(self.pending))):
            k, m = self.pending.pop(0)
            self.units[k][1](m)
        for k in range(self.next, min(self.next + n_start, len(self.units))):
            self.pending.append((k, self.units[k][0]()))
        self.next = min(self.next + n_start, len(self.units))

    def gap(self):
        self._step(*next(self.schedule))

    def flush(self):
        assert self.next == len(self.units) and next(self.schedule, None) is None
        self._step(0, len(self.pending))


def _stage1_kernel(x_ref, g1_ref, wg_ref, wu_ref, wd_ref, gmix_ref, wqkv_ref, bqkv_ref, bias_ref, sink_ref,
                   h1_ref, u2_ref, qkv1_ref, qkv2_ref, yb_ref, o0_ref, l0_ref,
                   zs_ref, cur0_ref, curb_ref, prev0_ref, prevb_ref, s_ref, p_ref, *, tiles_per_seq):
    i = pl.program_id(0)
    par = lax.rem(i, 2)
    opar = 1 - par
    opens_seq = lax.rem(i + tiles_per_seq - 1, tiles_per_seq) == 0
    tm = x_ref.shape[0]
    n_pair = A_OUT // LANES
    n_qp = B_OUT // LANES

    @pl.when(i == 0)
    def _():
        cur0_ref[...] = jnp.zeros_like(cur0_ref)
        curb_ref[...] = jnp.zeros_like(curb_ref)
        prev0_ref[...] = jnp.zeros_like(prev0_ref)
        prevb_ref[...] = jnp.zeros_like(prevb_ref)

    def cols(base, j):
        return slice(base + j * LANES, base + (j + 1) * LANES)

    def nat_units(c, slot0):
        rows = slice(c * BLOCK, (c + 1) * BLOCK)
        variant = jnp.where(opens_seq, 1, 0) if c == 0 else 0

        def keys(cur_ref, prev_ref, cs):
            if c == 0:
                return jnp.concatenate([prev_ref[opar, :, cs], cur_ref[opar, rows, cs]], axis=0)
            return cur_ref[opar, (c - 1) * BLOCK:(c + 1) * BLOCK, cs]

        units = []
        for j in range(n_pair):
            scr = (s_ref, p_ref, (slot0 + j) % N_SLOTS)

            def first_pass(j=j, scr=scr):
                q2 = cur0_ref[opar, rows, cols(0, j)]
                return _scores(scr, q2, q2, 0, 1, keys(cur0_ref, prev0_ref, cols(A_OUT, j)),
                               bias_ref[variant, 2 * j], bias_ref[variant, 2 * j + 1])

            def second_pass(m, j=j, scr=scr):
                o_t, l = _softmax_pv(scr, m, keys(cur0_ref, prev0_ref, cols(2 * A_OUT, j)))
                o0_ref[j, rows, :], l0_ref[j, rows, :] = _pair_finish(o_t, m, l)

            units.append((first_pass, second_pass))
        done = {}
        for half in range(2):
            for u in range(n_qp // 2):
                scr = (s_ref, p_ref, (slot0 + n_pair + half * (n_qp // 2) + u) % N_SLOTS)

                def first_pass(half=half, u=u, scr=scr):
                    ha = A_HEADS_PER_GROUP + n_qp * half + 2 * u
                    sink_row = sink_ref[half, :, 2 * u * BLOCK:(2 * u + 2) * BLOCK]
                    return _scores(scr, curb_ref[opar, rows, cols(0, 2 * u)], curb_ref[opar, rows, cols(0, 2 * u + 1)],
                                   half, half, keys(curb_ref, prevb_ref, slice(B_OUT, B_OUT + B_KV_W)),
                                   bias_ref[variant, ha], bias_ref[variant, ha + 1], sink_row=sink_row)

                def second_pass(m, half=half, u=u, scr=scr):
                    sink_row = sink_ref[half, :, 2 * u * BLOCK:(2 * u + 2) * BLOCK]
                    v0 = B_OUT + B_KV_W + half * HEAD_DIM
                    vc = keys(curb_ref, prevb_ref, slice(v0, v0 + HEAD_DIM))
                    o_t, l = _softmax_pv(scr, m, vc, sink_row=sink_row)
                    done[half, u] = o_t / l
                    if half == 1:
                        for jj in range(2):
                            j = 2 * u + jj
                            tile_t = jnp.concatenate([done[0, u][:, jj * BLOCK:(jj + 1) * BLOCK],
                                                      done[1, u][:, jj * BLOCK:(jj + 1) * BLOCK]], axis=0)
                            yb_ref[rows, cols(0, j)] = tile_t.T.astype(_BF16)

                units.append((first_pass, second_pass))
        return units

    units = []
    for c in range(tm // BLOCK):
        units += nat_units(c, c * (n_pair + n_qp))
    att = _UnitQueue(units, UNIT_SCHEDULE)

    subs = [slice(k * SUB_TILE, (k + 1) * SUB_TILE) for k in range(tm // SUB_TILE)]
    xs = [x_ref[sl, :] for sl in subs]
    xbs = [_rms(x, g1_ref[...]).astype(_BF16) for x in xs]
    acts = [_gate_up(xb, wg_ref, wu_ref, att.gap) for xb in xbs]
    ys = [_dot_chunks(a, wd_ref, D_CHUNKS, att.gap) for a in acts]
    u2s = []
    for sl, x, y in zip(subs, xs, ys):
        h1 = x + 0.5 * y
        h1_ref[sl, :] = h1
        u2 = _rms(h1, gmix_ref[...]).astype(_BF16)
        u2_ref[sl, :] = u2
        u2s.append(u2)
    prev0_ref[par] = cur0_ref[opar, tm - BLOCK:, :]
    prevb_ref[par] = curb_ref[opar, tm - BLOCK:, :]
    n_slab = GROUP_W // LANES
    for k, (sl, u2) in enumerate(zip(subs, u2s)):
        for q in Z_ORDER:
            cs = slice(q * GROUP_W, (q + 1) * GROUP_W)
            att.gap()
            z = _dot(u2, wqkv_ref[:, cs]) + bqkv_ref[:, cs]
            if q == 0:
                cur0_ref[par, sl, :] = z.astype(_BF16)
            elif q == len(DIL_GROUPS):
                curb_ref[par, sl, :] = z.astype(_BF16)
            else:
                out_ref = (qkv1_ref, qkv2_ref)[q - 1]
                d = DIL_GROUPS[q][1]
                n = SUB_TILE // d
                for j in range(n_slab):
                    zs_ref[k, (q - 1) * n_slab + j] = z[:, j * LANES:(j + 1) * LANES]
                for j in range(n_slab):
                    for r in range(d):
                        rows = zs_ref[k, (q - 1) * n_slab + j, pl.ds(r, n, stride=d), :]
                        out_ref[0, r, k * n:(k + 1) * n, j * LANES:(j + 1) * LANES] = rows.astype(_BF16)
    att.flush()


def _stage2_kernel(qkv1_ref, qkv2_ref, o0_ref, l0_ref, bias_ref,
                   ya_ref, y_ref, o2_ref, l2_ref, s_ref, p_ref):
    n_pair = A_OUT // LANES
    seq = ya_ref.shape[0]

    def cols(base, j):
        return slice(base + j * LANES, base + (j + 1) * LANES)

    def bias_pair(h, first):
        if first:
            return bias_ref[h, BLOCK:, :], bias_ref[h + 1, BLOCK:, :]
        return bias_ref[h], bias_ref[h + 1]

    def pair_unit(slot, load, h, first, finish):
        scr = (s_ref, p_ref, slot)

        def first_pass():
            q2 = load(0, True)
            return _scores(scr, q2, q2, 0, 1, load(A_OUT, False), *bias_pair(h, first))

        def second_pass(m):
            o_t, l = _softmax_pv(scr, m, load(2 * A_OUT, False))
            finish(*_pair_finish(o_t, m, l))

        return first_pass, second_pass

    d1, d2 = DIL_GROUPS[1][1], DIL_GROUPS[2][1]
    assert G2_CLASSES_PER_STEP % d1 == 0

    def g2_body(it, carry):
        units = []
        for c in range(G2_CLASSES_PER_STEP):
            r = it * G2_CLASSES_PER_STEP + c
            for j in range(n_pair):
                def load(base, is_q, r=r, j=j):
                    return qkv2_ref[0, r, :, cols(base, j)]

                def finish(out, lse2, c=c, j=j):
                    start = (c % d1) * (seq // d1) + it * (G2_CLASSES_PER_STEP // d1) + c // d1
                    o2_ref[j, pl.ds(start, BLOCK, stride=d1), :] = out
                    l2_ref[j, pl.ds(start, BLOCK, stride=d1), :] = lse2

                units.append(pair_unit((c * n_pair + j) % N_SLOTS, load, 8 + 2 * j, True, finish))
        _run_units(units)
        return carry

    lax.fori_loop(0, d2 // G2_CLASSES_PER_STEP, g2_body, 0)

    nb1 = seq // d1 // BLOCK

    def g1_body(it, carry):
        units = []
        for c in range(G1_CLASSES_PER_STEP):
            r = it * G1_CLASSES_PER_STEP + c
            for b in range(nb1):
                first = b == 0
                row0 = b * BLOCK
                k0 = 0 if first else (b - 1) * BLOCK
                kn = BLOCK if first else 2 * BLOCK
                for j in range(n_pair):
                    def load(base, is_q, r=r, j=j, row0=row0, k0=k0, kn=kn):
                        return qkv1_ref[0, r, pl.ds(row0, BLOCK) if is_q else pl.ds(k0, kn), cols(base, j)]

                    def finish(o1, l1, r=r, j=j, row0=row0):
                        tok = pl.ds(row0 * d1 + r, BLOCK, stride=d1)
                        o0, l0 = o0_ref[j, tok, :], l0_ref[j, tok, :]
                        own = pl.ds(pl.multiple_of(r * (seq // d1) + row0, BLOCK), BLOCK)
                        o2, l2 = o2_ref[j, own, :], l2_ref[j, own, :]
                        top = jnp.maximum(jnp.maximum(l0, l1), l2)
                        w0, w1, w2 = jnp.exp(l0 - top), jnp.exp(l1 - top), jnp.exp(l2 - top)
                        y_ref[j, tok, :] = (w0 * o0 + w1 * o1 + w2 * o2) / (w0 + w1 + w2)

                    units.append(pair_unit(len(units) % N_SLOTS, load, 4 + 2 * j, first, finish))
        _run_units(units)
        return carry

    lax.fori_loop(0, d1 // G1_CLASSES_PER_STEP, g1_body, 0)
    for j in range(n_pair):
        ya_ref[:, cols(0, j)] = y_ref[j].astype(_BF16)


def _stage3_kernel(h1_ref, u2_ref, ya_ref, yb_ref, wgate_ref, bgate_ref, wa_ref, wb_ref, wo_ref,
                   g2_ref, wg_ref, wu_ref, wd_ref, gf_ref, out_ref):
    d_model = h1_ref.shape[1]
    subs = [slice(i * SUB_TILE, (i + 1) * SUB_TILE) for i in range(h1_ref.shape[0] // SUB_TILE)]
    pre = [(_dot(u2_ref[sl, :], wgate_ref[...]), _dot(ya_ref[sl, :], wa_ref[...]), _dot(yb_ref[sl, :], wb_ref[...]))
           for sl in subs]
    merged = []
    for gl, pa, pb in pre:
        gates = jax.nn.sigmoid(gl + bgate_ref[...])
        merged.append((gates[:, :d_model] * pa + gates[:, d_model:] * pb).astype(_BF16))
    h2s = [h1_ref[sl, :] + _dot(mg, wo_ref[...]) for sl, mg in zip(subs, merged)]
    acts = [_gate_up(_rms(h2, g2_ref[...]).astype(_BF16), wg_ref, wu_ref) for h2 in h2s]
    ys = [_dot(a, wd_ref[...]) for a in acts]
    for sl, h2, y in zip(subs, h2s, ys):
        out_ref[sl, :] = _rms(h2 + 0.5 * y, gf_ref[...])


def _t5_bucket_const(dist):
    max_exact = N_BUCKETS // 2
    n = jnp.maximum(dist, 0)
    nf = jnp.maximum(n, 1).astype(jnp.float32)
    large = max_exact + (jnp.log(nf / max_exact) / math.log(MAX_DISTANCE / max_exact)
                         * (N_BUCKETS - max_exact)).astype(jnp.int32)
    large = jnp.minimum(large, N_BUCKETS - 1)
    return jnp.where(n < max_exact, n, large)


def _bias_kernel(table_ref, idx_ref, out_ref):
    h = pl.program_id(0)
    idx = idx_ref[0]
    acc = jnp.full(idx.shape, MASKED, _F32)
    for b in range(N_BUCKETS):
        acc = jnp.where(idx == b, table_ref[b, h] * LOG2E, acc)
    out_ref[0, 0] = acc
    prev_key = lax.broadcasted_iota(jnp.int32, idx.shape, 0) < BLOCK
    out_ref[1, 0] = jnp.where(prev_key, MASKED, acc)


def _band_bias(rel_bias):
    ki = jnp.arange(2 * BLOCK)[:, None]
    qi = jnp.arange(BLOCK)[None, :]
    dist = qi + BLOCK - ki
    variants = [(d, w // d) for w, d in DIL_GROUPS] + [(1, B_WINDOW - 1)]
    idx = jnp.stack([jnp.where((dist >= 0) & (dist <= max_steps), _t5_bucket_const(dist * stride), -1)
                     for stride, max_steps in variants]).astype(jnp.int32)
    n_heads = rel_bias.shape[1]
    last = len(variants) - 1
    return pl.pallas_call(
        _bias_kernel,
        grid=(n_heads,),
        in_specs=[pl.BlockSpec(memory_space=pltpu.SMEM),
                  pl.BlockSpec((1, 2 * BLOCK, BLOCK),
                               lambda h: (jnp.minimum(h // A_HEADS_PER_GROUP, last), 0, 0))],
        out_specs=pl.BlockSpec((2, 1, 2 * BLOCK, BLOCK), lambda h: (0, h, 0, 0)),
        out_shape=jax.ShapeDtypeStruct((2, n_heads, 2 * BLOCK, BLOCK), _F32),
        name="band_bias",
    )(rel_bias.astype(_F32), idx)


def _const_spec(shape):
    nd = len(shape)
    return pl.BlockSpec(shape, lambda *_: (0,) * nd, pipeline_mode=pl.Buffered(1))


def _layer(x, ffn1_norm, ffn1_w_gate, ffn1_w_up, ffn1_w_down, mix_norm, w_in, b_in,
           w_branch_a, w_branch_b, w_out, sinks, rel_bias,
           ffn2_norm, ffn2_w_gate, ffn2_w_up, ffn2_w_down, final_norm):
    batch, seq, d_model = x.shape
    tokens = batch * seq
    tm = TOKEN_TILE
    tiles_per_seq = seq // tm
    n_tiles = tokens // tm
    d_ff = ffn1_w_gate.shape[1]
    a_w = A_HEADS * HEAD_DIM
    scale = HEAD_DIM ** -0.5 * LOG2E

    aq, ak, av = w_in[:, :a_w], w_in[:, a_w:2 * a_w], w_in[:, 2 * a_w:3 * a_w]
    baq, bak, bav = b_in[:a_w], b_in[a_w:2 * a_w], b_in[2 * a_w:3 * a_w]
    off = 3 * a_w
    bq_w, bq_b = w_in[:, off:off + B_OUT], b_in[off:off + B_OUT]
    bkv_w, bkv_b = w_in[:, off + B_OUT:off + B_OUT + 2 * B_KV_W], b_in[off + B_OUT:off + B_OUT + 2 * B_KV_W]
    gate_w, gate_b = w_in[:, off + B_W:], b_in[off + B_W:]
    perm = np.concatenate([np.r_[j * HEAD_DIM:(j + 1) * HEAD_DIM,
                                 (j + 4) * HEAD_DIM:(j + 5) * HEAD_DIM] for j in range(B_Q_HEADS // 2)])
    w_cols, b_cols = [], []
    for gi in range(len(DIL_GROUPS)):
        sl = slice(gi * A_OUT, (gi + 1) * A_OUT)
        w_cols += [aq[:, sl] * scale, ak[:, sl], av[:, sl]]
        b_cols += [baq[sl] * scale, bak[sl], bav[sl]]
    w_cols += [bq_w[:, perm] * scale, bkv_w]
    b_cols += [bq_b[perm] * scale, bkv_b]
    wqkv = jnp.concatenate(w_cols, axis=1).astype(_BF16)
    bqkv = jnp.concatenate(b_cols)[None, :].astype(_F32)
    wb_perm = w_branch_b[perm, :].astype(_BF16)

    row = lambda v: v[None, :].astype(_F32)
    x2 = x.reshape(tokens, d_model)

    bias = _band_bias(rel_bias)
    bias_nat = jnp.concatenate([bias[:, :A_HEADS_PER_GROUP], bias[:, A_HEADS:]], axis=1)
    sink_rows = jnp.repeat((sinks.astype(_F32) * LOG2E).reshape(B_KV_HEADS, B_Q_HEADS // B_KV_HEADS), BLOCK,
                           axis=1)[:, None, :]

    cur = lambda i: jnp.minimum(i, n_tiles - 1)
    prv = lambda i: jnp.maximum(i - 1, 0)
    d1, d2 = DIL_GROUPS[1][1], DIL_GROUPS[2][1]
    n_sub = tm // SUB_TILE
    h1, u2, qkv1, qkv2, yb, o0, l0 = pl.pallas_call(
        functools.partial(_stage1_kernel, tiles_per_seq=tiles_per_seq),
        grid=(n_tiles + 1,),
        in_specs=[pl.BlockSpec((tm, d_model), lambda i: (cur(i), 0)), _const_spec((1, d_model)),
                  _const_spec((d_model, d_ff)), _const_spec((d_model, d_ff)), _const_spec((d_ff, d_model)),
                  _const_spec((1, d_model)), _const_spec((d_model, 3 * GROUP_W + B_W)),
                  _const_spec((1, 3 * GROUP_W + B_W)),
                  _const_spec(bias_nat.shape), _const_spec(sink_rows.shape)],
        out_specs=[pl.BlockSpec((tm, d_model), lambda i: (cur(i), 0)),
                   pl.BlockSpec((tm, d_model), lambda i: (cur(i), 0)),
                   pl.BlockSpec((1, d1, tm // d1, GROUP_W),
                                lambda i: (cur(i) // tiles_per_seq, 0, cur(i) % tiles_per_seq, 0)),
                   pl.BlockSpec((1, d2, tm // d2, GROUP_W),
                                lambda i: (cur(i) // tiles_per_seq, 0, cur(i) % tiles_per_seq, 0)),
                   pl.BlockSpec((tm, B_OUT), lambda i: (prv(i), 0)),
                   pl.BlockSpec((A_OUT // LANES, tm, LANES), lambda i: (0, prv(i), 0)),
                   pl.BlockSpec((A_OUT // LANES, tm, LANES), lambda i: (0, prv(i), 0))],
        out_shape=[jax.ShapeDtypeStruct((tokens, d_model), _F32),
                   jax.ShapeDtypeStruct((tokens, d_model), _BF16),
                   jax.ShapeDtypeStruct((batch, d1, seq // d1, GROUP_W), _BF16),
                   jax.ShapeDtypeStruct((batch, d2, seq // d2, GROUP_W), _BF16),
                   jax.ShapeDtypeStruct((tokens, B_OUT), _BF16),
                   jax.ShapeDtypeStruct((A_OUT // LANES, tokens, LANES), _F32),
                   jax.ShapeDtypeStruct((A_OUT // LANES, tokens, LANES), _F32)],
        scratch_shapes=[pltpu.VMEM((n_sub, 2 * GROUP_W // LANES, SUB_TILE, LANES), _F32),
                        pltpu.VMEM((2, tm, GROUP_W), _BF16), pltpu.VMEM((2, tm, B_W), _BF16),
                        pltpu.VMEM((2, BLOCK, GROUP_W), _BF16), pltpu.VMEM((2, BLOCK, B_W), _BF16),
                        pltpu.VMEM((N_SLOTS, 2 * BLOCK, 2 * BLOCK), _F32),
                        pltpu.VMEM((N_SLOTS, 2 * BLOCK, 2 * BLOCK), _BF16)],
        compiler_params=pltpu.CompilerParams(dimension_semantics=("arbitrary",), vmem_limit_bytes=VMEM_LIMIT),
        name="stage1_ffn_qkv_attn",
    )(x2, row(ffn1_norm), ffn1_w_gate.astype(_BF16), ffn1_w_up.astype(_BF16), ffn1_w_down.astype(_BF16),
      row(mix_norm), wqkv, bqkv, bias_nat, sink_rows)

    params = pltpu.CompilerParams(dimension_semantics=("parallel",), vmem_limit_bytes=VMEM_LIMIT)

    n_heads = bias.shape[1]
    ya = pl.pallas_call(
        _stage2_kernel,
        grid=(batch,),
        in_specs=[pl.BlockSpec((1, d1, seq // d1, GROUP_W), lambda b: (b, 0, 0, 0)),
                  pl.BlockSpec((1, d2, seq // d2, GROUP_W), lambda b: (b, 0, 0, 0)),
                  pl.BlockSpec((A_OUT // LANES, seq, LANES), lambda b: (0, b, 0)),
                  pl.BlockSpec((A_OUT // LANES, seq, LANES), lambda b: (0, b, 0)),
                  pl.BlockSpec((None, n_heads, 2 * BLOCK, BLOCK), lambda b: (0, 0, 0, 0),
                               pipeline_mode=pl.Buffered(1))],
        out_specs=pl.BlockSpec((seq, A_OUT), lambda b: (b, 0)),
        out_shape=jax.ShapeDtypeStruct((tokens, A_OUT), _BF16),
        scratch_shapes=[pltpu.VMEM((A_OUT // LANES, seq, LANES), _F32) for _ in range(3)]
        + [pltpu.VMEM((N_SLOTS, 2 * BLOCK, 2 * BLOCK), _F32), pltpu.VMEM((N_SLOTS, 2 * BLOCK, 2 * BLOCK), _BF16)],
        compiler_params=params,
        name="stage2_dilated_attention",
    )(qkv1, qkv2, o0, l0, bias)

    tile = lambda w: pl.BlockSpec((tm, w), lambda i: (i, 0))
    out = pl.pallas_call(
        _stage3_kernel,
        grid=(n_tiles,),
        in_specs=[tile(d_model), tile(d_model), tile(A_OUT), tile(B_OUT),
                  _const_spec((d_model, 2 * d_model)), _const_spec((1, 2 * d_model)),
                  _const_spec((A_OUT, d_model)), _const_spec((B_OUT, d_model)), _const_spec((d_model, d_model)),
                  _const_spec((1, d_model)),
                  _const_spec((d_model, d_ff)), _const_spec((d_model, d_ff)), _const_spec((d_ff, d_model)),
                  _const_spec((1, d_model))],
        out_specs=tile(d_model),
        out_shape=jax.ShapeDtypeStruct((tokens, d_model), _F32),
        compiler_params=params,
        name="stage3_merge_ffn",
    )(h1, u2, ya, yb, gate_w.astype(_BF16), row(gate_b), w_branch_a.astype(_BF16), wb_perm,
      w_out.astype(_BF16), row(ffn2_norm), ffn2_w_gate.astype(_BF16), ffn2_w_up.astype(_BF16),
      ffn2_w_down.astype(_BF16), row(final_norm))
    return out.reshape(batch, seq, d_model)


def kernel(x, ffn1_norm, ffn1_w_gate, ffn1_w_up, ffn1_w_down, mix_norm, w_in, b_in, w_branch_a, w_branch_b,
           w_out, sinks, rel_bias, ffn2_norm, ffn2_w_gate, ffn2_w_up, ffn2_w_down, final_norm):
    assert ffn1_norm.shape[0] == 1, "single-layer kernel"
    return _layer(x, ffn1_norm[0], ffn1_w_gate[0], ffn1_w_up[0], ffn1_w_down[0], mix_norm[0], w_in[0], b_in[0],
                  w_branch_a[0], w_branch_b[0], w_out[0], sinks[0], rel_bias,
                  ffn2_norm[0], ffn2_w_gate[0], ffn2_w_up[0], ffn2_w_down[0], final_norm)
```

```python
import functools
import math

import numpy as np
import jax
import jax.numpy as jnp
from jax import lax
from jax.experimental import pallas as pl
from jax.experimental.pallas import tpu as pltpu

HEAD_DIM = 64
DIL_GROUPS = ((128, 1), (512, 4), (2048, 16))
A_HEADS_PER_GROUP = 4
A_HEADS = A_HEADS_PER_GROUP * len(DIL_GROUPS)
A_OUT = A_HEADS_PER_GROUP * HEAD_DIM
B_Q_HEADS = 8
B_KV_HEADS = 2
B_WINDOW = 128
B_OUT = B_Q_HEADS * HEAD_DIM
B_KV_W = B_KV_HEADS * HEAD_DIM
BLOCK = 128
N_BUCKETS = 32
MAX_DISTANCE = 2048
EPS = 1e-6
GROUP_W = 3 * A_OUT
B_W = B_OUT + 2 * B_KV_W
LANES = 128
SUBLANES = 8
MXU_COLS = 256
FF_CHUNKS = (768, 768, 768, 512)
LOG2E = math.log2(math.e)
MASKED = -1e30

TOKEN_TILE = 512
SUB_TILE = 256
VMEM_LIMIT = 60 * 1024 * 1024

_BF16 = jnp.bfloat16
_F32 = jnp.float32


def _rms(x, g):
    return (x * lax.rsqrt(jnp.mean(x * x, axis=-1, keepdims=True) + EPS)) * g


def _dot(a, b):
    return jnp.dot(a, b, preferred_element_type=_F32)


def _gate_up(xb, wg_ref, wu_ref, gap=None):
    acts = []
    c0 = 0
    for w in FF_CHUNKS:
        if gap is not None:
            gap()
        g = _dot(xb, wg_ref[:, c0:c0 + w])
        u = _dot(xb, wu_ref[:, c0:c0 + w])
        c0 += w
        acts.append(((g * jax.nn.sigmoid(g)) * u).astype(_BF16))
    return jnp.concatenate(acts, axis=1)


def _dot_chunks(a, w_ref, chunks, gap):
    outs = []
    c0 = 0
    for w in chunks:
        gap()
        outs.append(_dot(a, w_ref[:, c0:c0 + w]))
        c0 += w
    return jnp.concatenate(outs, axis=1)


KEY_CHUNK = 32
LOOKAHEAD = 4
G2_CLASSES_PER_STEP = 8
G1_CLASSES_PER_STEP = 2
N_SLOTS = 8
D_CHUNKS = (512, 512)
UNIT_SCHEDULE = (((2, 0),) + ((1, 1),) * 7 + ((2, 2),) * 4
                 + ((2, 1), (2, 1), (2, 1), (1, 1), (0, 1), (0, 2), (0, 2), (0, 1)))
Z_ORDER = (1, 2, 0, 3)


def _scores(scr, q2a, q2b, half_a, half_b, kc, bias_a, bias_b, sink_row=None):
    s_ref, _, slot = scr
    kn = kc.shape[0]
    lane = lax.broadcasted_iota(jnp.int32, q2a.shape, 1)
    keep = lambda q2, half: jnp.where((lane < HEAD_DIM) if half == 0 else (lane >= HEAD_DIM), q2, jnp.zeros_like(q2))
    qm = jnp.concatenate([keep(q2a, half_a), keep(q2b, half_b)], axis=0)
    s = lax.dot_general(kc, qm, (((1,), (1,)), ((), ())), preferred_element_type=_F32)
    s = s + jnp.concatenate([bias_a, bias_b], axis=1)
    s_ref[slot, :kn, :] = s
    m = jnp.max(s, axis=0, keepdims=True)
    if sink_row is not None:
        m = jnp.maximum(m, sink_row)
    return m


def _softmax_pv(scr, m, vc, sink_row=None):
    s_ref, p_ref, slot = scr
    kn = vc.shape[0]
    acc = jnp.zeros((SUBLANES, m.shape[1]), _F32)
    for c in range(kn // KEY_CHUNK):
        rows = slice(c * KEY_CHUNK, (c + 1) * KEY_CHUNK)
        p = jnp.exp2(s_ref[slot, rows, :] - m)
        for t in range(KEY_CHUNK // SUBLANES):
            acc = acc + p[t * SUBLANES:(t + 1) * SUBLANES]
        p_ref[slot, rows, :] = p.astype(_BF16)
    l = jnp.sum(acc, axis=0, keepdims=True)
    if sink_row is not None:
        l = l + jnp.exp2(sink_row - m)
    o_t = lax.dot_general(vc, p_ref[slot, :kn, :], (((0,), (0,)), ((), ())), preferred_element_type=_F32)
    return o_t, l


def _pair_finish(o_t, m, l):
    lse = (m + jnp.log2(l)) * (1.0 / LOG2E)
    on = jnp.concatenate([o_t[:HEAD_DIM, :BLOCK] / l[:, :BLOCK], o_t[HEAD_DIM:, BLOCK:] / l[:, BLOCK:]], axis=0)
    lb = jnp.concatenate([jnp.broadcast_to(lse[:, :BLOCK], (HEAD_DIM, BLOCK)),
                          jnp.broadcast_to(lse[:, BLOCK:], (HEAD_DIM, BLOCK))], axis=0)
    return on.T, lb.T


def _run_units(units):
    state = {}
    for i in range(len(units) + LOOKAHEAD):
        if i < len(units):
            state[i] = units[i][0]()
        k = i - LOOKAHEAD
        if k >= 0:
            units[k][1](state.pop(k))


class _UnitQueue:
    def __init__(self, units, schedule):
        self.units, self.next, self.pending, self.schedule = units, 0, [], iter(schedule)

    def _step(self, n_start, n_finish):
        for _ in range(min(n_finish, len(self.pending))):
            k, m = self.pending.pop(0)
            self.units[k][1](m)
        for k in range(self.next, min(self.next + n_start, len(self.units))):
            self.pending.append((k, self.units[k][0]()))
        self.next = min(self.next + n_start, len(self.units))

    def gap(self):
        self._step(*next(self.schedule))

    def flush(self):
        assert self.next == len(self.units) and next(self.schedule, None) is None
        self._step(0, len(self.pending))


def _stage1_kernel(x_ref, g1_ref, wg_ref, wu_ref, wd_ref, gmix_ref, wqkv_ref, bqkv_ref, bias_ref, sink_ref,
                   h1_ref, u2_ref, qkv1_ref, qkv2_ref, yb_ref, o0_ref, l0_ref,
                   zs_ref, cur0_ref, curb_ref, prev0_ref, prevb_ref, s_ref, p_ref, *, tiles_per_seq):
    i = pl.program_id(0)
    par = lax.rem(i, 2)
    opar = 1 - par
    opens_seq = lax.rem(i + tiles_per_seq - 1, tiles_per_seq) == 0
    tm = x_ref.shape[0]
    n_pair = A_OUT // LANES
    n_qp = B_OUT // LANES

    @pl.when(i == 0)
    def _():
        cur0_ref[...] = jnp.zeros_like(cur0_ref)
        curb_ref[...] = jnp.zeros_like(curb_ref)
        prev0_ref[...] = jnp.zeros_like(prev0_ref)
        prevb_ref[...] = jnp.zeros_like(prevb_ref)

    def cols(base, j):
        return slice(base + j * LANES, base + (j + 1) * LANES)

    def nat_units(c, slot0):
        rows = slice(c * BLOCK, (c + 1) * BLOCK)
        variant = jnp.where(opens_seq, 1, 0) if c == 0 else 0

        def keys(cur_ref, prev_ref, cs):
            if c == 0:
                return jnp.concatenate([prev_ref[opar, :, cs], cur_ref[opar, rows, cs]], axis=0)
            return cur_ref[opar, (c - 1) * BLOCK:(c + 1) * BLOCK, cs]

        units = []
        for j in range(n_pair):
            scr = (s_ref, p_ref, (slot0 + j) % N_SLOTS)

            def first_pass(j=j, scr=scr):
                q2 = cur0_ref[opar, rows, cols(0, j)]
                return _scores(scr, q2, q2, 0, 1, keys(cur0_ref, prev0_ref, cols(A_OUT, j)),
                               bias_ref[variant, 2 * j], bias_ref[variant, 2 * j + 1])

            def second_pass(m, j=j, scr=scr):
                o_t, l = _softmax_pv(scr, m, keys(cur0_ref, prev0_ref, cols(2 * A_OUT, j)))
                o0_ref[j, rows, :], l0_ref[j, rows, :] = _pair_finish(o_t, m, l)

            units.append((first_pass, second_pass))
        done = {}
        for half in range(2):
            for u in range(n_qp // 2):
                scr = (s_ref, p_ref, (slot0 + n_pair + half * (n_qp // 2) + u) % N_SLOTS)

                def first_pass(half=half, u=u, scr=scr):
                    ha = A_HEADS_PER_GROUP + n_qp * half + 2 * u
                    sink_row = sink_ref[half, :, 2 * u * BLOCK:(2 * u + 2) * BLOCK]
                    return _scores(scr, curb_ref[opar, rows, cols(0, 2 * u)], curb_ref[opar, rows, cols(0, 2 * u + 1)],
                                   half, half, keys(curb_ref, prevb_ref, slice(B_OUT, B_OUT + B_KV_W)),
                                   bias_ref[variant, ha], bias_ref[variant, ha + 1], sink_row=sink_row)

                def second_pass(m, half=half, u=u, scr=scr):
                    sink_row = sink_ref[half, :, 2 * u * BLOCK:(2 * u + 2) * BLOCK]
                    vc = keys(curb_ref, prevb_ref, slice(B_OUT + B_KV_W, B_OUT + 2 * B_KV_W))
                    o_t, l = _softmax_pv(scr, m, vc, sink_row=sink_row)
                    done[half, u] = o_t[half * HEAD_DIM:(half + 1) * HEAD_DIM] / l
                    if half == 1:
                        for jj in range(2):
                            j = 2 * u + jj
                            tile_t = jnp.concatenate([done[0, u][:, jj * BLOCK:(jj + 1) * BLOCK],
                                                      done[1, u][:, jj * BLOCK:(jj + 1) * BLOCK]], axis=0)
                            yb_ref[rows, cols(0, j)] = tile_t.T.astype(_BF16)

                units.append((first_pass, second_pass))
        return units

    units = []
    for c in range(tm // BLOCK):
        units += nat_units(c, c * (n_pair + n_qp))
    att = _UnitQueue(units, UNIT_SCHEDULE)

    subs = [slice(k * SUB_TILE, (k + 1) * SUB_TILE) for k in range(tm // SUB_TILE)]
    xs = [x_ref[sl, :] for sl in subs]
    xbs = [_rms(x, g1_ref[...]).astype(_BF16) for x in xs]
    acts = [_gate_up(xb, wg_ref, wu_ref, att.gap) for xb in xbs]
    ys = [_dot_chunks(a, wd_ref, D_CHUNKS, att.gap) for a in acts]
    u2s = []
    for sl, x, y in zip(subs, xs, ys):
        h1 = x + 0.5 * y
        h1_ref[sl, :] = h1
        u2 = _rms(h1, gmix_ref[...]).astype(_BF16)
        u2_ref[sl, :] = u2
        u2s.append(u2)
    prev0_ref[par] = cur0_ref[opar, tm - BLOCK:, :]
    prevb_ref[par] = curb_ref[opar, tm - BLOCK:, :]
    n_slab = GROUP_W // LANES
    for k, (sl, u2) in enumerate(zip(subs, u2s)):
        for q in Z_ORDER:
            cs = slice(q * GROUP_W, (q + 1) * GROUP_W)
            att.gap()
            z = _dot(u2, wqkv_ref[:, cs]) + bqkv_ref[:, cs]
            if q == 0:
                cur0_ref[par, sl, :] = z.astype(_BF16)
            elif q == len(DIL_GROUPS):
                curb_ref[par, sl, :] = z.astype(_BF16)
            else:
                out_ref = (qkv1_ref, qkv2_ref)[q - 1]
                d = DIL_GROUPS[q][1]
                n = SUB_TILE // d
                for j in range(n_slab):
                    zs_ref[k, (q - 1) * n_slab + j] = z[:, j * LANES:(j + 1) * LANES]
                for j in range(n_slab):
                    for r in range(d):
                        rows = zs_ref[k, (q - 1) * n_slab + j, pl.ds(r, n, stride=d), :]
                        out_ref[0, r, k * n:(k + 1) * n, j * LANES:(j + 1) * LANES] = rows.astype(_BF16)
    att.flush()


def _stage2_kernel(qkv1_ref, qkv2_ref, o0_ref, l0_ref, bias_ref,
                   ya_ref, y_ref, o2_ref, l2_ref, s_ref, p_ref):
    n_pair = A_OUT // LANES
    seq = ya_ref.shape[0]

    def cols(base, j):
        return slice(base + j * LANES, base + (j + 1) * LANES)

    def bias_pair(h, first):
        if first:
            return bias_ref[h, BLOCK:, :], bias_ref[h + 1, BLOCK:, :]
        return bias_ref[h], bias_ref[h + 1]

    def pair_unit(slot, load, h, first, finish):
        scr = (s_ref, p_ref, slot)

        def first_pass():
            q2 = load(0, True)
            return _scores(scr, q2, q2, 0, 1, load(A_OUT, False), *bias_pair(h, first))

        def second_pass(m):
            o_t, l = _softmax_pv(scr, m, load(2 * A_OUT, False))
            finish(*_pair_finish(o_t, m, l))

        return first_pass, second_pass

    d1, d2 = DIL_GROUPS[1][1], DIL_GROUPS[2][1]
    assert G2_CLASSES_PER_STEP % d1 == 0

    def g2_body(it, carry):
        units = []
        for c in range(G2_CLASSES_PER_STEP):
            r = it * G2_CLASSES_PER_STEP + c
            for j in range(n_pair):
                def load(base, is_q, r=r, j=j):
                    return qkv2_ref[0, r, :, cols(base, j)]

                def finish(out, lse2, c=c, j=j):
                    start = (c % d1) * (seq // d1) + it * (G2_CLASSES_PER_STEP // d1) + c // d1
                    o2_ref[j, pl.ds(start, BLOCK, stride=d1), :] = out
                    l2_ref[j, pl.ds(start, BLOCK, stride=d1), :] = lse2

                units.append(pair_unit((c * n_pair + j) % N_SLOTS, load, 8 + 2 * j, True, finish))
        _run_units(units)
        return carry

    lax.fori_loop(0, d2 // G2_CLASSES_PER_STEP, g2_body, 0)

    nb1 = seq // d1 // BLOCK

    def g1_body(it, carry):
        units = []
        for c in range(G1_CLASSES_PER_STEP):
            r = it * G1_CLASSES_PER_STEP + c
            for b in range(nb1):
                first = b == 0
                row0 = b * BLOCK
                k0 = 0 if first else (b - 1) * BLOCK
                kn = BLOCK if first else 2 * BLOCK
                for j in range(n_pair):
                    def load(base, is_q, r=r, j=j, row0=row0, k0=k0, kn=kn):
                        return qkv1_ref[0, r, pl.ds(row0, BLOCK) if is_q else pl.ds(k0, kn), cols(base, j)]

                    def finish(o1, l1, r=r, j=j, row0=row0):
                        tok = pl.ds(row0 * d1 + r, BLOCK, stride=d1)
                        o0, l0 = o0_ref[j, tok, :], l0_ref[j, tok, :]
                        own = pl.ds(pl.multiple_of(r * (seq // d1) + row0, BLOCK), BLOCK)
                        o2, l2 = o2_ref[j, own, :], l2_ref[j, own, :]
                        top = jnp.maximum(jnp.maximum(l0, l1), l2)
                        w0, w1, w2 = jnp.exp(l0 - top), jnp.exp(l1 - top), jnp.exp(l2 - top)
                        y_ref[j, tok, :] = (w0 * o0 + w1 * o1 + w2 * o2) / (w0 + w1 + w2)

                    units.append(pair_unit(len(units) % N_SLOTS, load, 4 + 2 * j, first, finish))
        _run_units(units)
        return carry

    lax.fori_loop(0, d1 // G1_CLASSES_PER_STEP, g1_body, 0)
    for j in range(n_pair):
        ya_ref[:, cols(0, j)] = y_ref[j].astype(_BF16)


def _stage3_kernel(h1_ref, u2_ref, ya_ref, yb_ref, wgate_ref, bgate_ref, wa_ref, wb_ref, wo_ref,
                   g2_ref, wg_ref, wu_ref, wd_ref, gf_ref, out_ref):
    d_model = h1_ref.shape[1]
    subs = [slice(i * SUB_TILE, (i + 1) * SUB_TILE) for i in range(h1_ref.shape[0] // SUB_TILE)]
    pre = [(_dot(u2_ref[sl, :], wgate_ref[...]), _dot(ya_ref[sl, :], wa_ref[...]), _dot(yb_ref[sl, :], wb_ref[...]))
           for sl in subs]
    merged = []
    for gl, pa, pb in pre:
        gates = jax.nn.sigmoid(gl + bgate_ref[...])
        merged.append((gates[:, :d_model] * pa + gates[:, d_model:] * pb).astype(_BF16))
    h2s = [h1_ref[sl, :] + _dot(mg, wo_ref[...]) for sl, mg in zip(subs, merged)]
    acts = [_gate_up(_rms(h2, g2_ref[...]).astype(_BF16), wg_ref, wu_ref) for h2 in h2s]
    ys = [_dot(a, wd_ref[...]) for a in acts]
    for sl, h2, y in zip(subs, h2s, ys):
        out_ref[sl, :] = _rms(h2 + 0.5 * y, gf_ref[...])


def _t5_bucket_np(dist):
    max_exact = N_BUCKETS // 2
    n = np.maximum(dist, 0)
    nf = np.maximum(n, 1).astype(np.float32)
    large = max_exact + (np.log(nf / max_exact) / math.log(MAX_DISTANCE / max_exact)
                         * (N_BUCKETS - max_exact)).astype(np.int32)
    large = np.minimum(large, N_BUCKETS - 1)
    return np.where(n < max_exact, n, large)


def _bias_kernel(table_ref, idx_ref, out_ref):
    h = pl.program_id(0)
    idx = idx_ref[0]
    acc = jnp.full(idx.shape, MASKED, _F32)
    for b in range(N_BUCKETS):
        acc = jnp.where(idx == b, table_ref[b, h] * LOG2E, acc)
    out_ref[0, 0] = acc
    prev_key = lax.broadcasted_iota(jnp.int32, idx.shape, 0) < BLOCK
    out_ref[1, 0] = jnp.where(prev_key, MASKED, acc)


def _band_bias(rel_bias):
    ki = np.arange(2 * BLOCK)[:, None]
    qi = np.arange(BLOCK)[None, :]
    dist = qi + BLOCK - ki
    variants = [(d, w // d) for w, d in DIL_GROUPS] + [(1, B_WINDOW - 1)]
    idx = jnp.asarray(np.stack([np.where((dist >= 0) & (dist <= max_steps), _t5_bucket_np(dist * stride), -1)
                                for stride, max_steps in variants]).astype(np.int32))
    n_heads = rel_bias.shape[1]
    last = len(variants) - 1
    return pl.pallas_call(
        _bias_kernel,
        grid=(n_heads,),
        in_specs=[pl.BlockSpec(memory_space=pltpu.SMEM),
                  pl.BlockSpec((1, 2 * BLOCK, BLOCK),
                               lambda h: (jnp.minimum(h // A_HEADS_PER_GROUP, last), 0, 0))],
        out_specs=pl.BlockSpec((2, 1, 2 * BLOCK, BLOCK), lambda h: (0, h, 0, 0)),
        out_shape=jax.ShapeDtypeStruct((2, n_heads, 2 * BLOCK, BLOCK), _F32),
        name="band_bias",
    )(rel_bias.astype(_F32), idx)


def _const_spec(shape):
    nd = len(shape)
    return pl.BlockSpec(shape, lambda *_: (0,) * nd, pipeline_mode=pl.Buffered(1))


def _layer(x, ffn1_norm, ffn1_w_gate, ffn1_w_up, ffn1_w_down, mix_norm, w_in, b_in,
           w_branch_a, w_branch_b, w_out, sinks, rel_bias,
           ffn2_norm, ffn2_w_gate, ffn2_w_up, ffn2_w_down, final_norm):
    batch, seq, d_model = x.shape
    tokens = batch * seq
    tm = TOKEN_TILE
    tiles_per_seq = seq // tm
    n_tiles = tokens // tm
    d_ff = ffn1_w_gate.shape[1]
    a_w = A_HEADS * HEAD_DIM
    scale = HEAD_DIM ** -0.5 * LOG2E

    off = 3 * a_w
    gate_w, gate_b = w_in[:, off + B_W:], b_in[off + B_W:]
    perm = np.concatenate([np.r_[j * HEAD_DIM:(j + 1) * HEAD_DIM,
                                 (j + 4) * HEAD_DIM:(j + 5) * HEAD_DIM] for j in range(B_Q_HEADS // 2)])
    col_idx, col_scale = [], []
    for gi in range(len(DIL_GROUPS)):
        for part, part_scale in ((0, scale), (1, 1.0), (2, 1.0)):
            col_idx.append(part * a_w + gi * A_OUT + np.arange(A_OUT))
            col_scale.append(np.full(A_OUT, part_scale, np.float32))
    col_idx += [off + perm, off + B_OUT + np.arange(2 * B_KV_W)]
    col_scale += [np.full(B_OUT, scale, np.float32), np.ones(2 * B_KV_W, np.float32)]
    col_idx, col_scale = np.concatenate(col_idx), np.concatenate(col_scale)
    wqkv = (jnp.take(w_in, col_idx, axis=1) * col_scale).astype(_BF16)
    bqkv = (jnp.take(b_in, col_idx) * col_scale)[None, :].astype(_F32)
    wb_perm = w_branch_b[perm, :].astype(_BF16)

    row = lambda v: v[None, :].astype(_F32)
    x2 = x.reshape(tokens, d_model)

    bias = _band_bias(rel_bias)
    bias_nat = jnp.concatenate([bias[:, :A_HEADS_PER_GROUP], bias[:, A_HEADS:]], axis=1)
    sink_rows = jnp.repeat((sinks.astype(_F32) * LOG2E).reshape(B_KV_HEADS, B_Q_HEADS // B_KV_HEADS), BLOCK,
                           axis=1)[:, None, :]

    cur = lambda i: jnp.minimum(i, n_tiles - 1)
    prv = lambda i: jnp.maximum(i - 1, 0)
    d1, d2 = DIL_GROUPS[1][1], DIL_GROUPS[2][1]
    n_sub = tm // SUB_TILE
    h1, u2, qkv1, qkv2, yb, o0, l0 = pl.pallas_call(
        functools.partial(_stage1_kernel, tiles_per_seq=tiles_per_seq),
        grid=(n_tiles + 1,),
        in_specs=[pl.BlockSpec((tm, d_model), lambda i: (cur(i), 0)), _const_spec((1, d_model)),
                  _const_spec((d_model, d_ff)), _const_spec((d_model, d_ff)), _const_spec((d_ff, d_model)),
                  _const_spec((1, d_model)), _const_spec((d_model, 3 * GROUP_W + B_W)),
                  _const_spec((1, 3 * GROUP_W + B_W)),
                  _const_spec(bias_nat.shape), _const_spec(sink_rows.shape)],
        out_specs=[pl.BlockSpec((tm, d_model), lambda i: (cur(i), 0)),
                   pl.BlockSpec((tm, d_model), lambda i: (cur(i), 0)),
                   pl.BlockSpec((1, d1, tm // d1, GROUP_W),
                                lambda i: (cur(i) // tiles_per_seq, 0, cur(i) % tiles_per_seq, 0)),
                   pl.BlockSpec((1, d2, tm // d2, GROUP_W),
                                lambda i: (cur(i) // tiles_per_seq, 0, cur(i) % tiles_per_seq, 0)),
                   pl.BlockSpec((tm, B_OUT), lambda i: (prv(i), 0)),
                   pl.BlockSpec((A_OUT // LANES, tm, LANES), lambda i: (0, prv(i), 0)),
                   pl.BlockSpec((A_OUT // LANES, tm, LANES), lambda i: (0, prv(i), 0))],
        out_shape=[jax.ShapeDtypeStruct((tokens, d_model), _F32),
                   jax.ShapeDtypeStruct((tokens, d_model), _BF16),
                   jax.ShapeDtypeStruct((batch, d1, seq // d1, GROUP_W), _BF16),
                   jax.ShapeDtypeStruct((batch, d2, seq // d2, GROUP_W), _BF16),
                   jax.ShapeDtypeStruct((tokens, B_OUT), _BF16),
                   jax.ShapeDtypeStruct((A_OUT // LANES, tokens, LANES), _F32),
                   jax.ShapeDtypeStruct((A_OUT // LANES, tokens, LANES), _F32)],
        scratch_shapes=[pltpu.VMEM((n_sub, 2 * GROUP_W // LANES, SUB_TILE, LANES), _F32),
                        pltpu.VMEM((2, tm, GROUP_W), _BF16), pltpu.VMEM((2, tm, B_W), _BF16),
                        pltpu.VMEM((2, BLOCK, GROUP_W), _BF16), pltpu.VMEM((2, BLOCK, B_W), _BF16),
                        pltpu.VMEM((N_SLOTS, 2 * BLOCK, 2 * BLOCK), _F32),
                        pltpu.VMEM((N_SLOTS, 2 * BLOCK, 2 * BLOCK), _BF16)],
        compiler_params=pltpu.CompilerParams(dimension_semantics=("arbitrary",), vmem_limit_bytes=VMEM_LIMIT),
        name="stage1_ffn_qkv_attn",
    )(x2, row(ffn1_norm), ffn1_w_gate.astype(_BF16), ffn1_w_up.astype(_BF16), ffn1_w_down.astype(_BF16),
      row(mix_norm), wqkv, bqkv, bias_nat, sink_rows)

    params = pltpu.CompilerParams(dimension_semantics=("parallel",), vmem_limit_bytes=VMEM_LIMIT)

    n_heads = bias.shape[1]
    ya = pl.pallas_call(
        _stage2_kernel,
        grid=(batch,),
        in_specs=[pl.BlockSpec((1, d1, seq // d1, GROUP_W), lambda b: (b, 0, 0, 0)),
                  pl.BlockSpec((1, d2, seq // d2, GROUP_W), lambda b: (b, 0, 0, 0)),
                  pl.BlockSpec((A_OUT // LANES, seq, LANES), lambda b: (0, b, 0)),
                  pl.BlockSpec((A_OUT // LANES, seq, LANES), lambda b: (0, b, 0)),
                  pl.BlockSpec((None, n_heads, 2 * BLOCK, BLOCK), lambda b: (0, 0, 0, 0),
                               pipeline_mode=pl.Buffered(1))],
        out_specs=pl.BlockSpec((seq, A_OUT), lambda b: (b, 0)),
        out_shape=jax.ShapeDtypeStruct((tokens, A_OUT), _BF16),
        scratch_shapes=[pltpu.VMEM((A_OUT // LANES, seq, LANES), _F32) for _ in range(3)]
        + [pltpu.VMEM((N_SLOTS, 2 * BLOCK, 2 * BLOCK), _F32), pltpu.VMEM((N_SLOTS, 2 * BLOCK, 2 * BLOCK), _BF16)],
        compiler_params=params,
        name="stage2_dilated_attention",
    )(qkv1, qkv2, o0, l0, bias)

    tile = lambda w: pl.BlockSpec((tm, w), lambda i: (i, 0))
    out = pl.pallas_call(
        _stage3_kernel,
        grid=(n_tiles,),
        in_specs=[tile(d_model), tile(d_model), tile(A_OUT), tile(B_OUT),
                  _const_spec((d_model, 2 * d_model)), _const_spec((1, 2 * d_model)),
                  _const_spec((A_OUT, d_model)), _const_spec((B_OUT, d_model)), _const_spec((d_model, d_model)),
                  _const_spec((1, d_model)),
                  _const_spec((d_model, d_ff)), _const_spec((d_model, d_ff)), _const_spec((d_ff, d_model)),
                  _const_spec((1, d_model))],
        out_specs=tile(d_model),
        out_shape=jax.ShapeDtypeStruct((tokens, d_model), _F32),
        compiler_params=params,
        name="stage3_merge_ffn",
    )(h1, u2, ya, yb, gate_w.astype(_BF16), row(gate_b), w_branch_a.astype(_BF16), wb_perm,
      w_out.astype(_BF16), row(ffn2_norm), ffn2_w_gate.astype(_BF16), ffn2_w_up.astype(_BF16),
      ffn2_w_down.astype(_BF16), row(final_norm))
    return out.reshape(batch, seq, d_model)


def kernel(x, ffn1_norm, ffn1_w_gate, ffn1_w_up, ffn1_w_down, mix_norm, w_in, b_in, w_branch_a, w_branch_b,
           w_out, sinks, rel_bias, ffn2_norm, ffn2_w_gate, ffn2_w_up, ffn2_w_down, final_norm):
    assert ffn1_norm.shape[0] == 1, "single-layer kernel"
    return _layer(x, ffn1_norm[0], ffn1_w_gate[0], ffn1_w_up[0], ffn1_w_down[0], mix_norm[0], w_in[0], b_in[0],
                  w_branch_a[0], w_branch_b[0], w_out[0], sinks[0], rel_bias,
                  ffn2_norm[0], ffn2_w_gate[0], ffn2_w_up[0], ffn2_w_down[0], final_norm)
```

```python
import functools
import math

import numpy as np
import jax
import jax.numpy as jnp
from jax import lax
from jax.experimental import pallas as pl
from jax.experimental.pallas import tpu as pltpu

HEAD_DIM = 64
DIL_GROUPS = ((128, 1), (512, 4), (2048, 16))
A_HEADS_PER_GROUP = 4
A_HEADS = A_HEADS_PER_GROUP * len(DIL_GROUPS)
A_OUT = A_HEADS_PER_GROUP * HEAD_DIM
B_Q_HEADS = 8
B_KV_HEADS = 2
B_WINDOW = 128
B_OUT = B_Q_HEADS * HEAD_DIM
B_KV_W = B_KV_HEADS * HEAD_DIM
BLOCK = 128
N_BUCKETS = 32
MAX_DISTANCE = 2048
EPS = 1e-6
GROUP_W = 3 * A_OUT
B_W = B_OUT + 2 * B_KV_W
LANES = 128
SUBLANES = 8
MXU_COLS = 256
FF_CHUNKS = (768, 768, 768, 512)
LOG2E = math.log2(math.e)
MASKED = -1e30

TOKEN_TILE = 512
SUB_TILE = 256
VMEM_LIMIT = 60 * 1024 * 1024

_BF16 = jnp.bfloat16
_F32 = jnp.float32


def _rms(x, g):
    return (x * lax.rsqrt(jnp.mean(x * x, axis=-1, keepdims=True) + EPS)) * g


def _dot(a, b):
    return jnp.dot(a, b, preferred_element_type=_F32)


def _gate_up(xb, wg_ref, wu_ref, gap=None):
    acts = []
    c0 = 0
    for w in FF_CHUNKS:
        if gap is not None:
            gap()
        g = _dot(xb, wg_ref[:, c0:c0 + w])
        u = _dot(xb, wu_ref[:, c0:c0 + w])
        c0 += w
        acts.append(((g * jax.nn.sigmoid(g)) * u).astype(_BF16))
    return jnp.concatenate(acts, axis=1)


def _dot_chunks(a, w_ref, chunks, gap):
    outs = []
    c0 = 0
    for w in chunks:
        gap()
        outs.append(_dot(a, w_ref[:, c0:c0 + w]))
        c0 += w
    return jnp.concatenate(outs, axis=1)


KEY_CHUNK = 32
LOOKAHEAD = 4
G2_CLASSES_PER_STEP = 8
G1_CLASSES_PER_STEP = 2
N_SLOTS = 8
D_CHUNKS = (512, 512)
UNIT_SCHEDULE = (((2, 0),) + ((1, 1),) * 7 + ((2, 2),) * 4
                 + ((2, 1), (2, 1), (2, 1), (1, 1), (0, 1), (0, 2), (0, 2), (0, 1)))
Z_ORDER = (1, 2, 0, 3)


def _scores(scr, q2a, q2b, half_a, half_b, kc, bias_a, bias_b, sink_row=None):
    s_ref, _, slot = scr
    kn = kc.shape[0]
    lane = lax.broadcasted_iota(jnp.int32, q2a.shape, 1)
    keep = lambda q2, half: jnp.where((lane < HEAD_DIM) if half == 0 else (lane >= HEAD_DIM), q2, jnp.zeros_like(q2))
    qm = jnp.concatenate([keep(q2a, half_a), keep(q2b, half_b)], axis=0)
    s = lax.dot_general(kc, qm, (((1,), (1,)), ((), ())), preferred_element_type=_F32)
    s = s + jnp.concatenate([bias_a, bias_b], axis=1)
    s_ref[slot, :kn, :] = s
    m = jnp.max(s, axis=0, keepdims=True)
    if sink_row is not None:
        m = jnp.maximum(m, sink_row)
    return m


def _softmax_pv(scr, m, vc, sink_row=None):
    s_ref, p_ref, slot = scr
    kn = vc.shape[0]
    acc = jnp.zeros((SUBLANES, m.shape[1]), _F32)
    for c in range(kn // KEY_CHUNK):
        rows = slice(c * KEY_CHUNK, (c + 1) * KEY_CHUNK)
        p = jnp.exp2(s_ref[slot, rows, :] - m)
        for t in range(KEY_CHUNK // SUBLANES):
            acc = acc + p[t * SUBLANES:(t + 1) * SUBLANES]
        p_ref[slot, rows, :] = p.astype(_BF16)
    l = jnp.sum(acc, axis=0, keepdims=True)
    if sink_row is not None:
        l = l + jnp.exp2(sink_row - m)
    o_t = lax.dot_general(vc, p_ref[slot, :kn, :], (((0,), (0,)), ((), ())), preferred_element_type=_F32)
    return o_t, l


def _pair_finish(o_t, m, l):
    lse = (m + jnp.log2(l)) * (1.0 / LOG2E)
    on = jnp.concatenate([o_t[:HEAD_DIM, :BLOCK] / l[:, :BLOCK], o_t[HEAD_DIM:, BLOCK:] / l[:, BLOCK:]], axis=0)
    lb = jnp.concatenate([jnp.broadcast_to(lse[:, :BLOCK], (HEAD_DIM, BLOCK)),
                          jnp.broadcast_to(lse[:, BLOCK:], (HEAD_DIM, BLOCK))], axis=0)
    return on.T, lb.T


def _run_units(units):
    state = {}
    for i in range(len(units) + LOOKAHEAD):
        if i < len(units):
            state[i] = units[i][0]()
        k = i - LOOKAHEAD
        if k >= 0:
            units[k][1](state.pop(k))


class _UnitQueue:
    def __init__(self, units, schedule):
        self.units, self.next, self.pending, self.schedule = units, 0, [], iter(schedule)

    def _step(self, n_start, n_finish):
        for _ in range(min(n_finish, len(self.pending))):
            k, m = self.pending.pop(0)
            self.units[k][1](m)
        for k in range(self.next, min(self.next + n_start, len(self.units))):
            self.pending.append((k, self.units[k][0]()))
        self.next = min(self.next + n_start, len(self.units))

    def gap(self):
        self._step(*next(self.schedule))

    def flush(self):
        assert self.next == len(self.units) and next(self.schedule, None) is None
        self._step(0, len(self.pending))


def _stage1_kernel(x_ref, g1_ref, wg_ref, wu_ref, wd_ref, gmix_ref, wqkv_ref, bqkv_ref, bias_ref, sink_ref,
                   h1_ref, u2_ref, qkv1_ref, qkv2_ref, yb_ref, o0_ref, l0_ref,
                   zs_ref, cur0_ref, curb_ref, prev0_ref, prevb_ref, s_ref, p_ref, *, tiles_per_seq):
    i = pl.program_id(0)
    par = lax.rem(i, 2)
    opar = 1 - par
    opens_seq = lax.rem(i + tiles_per_seq - 1, tiles_per_seq) == 0
    tm = x_ref.shape[0]
    n_pair = A_OUT // LANES
    n_qp = B_OUT // LANES

    @pl.when(i == 0)
    def _():
        cur0_ref[...] = jnp.zeros_like(cur0_ref)
        curb_ref[...] = jnp.zeros_like(curb_ref)
        prev0_ref[...] = jnp.zeros_like(prev0_ref)
        prevb_ref[...] = jnp.zeros_like(prevb_ref)

    def cols(base, j):
        return slice(base + j * LANES, base + (j + 1) * LANES)

    def nat_units(c, slot0):
        rows = slice(c * BLOCK, (c + 1) * BLOCK)
        variant = jnp.where(opens_seq, 1, 0) if c == 0 else 0

        def keys(cur_ref, prev_ref, cs):
            if c == 0:
                return jnp.concatenate([prev_ref[opar, :, cs], cur_ref[opar, rows, cs]], axis=0)
            return cur_ref[opar, (c - 1) * BLOCK:(c + 1) * BLOCK, cs]

        units = []
        for j in range(n_pair):
            scr = (s_ref, p_ref, (slot0 + j) % N_SLOTS)

            def first_pass(j=j, scr=scr):
                q2 = cur0_ref[opar, rows, cols(0, j)]
                return _scores(scr, q2, q2, 0, 1, keys(cur0_ref, prev0_ref, cols(A_OUT, j)),
                               bias_ref[variant, 2 * j], bias_ref[variant, 2 * j + 1])

            def second_pass(m, j=j, scr=scr):
                o_t, l = _softmax_pv(scr, m, keys(cur0_ref, prev0_ref, cols(2 * A_OUT, j)))
                o0_ref[j, rows, :], l0_ref[j, rows, :] = _pair_finish(o_t, m, l)

            units.append((first_pass, second_pass))
        done = {}
        for half in range(2):
            for u in range(n_qp // 2):
                scr = (s_ref, p_ref, (slot0 + n_pair + half * (n_qp // 2) + u) % N_SLOTS)

                def first_pass(half=half, u=u, scr=scr):
                    ha = A_HEADS_PER_GROUP + n_qp * half + 2 * u
                    sink_row = sink_ref[half, :, 2 * u * BLOCK:(2 * u + 2) * BLOCK]
                    return _scores(scr, curb_ref[opar, rows, cols(0, 2 * u)], curb_ref[opar, rows, cols(0, 2 * u + 1)],
                                   half, half, keys(curb_ref, prevb_ref, slice(B_OUT, B_OUT + B_KV_W)),
                                   bias_ref[variant, ha], bias_ref[variant, ha + 1], sink_row=sink_row)

                def second_pass(m, half=half, u=u, scr=scr):
                    sink_row = sink_ref[half, :, 2 * u * BLOCK:(2 * u + 2) * BLOCK]
                    vc = keys(curb_ref, prevb_ref, slice(B_OUT + B_KV_W, B_OUT + 2 * B_KV_W))
                    o_t, l = _softmax_pv(scr, m, vc, sink_row=sink_row)
                    done[half, u] = o_t[half * HEAD_DIM:(half + 1) * HEAD_DIM] / l
                    if half == 1:
                        for jj in range(2):
                            j = 2 * u + jj
                            tile_t = jnp.concatenate([done[0, u][:, jj * BLOCK:(jj + 1) * BLOCK],
                                                      done[1, u][:, jj * BLOCK:(jj + 1) * BLOCK]], axis=0)
                            yb_ref[rows, cols(0, j)] = tile_t.T.astype(_BF16)

                units.append((first_pass, second_pass))
        return units

    units = []
    for c in range(tm // BLOCK):
        units += nat_units(c, c * (n_pair + n_qp))
    att = _UnitQueue(units, UNIT_SCHEDULE)

    subs = [slice(k * SUB_TILE, (k + 1) * SUB_TILE) for k in range(tm // SUB_TILE)]
    xs = [x_ref[sl, :] for sl in subs]
    xbs = [_rms(x, g1_ref[...]).astype(_BF16) for x in xs]
    acts = [_gate_up(xb, wg_ref, wu_ref, att.gap) for xb in xbs]
    ys = [_dot_chunks(a, wd_ref, D_CHUNKS, att.gap) for a in acts]
    u2s = []
    for sl, x, y in zip(subs, xs, ys):
        h1 = x + 0.5 * y
        h1_ref[sl, :] = h1
        u2 = _rms(h1, gmix_ref[...]).astype(_BF16)
        u2_ref[sl, :] = u2
        u2s.append(u2)
    prev0_ref[par] = cur0_ref[opar, tm - BLOCK:, :]
    prevb_ref[par] = curb_ref[opar, tm - BLOCK:, :]
    n_slab = GROUP_W // LANES
    for k, (sl, u2) in enumerate(zip(subs, u2s)):
        for q in Z_ORDER:
            cs = slice(q * GROUP_W, (q + 1) * GROUP_W)
            att.gap()
            z = _dot(u2, wqkv_ref[:, cs]) + bqkv_ref[:, cs]
            if q == 0:
                cur0_ref[par, sl, :] = z.astype(_BF16)
            elif q == len(DIL_GROUPS):
                curb_ref[par, sl, :] = z.astype(_BF16)
            else:
                out_ref = (qkv1_ref, qkv2_ref)[q - 1]
                d = DIL_GROUPS[q][1]
                n = SUB_TILE // d
                for j in range(n_slab):
                    zs_ref[k, (q - 1) * n_slab + j] = z[:, j * LANES:(j + 1) * LANES]
                for j in range(n_slab):
                    for r in range(d):
                        rows = zs_ref[k, (q - 1) * n_slab + j, pl.ds(r, n, stride=d), :]
                        out_ref[0, r, k * n:(k + 1) * n, j * LANES:(j + 1) * LANES] = rows.astype(_BF16)
    att.flush()


def _stage2_kernel(qkv1_ref, qkv2_ref, o0_ref, l0_ref, bias_ref,
                   ya_ref, y_ref, o2_ref, l2_ref, s_ref, p_ref):
    n_pair = A_OUT // LANES
    seq = ya_ref.shape[0]

    def cols(base, j):
        return slice(base + j * LANES, base + (j + 1) * LANES)

    def bias_pair(h, first):
        if first:
            return bias_ref[h, BLOCK:, :], bias_ref[h + 1, BLOCK:, :]
        return bias_ref[h], bias_ref[h + 1]

    def pair_unit(slot, load, h, first, finish):
        scr = (s_ref, p_ref, slot)

        def first_pass():
            q2 = load(0, True)
            return _scores(scr, q2, q2, 0, 1, load(A_OUT, False), *bias_pair(h, first))

        def second_pass(m):
            o_t, l = _softmax_pv(scr, m, load(2 * A_OUT, False))
            finish(*_pair_finish(o_t, m, l))

        return first_pass, second_pass

    d1, d2 = DIL_GROUPS[1][1], DIL_GROUPS[2][1]
    assert G2_CLASSES_PER_STEP % d1 == 0

    def g2_body(it, carry):
        units = []
        for c in range(G2_CLASSES_PER_STEP):
            r = it * G2_CLASSES_PER_STEP + c
            for j in range(n_pair):
                def load(base, is_q, r=r, j=j):
                    return qkv2_ref[0, r, :, cols(base, j)]

                def finish(out, lse2, c=c, j=j):
                    start = (c % d1) * (seq // d1) + it * (G2_CLASSES_PER_STEP // d1) + c // d1
                    o2_ref[j, pl.ds(start, BLOCK, stride=d1), :] = out
                    l2_ref[j, pl.ds(start, BLOCK, stride=d1), :] = lse2

                units.append(pair_unit((c * n_pair + j) % N_SLOTS, load, 8 + 2 * j, True, finish))
        _run_units(units)
        return carry

    lax.fori_loop(0, d2 // G2_CLASSES_PER_STEP, g2_body, 0)

    nb1 = seq // d1 // BLOCK

    def g1_body(it, carry):
        units = []
        for c in range(G1_CLASSES_PER_STEP):
            r = it * G1_CLASSES_PER_STEP + c
            for b in range(nb1):
                first = b == 0
                row0 = b * BLOCK
                k0 = 0 if first else (b - 1) * BLOCK
                kn = BLOCK if first else 2 * BLOCK
                for j in range(n_pair):
                    def load(base, is_q, r=r, j=j, row0=row0, k0=k0, kn=kn):
                        return qkv1_ref[0, r, pl.ds(row0, BLOCK) if is_q else pl.ds(k0, kn), cols(base, j)]

                    def finish(o1, l1, r=r, j=j, row0=row0):
                        tok = pl.ds(row0 * d1 + r, BLOCK, stride=d1)
                        o0, l0 = o0_ref[j, tok, :], l0_ref[j, tok, :]
                        own = pl.ds(pl.multiple_of(r * (seq // d1) + row0, BLOCK), BLOCK)
                        o2, l2 = o2_ref[j, own, :], l2_ref[j, own, :]
                        top = jnp.maximum(jnp.maximum(l0, l1), l2)
                        w0, w1, w2 = jnp.exp(l0 - top), jnp.exp(l1 - top), jnp.exp(l2 - top)
                        y_ref[j, tok, :] = (w0 * o0 + w1 * o1 + w2 * o2) / (w0 + w1 + w2)

                    units.append(pair_unit(len(units) % N_SLOTS, load, 4 + 2 * j, first, finish))
        _run_units(units)
        return carry

    lax.fori_loop(0, d1 // G1_CLASSES_PER_STEP, g1_body, 0)
    for j in range(n_pair):
        ya_ref[:, cols(0, j)] = y_ref[j].astype(_BF16)


def _stage3_kernel(h1_ref, u2_ref, ya_ref, yb_ref, wgate_ref, bgate_ref, wa_ref, wb_ref, wo_ref,
                   g2_ref, wg_ref, wu_ref, wd_ref, gf_ref, out_ref):
    d_model = h1_ref.shape[1]
    subs = [slice(i * SUB_TILE, (i + 1) * SUB_TILE) for i in range(h1_ref.shape[0] // SUB_TILE)]
    pre = [(_dot(u2_ref[sl, :], wgate_ref[...]), _dot(ya_ref[sl, :], wa_ref[...]), _dot(yb_ref[sl, :], wb_ref[...]))
           for sl in subs]
    merged = []
    for gl, pa, pb in pre:
        gates = jax.nn.sigmoid(gl + bgate_ref[...])
        merged.append((gates[:, :d_model] * pa + gates[:, d_model:] * pb).astype(_BF16))
    h2s = [h1_ref[sl, :] + _dot(mg, wo_ref[...]) for sl, mg in zip(subs, merged)]
    acts = [_gate_up(_rms(h2, g2_ref[...]).astype(_BF16), wg_ref, wu_ref) for h2 in h2s]
    ys = [_dot(a, wd_ref[...]) for a in acts]
    for sl, h2, y in zip(subs, h2s, ys):
        out_ref[sl, :] = _rms(h2 + 0.5 * y, gf_ref[...])


def _t5_bucket_np(dist):
    max_exact = N_BUCKETS // 2
    n = np.maximum(dist, 0)
    nf = np.maximum(n, 1).astype(np.float32)
    large = max_exact + (np.log(nf / max_exact) / math.log(MAX_DISTANCE / max_exact)
                         * (N_BUCKETS - max_exact)).astype(np.int32)
    large = np.minimum(large, N_BUCKETS - 1)
    return np.where(n < max_exact, n, large)


def _bias_kernel(table_ref, idx_ref, out_ref):
    h = pl.program_id(0)
    idx = idx_ref[0]
    acc = jnp.full(idx.shape, MASKED, _F32)
    for b in range(N_BUCKETS):
        acc = jnp.where(idx == b, table_ref[b, h] * LOG2E, acc)
    out_ref[0, 0] = acc
    prev_key = lax.broadcasted_iota(jnp.int32, idx.shape, 0) < BLOCK
    out_ref[1, 0] = jnp.where(prev_key, MASKED, acc)


def _band_bias(rel_bias):
    ki = np.arange(2 * BLOCK)[:, None]
    qi = np.arange(BLOCK)[None, :]
    dist = qi + BLOCK - ki
    variants = [(d, w // d) for w, d in DIL_GROUPS] + [(1, B_WINDOW - 1)]
    idx = jnp.asarray(np.stack([np.where((dist >= 0) & (dist <= max_steps), _t5_bucket_np(dist * stride), -1)
                                for stride, max_steps in variants]).astype(np.int32))
    n_heads = rel_bias.shape[1]
    last = len(variants) - 1
    return pl.pallas_call(
        _bias_kernel,
        grid=(n_heads,),
        in_specs=[pl.BlockSpec(memory_space=pltpu.SMEM),
                  pl.BlockSpec((1, 2 * BLOCK, BLOCK),
                               lambda h: (jnp.minimum(h // A_HEADS_PER_GROUP, last), 0, 0))],
        out_specs=pl.BlockSpec((2, 1, 2 * BLOCK, BLOCK), lambda h: (0, h, 0, 0)),
        out_shape=jax.ShapeDtypeStruct((2, n_heads, 2 * BLOCK, BLOCK), _F32),
        name="band_bias",
    )(rel_bias.astype(_F32), idx)


def _const_spec(shape):
    nd = len(shape)
    return pl.BlockSpec(shape, lambda *_: (0,) * nd, pipeline_mode=pl.Buffered(1))


def _layer(x, ffn1_norm, ffn1_w_gate, ffn1_w_up, ffn1_w_down, mix_norm, w_in, b_in,
           w_branch_a, w_branch_b, w_out, sinks, rel_bias,
           ffn2_norm, ffn2_w_gate, ffn2_w_up, ffn2_w_down, final_norm):
    batch, seq, d_model = x.shape
    tokens = batch * seq
    tm = TOKEN_TILE
    tiles_per_seq = seq // tm
    n_tiles = tokens // tm
    d_ff = ffn1_w_gate.shape[1]
    a_w = A_HEADS * HEAD_DIM
    scale = HEAD_DIM ** -0.5 * LOG2E

    aq, ak, av = w_in[:, :a_w], w_in[:, a_w:2 * a_w], w_in[:, 2 * a_w:3 * a_w]
    baq, bak, bav = b_in[:a_w], b_in[a_w:2 * a_w], b_in[2 * a_w:3 * a_w]
    off = 3 * a_w
    bq_w, bq_b = w_in[:, off:off + B_OUT], b_in[off:off + B_OUT]
    bkv_w, bkv_b = w_in[:, off + B_OUT:off + B_OUT + 2 * B_KV_W], b_in[off + B_OUT:off + B_OUT + 2 * B_KV_W]
    gate_w, gate_b = w_in[:, off + B_W:], b_in[off + B_W:]
    perm = np.concatenate([np.r_[j * HEAD_DIM:(j + 1) * HEAD_DIM,
                                 (j + 4) * HEAD_DIM:(j + 5) * HEAD_DIM] for j in range(B_Q_HEADS // 2)])
    w_cols, b_cols = [], []
    for gi in range(len(DIL_GROUPS)):
        sl = slice(gi * A_OUT, (gi + 1) * A_OUT)
        w_cols += [aq[:, sl] * scale, ak[:, sl], av[:, sl]]
        b_cols += [baq[sl] * scale, bak[sl], bav[sl]]
    w_cols += [bq_w[:, perm] * scale, bkv_w]
    b_cols += [bq_b[perm] * scale, bkv_b]
    wqkv = jnp.concatenate(w_cols, axis=1).astype(_BF16)
    bqkv = jnp.concatenate(b_cols)[None, :].astype(_F32)
    wb_perm = w_branch_b[perm, :].astype(_BF16)

    row = lambda v: v[None, :].astype(_F32)
    x2 = x.reshape(tokens, d_model)

    bias = _band_bias(rel_bias)
    bias_nat = jnp.concatenate([bias[:, :A_HEADS_PER_GROUP], bias[:, A_HEADS:]], axis=1)
    sink_rows = jnp.repeat((sinks.astype(_F32) * LOG2E).reshape(B_KV_HEADS, B_Q_HEADS // B_KV_HEADS), BLOCK,
                           axis=1)[:, None, :]

    cur = lambda i: jnp.minimum(i, n_tiles - 1)
    prv = lambda i: jnp.maximum(i - 1, 0)
    d1, d2 = DIL_GROUPS[1][1], DIL_GROUPS[2][1]
    n_sub = tm // SUB_TILE
    h1, u2, qkv1, qkv2, yb, o0, l0 = pl.pallas_call(
        functools.partial(_stage1_kernel, tiles_per_seq=tiles_per_seq),
        grid=(n_tiles + 1,),
        in_specs=[pl.BlockSpec((tm, d_model), lambda i: (cur(i), 0)), _const_spec((1, d_model)),
                  _const_spec((d_model, d_ff)), _const_spec((d_model, d_ff)), _const_spec((d_ff, d_model)),
                  _const_spec((1, d_model)), _const_spec((d_model, 3 * GROUP_W + B_W)),
                  _const_spec((1, 3 * GROUP_W + B_W)),
                  _const_spec(bias_nat.shape), _const_spec(sink_rows.shape)],
        out_specs=[pl.BlockSpec((tm, d_model), lambda i: (cur(i), 0)),
                   pl.BlockSpec((tm, d_model), lambda i: (cur(i), 0)),
                   pl.BlockSpec((1, d1, tm // d1, GROUP_W),
                                lambda i: (cur(i) // tiles_per_seq, 0, cur(i) % tiles_per_seq, 0)),
                   pl.BlockSpec((1, d2, tm // d2, GROUP_W),
                                lambda i: (cur(i) // tiles_per_seq, 0, cur(i) % tiles_per_seq, 0)),
                   pl.BlockSpec((tm, B_OUT), lambda i: (prv(i), 0)),
                   pl.BlockSpec((A_OUT // LANES, tm, LANES), lambda i: (0, prv(i), 0)),
                   pl.BlockSpec((A_OUT // LANES, tm, LANES), lambda i: (0, prv(i), 0))],
        out_shape=[jax.ShapeDtypeStruct((tokens, d_model), _F32),
                   jax.ShapeDtypeStruct((tokens, d_model), _BF16),
                   jax.ShapeDtypeStruct((batch, d1, seq // d1, GROUP_W), _BF16),
                   jax.ShapeDtypeStruct((batch, d2, seq // d2, GROUP_W), _BF16),
                   jax.ShapeDtypeStruct((tokens, B_OUT), _BF16),
                   jax.ShapeDtypeStruct((A_OUT // LANES, tokens, LANES), _F32),
                   jax.ShapeDtypeStruct((A_OUT // LANES, tokens, LANES), _F32)],
        scratch_shapes=[pltpu.VMEM((n_sub, 2 * GROUP_W // LANES, SUB_TILE, LANES), _F32),
                        pltpu.VMEM((2, tm, GROUP_W), _BF16), pltpu.VMEM((2, tm, B_W), _BF16),
                        pltpu.VMEM((2, BLOCK, GROUP_W), _BF16), pltpu.VMEM((2, BLOCK, B_W), _BF16),
                        pltpu.VMEM((N_SLOTS, 2 * BLOCK, 2 * BLOCK), _F32),
                        pltpu.VMEM((N_SLOTS, 2 * BLOCK, 2 * BLOCK), _BF16)],
        compiler_params=pltpu.CompilerParams(dimension_semantics=("arbitrary",), vmem_limit_bytes=VMEM_LIMIT),
        name="stage1_ffn_qkv_attn",
    )(x2, row(ffn1_norm), ffn1_w_gate.astype(_BF16), ffn1_w_up.astype(_BF16), ffn1_w_down.astype(_BF16),
      row(mix_norm), wqkv, bqkv, bias_nat, sink_rows)

    params = pltpu.CompilerParams(dimension_semantics=("parallel",), vmem_limit_bytes=VMEM_LIMIT)

    n_heads = bias.shape[1]
    ya = pl.pallas_call(
        _stage2_kernel,
        grid=(batch,),
        in_specs=[pl.BlockSpec((1, d1, seq // d1, GROUP_W), lambda b: (b, 0, 0, 0)),
                  pl.BlockSpec((1, d2, seq // d2, GROUP_W), lambda b: (b, 0, 0, 0)),
                  pl.BlockSpec((A_OUT // LANES, seq, LANES), lambda b: (0, b, 0)),
                  pl.BlockSpec((A_OUT // LANES, seq, LANES), lambda b: (0, b, 0)),
                  pl.BlockSpec((None, n_heads, 2 * BLOCK, BLOCK), lambda b: (0, 0, 0, 0),
                               pipeline_mode=pl.Buffered(1))],
        out_specs=pl.BlockSpec((seq, A_OUT), lambda b: (b, 0)),
        out_shape=jax.ShapeDtypeStruct((tokens, A_OUT), _BF16),
        scratch_shapes=[pltpu.VMEM((A_OUT // LANES, seq, LANES), _F32) for _ in range(3)]
        + [pltpu.VMEM((N_SLOTS, 2 * BLOCK, 2 * BLOCK), _F32), pltpu.VMEM((N_SLOTS, 2 * BLOCK, 2 * BLOCK), _BF16)],
        compiler_params=params,
        name="stage2_dilated_attention",
    )(qkv1, qkv2, o0, l0, bias)

    tile = lambda w: pl.BlockSpec((tm, w), lambda i: (i, 0))
    out = pl.pallas_call(
        _stage3_kernel,
        grid=(n_tiles,),
        in_specs=[tile(d_model), tile(d_model), tile(A_OUT), tile(B_OUT),
                  _const_spec((d_model, 2 * d_model)), _const_spec((1, 2 * d_model)),
                  _const_spec((A_OUT, d_model)), _const_spec((B_OUT, d_model)), _const_spec((d_model, d_model)),
                  _const_spec((1, d_model)),
                  _const_spec((d_model, d_ff)), _const_spec((d_model, d_ff)), _const_spec((d_ff, d_model)),
                  _const_spec((1, d_model))],
        out_specs=tile(d_model),
        out_shape=jax.ShapeDtypeStruct((tokens, d_model), _F32),
        compiler_params=params,
        name="stage3_merge_ffn",
    )(h1, u2, ya, yb, gate_w.astype(_BF16), row(gate_b), w_branch_a.astype(_BF16), wb_perm,
      w_out.astype(_BF16), row(ffn2_norm), ffn2_w_gate.astype(_BF16), ffn2_w_up.astype(_BF16),
      ffn2_w_down.astype(_BF16), row(final_norm))
    return out.reshape(batch, seq, d_model)


def kernel(x, ffn1_norm, ffn1_w_gate, ffn1_w_up, ffn1_w_down, mix_norm, w_in, b_in, w_branch_a, w_branch_b,
           w_out, sinks, rel_bias, ffn2_norm, ffn2_w_gate, ffn2_w_up, ffn2_w_down, final_norm):
    assert ffn1_norm.shape[0] == 1, "single-layer kernel"
    return _layer(x, ffn1_norm[0], ffn1_w_gate[0], ffn1_w_up[0], ffn1_w_down[0], mix_norm[0], w_in[0], b_in[0],
                  w_branch_a[0], w_branch_b[0], w_out[0], sinks[0], rel_bias,
                  ffn2_norm[0], ffn2_w_gate[0], ffn2_w_up[0], ffn2_w_down[0], final_norm)
```

```python
import functools
import math

import numpy as np
import jax
import jax.numpy as jnp
from jax import lax
from jax.experimental import pallas as pl
from jax.experimental.pallas import tpu as pltpu

HEAD_DIM = 64
DIL_GROUPS = ((128, 1), (512, 4), (2048, 16))
A_HEADS_PER_GROUP = 4
A_HEADS = A_HEADS_PER_GROUP * len(DIL_GROUPS)
A_OUT = A_HEADS_PER_GROUP * HEAD_DIM
B_Q_HEADS = 8
B_KV_HEADS = 2
B_WINDOW = 128
B_OUT = B_Q_HEADS * HEAD_DIM
B_KV_W = B_KV_HEADS * HEAD_DIM
BLOCK = 128
N_BUCKETS = 32
MAX_DISTANCE = 2048
EPS = 1e-6
GROUP_W = 3 * A_OUT
B_W = B_OUT + 2 * B_KV_W
LANES = 128
SUBLANES = 8
MXU_COLS = 256
FF_CHUNKS = (768, 768, 768, 512)
STAGE3_FF_CHUNKS = (256,) * 11
LOG2E = math.log2(math.e)
MASKED = -1e30

TOKEN_TILE = 512
STAGE3_TILE = 512
SUB_TILE = 256
VMEM_LIMIT = 60 * 1024 * 1024

_BF16 = jnp.bfloat16
_F32 = jnp.float32


def _rms(x, g):
    return (x * lax.rsqrt(jnp.mean(x * x, axis=-1, keepdims=True) + EPS)) * g


def _dot(a, b):
    return jnp.dot(a, b, preferred_element_type=_F32)


def _gate_up(xb, wg_ref, wu_ref, gap=None, chunks=FF_CHUNKS):
    acts = []
    c0 = 0
    for w in chunks:
        if gap is not None:
            gap()
        g = _dot(xb, wg_ref[:, c0:c0 + w])
        u = _dot(xb, wu_ref[:, c0:c0 + w])
        c0 += w
        acts.append(((g * jax.nn.sigmoid(g)) * u).astype(_BF16))
    return jnp.concatenate(acts, axis=1)


def _dot_chunks(a, w_ref, chunks, gap):
    outs = []
    c0 = 0
    for w in chunks:
        gap()
        outs.append(_dot(a, w_ref[:, c0:c0 + w]))
        c0 += w
    return jnp.concatenate(outs, axis=1)


KEY_CHUNK = 32
LOOKAHEAD = 4
G2_CLASSES_PER_STEP = 8
G1_CLASSES_PER_STEP = 2
N_SLOTS = 8
assert N_SLOTS > LOOKAHEAD and all(w % MXU_COLS == 0 for w in FF_CHUNKS)
D_CHUNKS = (512, 512)
UNIT_SCHEDULE = (((2, 0),) + ((1, 1),) * 7 + ((2, 2),) * 4
                 + ((2, 1), (2, 1), (2, 1), (1, 1), (0, 1), (0, 2), (0, 2), (0, 1)))
Z_ORDER = (1, 2, 0, 3)


def _scores(scr, q2a, q2b, half_a, half_b, kc, bias_a, bias_b, sink_row=None):
    s_ref, _, slot = scr
    kn = kc.shape[0]
    lane = lax.broadcasted_iota(jnp.int32, q2a.shape, 1)
    keep = lambda q2, half: jnp.where((lane < HEAD_DIM) if half == 0 else (lane >= HEAD_DIM), q2, jnp.zeros_like(q2))
    qm = jnp.concatenate([keep(q2a, half_a), keep(q2b, half_b)], axis=0)
    s = lax.dot_general(kc, qm, (((1,), (1,)), ((), ())), preferred_element_type=_F32)
    s = s + jnp.concatenate([bias_a, bias_b], axis=1)
    s_ref[slot, :kn, :] = s
    m = jnp.max(s, axis=0, keepdims=True)
    if sink_row is not None:
        m = jnp.maximum(m, sink_row)
    return m


def _softmax_pv(scr, m, vc, sink_row=None):
    s_ref, p_ref, slot = scr
    kn = vc.shape[0]
    acc = jnp.zeros((SUBLANES, m.shape[1]), _F32)
    for c in range(kn // KEY_CHUNK):
        rows = slice(c * KEY_CHUNK, (c + 1) * KEY_CHUNK)
        p = jnp.exp2(s_ref[slot, rows, :] - m)
        for t in range(KEY_CHUNK // SUBLANES):
            acc = acc + p[t * SUBLANES:(t + 1) * SUBLANES]
        p_ref[slot, rows, :] = p.astype(_BF16)
    l = jnp.sum(acc, axis=0, keepdims=True)
    if sink_row is not None:
        l = l + jnp.exp2(sink_row - m)
    o_t = lax.dot_general(vc, p_ref[slot, :kn, :], (((0,), (0,)), ((), ())), preferred_element_type=_F32)
    return o_t, l


def _pair_finish(o_t, m, l):
    lse = (m + jnp.log2(l)) * (1.0 / LOG2E)
    on = jnp.concatenate([o_t[:HEAD_DIM, :BLOCK] / l[:, :BLOCK], o_t[HEAD_DIM:, BLOCK:] / l[:, BLOCK:]], axis=0)
    lb = jnp.concatenate([jnp.broadcast_to(lse[:, :BLOCK], (HEAD_DIM, BLOCK)),
                          jnp.broadcast_to(lse[:, BLOCK:], (HEAD_DIM, BLOCK))], axis=0)
    return on.T, lb.T


def _run_units(units):
    state = {}
    for i in range(len(units) + LOOKAHEAD):
        if i < len(units):
            state[i] = units[i][0]()
        k = i - LOOKAHEAD
        if k >= 0:
            units[k][1](state.pop(k))


class _UnitQueue:
    def __init__(self, units, schedule):
        self.units, self.next, self.pending, self.schedule = units, 0, [], iter(schedule)

    def _step(self, n_start, n_finish):
        for _ in range(min(n_finish, len(self.pending))):
            k, m = self.pending.pop(0)
            self.units[k][1](m)
        for k in range(self.next, min(self.next + n_start, len(self.units))):
            self.pending.append((k, self.units[k][0]()))
        self.next = min(self.next + n_start, len(self.units))
        assert len(self.pending) <= N_SLOTS

    def gap(self):
        self._step(*next(self.schedule))

    def flush(self):
        assert self.next == len(self.units) and next(self.schedule, None) is None
        self._step(0, len(self.pending))


def _stage1_kernel(x_ref, g1_ref, wg_ref, wu_ref, wd_ref, gmix_ref, wqkv_ref, bqkv_ref, bias_ref, sink_ref,
                   h1_ref, u2_ref, qkv1_ref, qkv2_ref, yb_ref, o0_ref, l0_ref,
                   zs_ref, cur0_ref, curb_ref, prev0_ref, prevb_ref, s_ref, p_ref, *, tiles_per_seq):
    i = pl.program_id(0)
    par = lax.rem(i, 2)
    opar = 1 - par
    opens_seq = lax.rem(i + tiles_per_seq - 1, tiles_per_seq) == 0
    tm = x_ref.shape[0]
    n_pair = A_OUT // LANES
    n_qp = B_OUT // LANES

    @pl.when(i == 0)
    def _():
        cur0_ref[...] = jnp.zeros_like(cur0_ref)
        curb_ref[...] = jnp.zeros_like(curb_ref)
        prev0_ref[...] = jnp.zeros_like(prev0_ref)
        prevb_ref[...] = jnp.zeros_like(prevb_ref)

    def cols(base, j):
        return slice(base + j * LANES, base + (j + 1) * LANES)

    def nat_units(c, slot0):
        rows = slice(c * BLOCK, (c + 1) * BLOCK)
        variant = jnp.where(opens_seq, 1, 0) if c == 0 else 0

        def keys(cur_ref, prev_ref, cs):
            if c == 0:
                return jnp.concatenate([prev_ref[opar, :, cs], cur_ref[opar, rows, cs]], axis=0)
            return cur_ref[opar, (c - 1) * BLOCK:(c + 1) * BLOCK, cs]

        units = []
        for j in range(n_pair):
            scr = (s_ref, p_ref, (slot0 + j) % N_SLOTS)

            def first_pass(j=j, scr=scr):
                q2 = cur0_ref[opar, rows, cols(0, j)]
                return _scores(scr, q2, q2, 0, 1, keys(cur0_ref, prev0_ref, cols(A_OUT, j)),
                               bias_ref[variant, 2 * j], bias_ref[variant, 2 * j + 1])

            def second_pass(m, j=j, scr=scr):
                o_t, l = _softmax_pv(scr, m, keys(cur0_ref, prev0_ref, cols(2 * A_OUT, j)))
                o0_ref[j, rows, :], l0_ref[j, rows, :] = _pair_finish(o_t, m, l)

            units.append((first_pass, second_pass))
        done = {}
        for half in range(2):
            for u in range(n_qp // 2):
                scr = (s_ref, p_ref, (slot0 + n_pair + half * (n_qp // 2) + u) % N_SLOTS)

                def first_pass(half=half, u=u, scr=scr):
                    ha = A_HEADS_PER_GROUP + n_qp * half + 2 * u
                    sink_row = sink_ref[half, :, 2 * u * BLOCK:(2 * u + 2) * BLOCK]
                    return _scores(scr, curb_ref[opar, rows, cols(0, 2 * u)], curb_ref[opar, rows, cols(0, 2 * u + 1)],
                                   half, half, keys(curb_ref, prevb_ref, slice(B_OUT, B_OUT + B_KV_W)),
                                   bias_ref[variant, ha], bias_ref[variant, ha + 1], sink_row=sink_row)

                def second_pass(m, half=half, u=u, scr=scr):
                    sink_row = sink_ref[half, :, 2 * u * BLOCK:(2 * u + 2) * BLOCK]
                    vc = keys(curb_ref, prevb_ref, slice(B_OUT + B_KV_W, B_OUT + 2 * B_KV_W))
                    o_t, l = _softmax_pv(scr, m, vc, sink_row=sink_row)
                    done[half, u] = o_t[half * HEAD_DIM:(half + 1) * HEAD_DIM] / l
                    if half == 1:
                        for jj in range(2):
                            j = 2 * u + jj
                            tile_t = jnp.concatenate([done[0, u][:, jj * BLOCK:(jj + 1) * BLOCK],
                                                      done[1, u][:, jj * BLOCK:(jj + 1) * BLOCK]], axis=0)
                            yb_ref[rows, cols(0, j)] = tile_t.T.astype(_BF16)

                units.append((first_pass, second_pass))
        return units

    units = []
    for c in range(tm // BLOCK):
        units += nat_units(c, c * (n_pair + n_qp))
    att = _UnitQueue(units, UNIT_SCHEDULE)

    subs = [slice(k * SUB_TILE, (k + 1) * SUB_TILE) for k in range(tm // SUB_TILE)]
    xs = [x_ref[sl, :] for sl in subs]
    xbs = [_rms(x, g1_ref[...]).astype(_BF16) for x in xs]
    acts = [_gate_up(xb, wg_ref, wu_ref, att.gap) for xb in xbs]
    ys = [_dot_chunks(a, wd_ref, D_CHUNKS, att.gap) for a in acts]
    u2s = []
    for sl, x, y in zip(subs, xs, ys):
        h1 = x + 0.5 * y
        h1_ref[sl, :] = h1
        u2 = _rms(h1, gmix_ref[...]).astype(_BF16)
        u2_ref[sl, :] = u2
        u2s.append(u2)
    prev0_ref[par] = cur0_ref[opar, tm - BLOCK:, :]
    prevb_ref[par] = curb_ref[opar, tm - BLOCK:, :]
    n_slab = GROUP_W // LANES
    for k, (sl, u2) in enumerate(zip(subs, u2s)):
        for q in Z_ORDER:
            cs = slice(q * GROUP_W, (q + 1) * GROUP_W)
            att.gap()
            z = _dot(u2, wqkv_ref[:, cs]) + bqkv_ref[:, cs]
            if q == 0:
                cur0_ref[par, sl, :] = z.astype(_BF16)
            elif q == len(DIL_GROUPS):
                curb_ref[par, sl, :] = z.astype(_BF16)
            else:
                out_ref = (qkv1_ref, qkv2_ref)[q - 1]
                d = DIL_GROUPS[q][1]
                n = SUB_TILE // d
                for j in range(n_slab):
                    zs_ref[k, (q - 1) * n_slab + j] = z[:, j * LANES:(j + 1) * LANES]
                for j in range(n_slab):
                    for r in range(d):
                        rows = zs_ref[k, (q - 1) * n_slab + j, pl.ds(r, n, stride=d), :]
                        out_ref[0, r, k * n:(k + 1) * n, j * LANES:(j + 1) * LANES] = rows.astype(_BF16)
    att.flush()


def _stage2_kernel(qkv1_ref, qkv2_ref, o0_ref, l0_ref, bias_ref,
                   ya_ref, y_ref, o2_ref, l2_ref, s_ref, p_ref):
    n_pair = A_OUT // LANES
    seq = ya_ref.shape[0]

    def cols(base, j):
        return slice(base + j * LANES, base + (j + 1) * LANES)

    def bias_pair(h, first):
        if first:
            return bias_ref[h, BLOCK:, :], bias_ref[h + 1, BLOCK:, :]
        return bias_ref[h], bias_ref[h + 1]

    def pair_unit(slot, load, h, first, finish):
        scr = (s_ref, p_ref, slot)

        def first_pass():
            q2 = load(0, True)
            return _scores(scr, q2, q2, 0, 1, load(A_OUT, False), *bias_pair(h, first))

        def second_pass(m):
            o_t, l = _softmax_pv(scr, m, load(2 * A_OUT, False))
            finish(*_pair_finish(o_t, m, l))

        return first_pass, second_pass

    d1, d2 = DIL_GROUPS[1][1], DIL_GROUPS[2][1]
    assert G2_CLASSES_PER_STEP % d1 == 0

    def g2_body(it, carry):
        units = []
        for c in range(G2_CLASSES_PER_STEP):
            r = it * G2_CLASSES_PER_STEP + c
            for j in range(n_pair):
                def load(base, is_q, r=r, j=j):
                    return qkv2_ref[0, r, :, cols(base, j)]

                def finish(out, lse2, c=c, j=j):
                    start = (c % d1) * (seq // d1) + it * (G2_CLASSES_PER_STEP // d1) + c // d1
                    o2_ref[j, pl.ds(start, BLOCK, stride=d1), :] = out
                    l2_ref[j, pl.ds(start, BLOCK, stride=d1), :] = lse2

                units.append(pair_unit(len(units) % N_SLOTS, load, 2 * A_HEADS_PER_GROUP + 2 * j, True, finish))
        _run_units(units)
        return carry

    lax.fori_loop(0, d2 // G2_CLASSES_PER_STEP, g2_body, 0)

    nb1 = seq // d1 // BLOCK

    def g1_body(it, carry):
        units = []
        for c in range(G1_CLASSES_PER_STEP):
            r = it * G1_CLASSES_PER_STEP + c
            for b in range(nb1):
                first = b == 0
                row0 = b * BLOCK
                k0 = 0 if first else (b - 1) * BLOCK
                kn = BLOCK if first else 2 * BLOCK
                for j in range(n_pair):
                    def load(base, is_q, r=r, j=j, row0=row0, k0=k0, kn=kn):
                        return qkv1_ref[0, r, pl.ds(row0, BLOCK) if is_q else pl.ds(k0, kn), cols(base, j)]

                    def finish(o1, l1, r=r, j=j, row0=row0):
                        tok = pl.ds(row0 * d1 + r, BLOCK, stride=d1)
                        o0, l0 = o0_ref[j, tok, :], l0_ref[j, tok, :]
                        own = pl.ds(pl.multiple_of(r * (seq // d1) + row0, BLOCK), BLOCK)
                        o2, l2 = o2_ref[j, own, :], l2_ref[j, own, :]
                        top = jnp.maximum(jnp.maximum(l0, l1), l2)
                        w0, w1, w2 = jnp.exp(l0 - top), jnp.exp(l1 - top), jnp.exp(l2 - top)
                        y_ref[j, tok, :] = (w0 * o0 + w1 * o1 + w2 * o2) / (w0 + w1 + w2)

                    units.append(pair_unit(len(units) % N_SLOTS, load, A_HEADS_PER_GROUP + 2 * j, first, finish))
        _run_units(units)
        return carry

    lax.fori_loop(0, d1 // G1_CLASSES_PER_STEP, g1_body, 0)
    for j in range(n_pair):
        ya_ref[:, cols(0, j)] = y_ref[j].astype(_BF16)


def _stage3_kernel(h1_ref, u2_ref, ya_ref, yb_ref, wgate_ref, bgate_ref, wa_ref, wb_ref, wo_ref,
                   g2_ref, wg_ref, wu_ref, wd_ref, gf_ref, out_ref):
    d_model = h1_ref.shape[1]
    n_sub = h1_ref.shape[0] // SUB_TILE
    for first in range(0, n_sub, 2):
        subs = [slice(i * SUB_TILE, (i + 1) * SUB_TILE) for i in range(first, min(first + 2, n_sub))]
        pre = [(_dot(u2_ref[sl, :], wgate_ref[...]), _dot(ya_ref[sl, :], wa_ref[...]),
                _dot(yb_ref[sl, :], wb_ref[...])) for sl in subs]
        merged = []
        for gl, pa, pb in pre:
            gates = jax.nn.sigmoid(gl + bgate_ref[...])
            merged.append((gates[:, :d_model] * pa + gates[:, d_model:] * pb).astype(_BF16))
        h2s = [h1_ref[sl, :] + _dot(mg, wo_ref[...]) for sl, mg in zip(subs, merged)]
        acts = [_gate_up(_rms(h2, g2_ref[...]).astype(_BF16), wg_ref, wu_ref, chunks=STAGE3_FF_CHUNKS)
                for h2 in h2s]
        ys = [_dot(a, wd_ref[...]) for a in acts]
        for sl, h2, y in zip(subs, h2s, ys):
            out_ref[sl, :] = _rms(h2 + 0.5 * y, gf_ref[...])


def _t5_bucket_np(dist):
    max_exact = N_BUCKETS // 2
    n = np.maximum(dist, 0)
    nf = np.maximum(n, 1).astype(np.float32)
    large = max_exact + (np.log(nf / max_exact) / math.log(MAX_DISTANCE / max_exact)
                         * (N_BUCKETS - max_exact)).astype(np.int32)
    large = np.minimum(large, N_BUCKETS - 1)
    return np.where(n < max_exact, n, large)


def _bias_kernel(table_ref, idx_ref, out_ref):
    h = pl.program_id(0)
    idx = idx_ref[0]
    acc = jnp.full(idx.shape, MASKED, _F32)
    for b in range(N_BUCKETS):
        acc = jnp.where(idx == b, table_ref[b, h] * LOG2E, acc)
    out_ref[0, 0] = acc
    prev_key = lax.broadcasted_iota(jnp.int32, idx.shape, 0) < BLOCK
    out_ref[1, 0] = jnp.where(prev_key, MASKED, acc)


def _band_bias(rel_bias):
    ki = np.arange(2 * BLOCK)[:, None]
    qi = np.arange(BLOCK)[None, :]
    dist = qi + BLOCK - ki
    variants = [(d, w // d) for w, d in DIL_GROUPS] + [(1, B_WINDOW - 1)]
    idx = jnp.asarray(np.stack([np.where((dist >= 0) & (dist <= max_steps), _t5_bucket_np(dist * stride), -1)
                                for stride, max_steps in variants]).astype(np.int32))
    n_heads = rel_bias.shape[1]
    last = len(variants) - 1
    return pl.pallas_call(
        _bias_kernel,
        grid=(n_heads,),
        in_specs=[pl.BlockSpec(memory_space=pltpu.SMEM),
                  pl.BlockSpec((1, 2 * BLOCK, BLOCK),
                               lambda h: (jnp.minimum(h // A_HEADS_PER_GROUP, last), 0, 0))],
        out_specs=pl.BlockSpec((2, 1, 2 * BLOCK, BLOCK), lambda h: (0, h, 0, 0)),
        out_shape=jax.ShapeDtypeStruct((2, n_heads, 2 * BLOCK, BLOCK), _F32),
        name="band_bias",
    )(rel_bias.astype(_F32), idx)


def _const_spec(shape):
    nd = len(shape)
    return pl.BlockSpec(shape, lambda *_: (0,) * nd, pipeline_mode=pl.Buffered(1))


def _layer(x, ffn1_norm, ffn1_w_gate, ffn1_w_up, ffn1_w_down, mix_norm, w_in, b_in,
           w_branch_a, w_branch_b, w_out, sinks, rel_bias,
           ffn2_norm, ffn2_w_gate, ffn2_w_up, ffn2_w_down, final_norm):
    batch, seq, d_model = x.shape
    tokens = batch * seq
    tm = TOKEN_TILE
    tiles_per_seq = seq // tm
    n_tiles = tokens // tm
    d_ff = ffn1_w_gate.shape[1]
    a_w = A_HEADS * HEAD_DIM
    scale = HEAD_DIM ** -0.5 * LOG2E

    aq, ak, av = w_in[:, :a_w], w_in[:, a_w:2 * a_w], w_in[:, 2 * a_w:3 * a_w]
    baq, bak, bav = b_in[:a_w], b_in[a_w:2 * a_w], b_in[2 * a_w:3 * a_w]
    off = 3 * a_w
    bq_w, bq_b = w_in[:, off:off + B_OUT], b_in[off:off + B_OUT]
    bkv_w, bkv_b = w_in[:, off + B_OUT:off + B_OUT + 2 * B_KV_W], b_in[off + B_OUT:off + B_OUT + 2 * B_KV_W]
    gate_w, gate_b = w_in[:, off + B_W:], b_in[off + B_W:]
    perm = np.concatenate([np.r_[j * HEAD_DIM:(j + 1) * HEAD_DIM,
                                 (j + 4) * HEAD_DIM:(j + 5) * HEAD_DIM] for j in range(B_Q_HEADS // 2)])
    w_cols, b_cols = [], []
    for gi in range(len(DIL_GROUPS)):
        sl = slice(gi * A_OUT, (gi + 1) * A_OUT)
        w_cols += [aq[:, sl] * scale, ak[:, sl], av[:, sl]]
        b_cols += [baq[sl] * scale, bak[sl], bav[sl]]
    w_cols += [bq_w[:, perm] * scale, bkv_w]
    b_cols += [bq_b[perm] * scale, bkv_b]
    wqkv = jnp.concatenate(w_cols, axis=1).astype(_BF16)
    bqkv = jnp.concatenate(b_cols)[None, :].astype(_F32)
    wb_perm = w_branch_b[perm, :].astype(_BF16)

    row = lambda v: v[None, :].astype(_F32)
    x2 = x.reshape(tokens, d_model)

    bias = _band_bias(rel_bias)
    bias_nat = jnp.concatenate([bias[:, :A_HEADS_PER_GROUP], bias[:, A_HEADS:]], axis=1)
    sink_rows = jnp.repeat((sinks.astype(_F32) * LOG2E).reshape(B_KV_HEADS, B_Q_HEADS // B_KV_HEADS), BLOCK,
                           axis=1)[:, None, :]

    cur = lambda i: jnp.minimum(i, n_tiles - 1)
    prv = lambda i: jnp.maximum(i - 1, 0)
    d1, d2 = DIL_GROUPS[1][1], DIL_GROUPS[2][1]
    n_sub = tm // SUB_TILE
    h1, u2, qkv1, qkv2, yb, o0, l0 = pl.pallas_call(
        functools.partial(_stage1_kernel, tiles_per_seq=tiles_per_seq),
        grid=(n_tiles + 1,),
        in_specs=[pl.BlockSpec((tm, d_model), lambda i: (cur(i), 0)), _const_spec((1, d_model)),
                  _const_spec((d_model, d_ff)), _const_spec((d_model, d_ff)), _const_spec((d_ff, d_model)),
                  _const_spec((1, d_model)), _const_spec((d_model, 3 * GROUP_W + B_W)),
                  _const_spec((1, 3 * GROUP_W + B_W)),
                  _const_spec(bias_nat.shape), _const_spec(sink_rows.shape)],
        out_specs=[pl.BlockSpec((tm, d_model), lambda i: (cur(i), 0)),
                   pl.BlockSpec((tm, d_model), lambda i: (cur(i), 0)),
                   pl.BlockSpec((1, d1, tm // d1, GROUP_W),
                                lambda i: (cur(i) // tiles_per_seq, 0, cur(i) % tiles_per_seq, 0)),
                   pl.BlockSpec((1, d2, tm // d2, GROUP_W),
                                lambda i: (cur(i) // tiles_per_seq, 0, cur(i) % tiles_per_seq, 0)),
                   pl.BlockSpec((tm, B_OUT), lambda i: (prv(i), 0)),
                   pl.BlockSpec((A_OUT // LANES, tm, LANES), lambda i: (0, prv(i), 0)),
                   pl.BlockSpec((A_OUT // LANES, tm, LANES), lambda i: (0, prv(i), 0))],
        out_shape=[jax.ShapeDtypeStruct((tokens, d_model), _F32),
                   jax.ShapeDtypeStruct((tokens, d_model), _BF16),
                   jax.ShapeDtypeStruct((batch, d1, seq // d1, GROUP_W), _BF16),
                   jax.ShapeDtypeStruct((batch, d2, seq // d2, GROUP_W), _BF16),
                   jax.ShapeDtypeStruct((tokens, B_OUT), _BF16),
                   jax.ShapeDtypeStruct((A_OUT // LANES, tokens, LANES), _F32),
                   jax.ShapeDtypeStruct((A_OUT // LANES, tokens, LANES), _F32)],
        scratch_shapes=[pltpu.VMEM((n_sub, 2 * GROUP_W // LANES, SUB_TILE, LANES), _F32),
                        pltpu.VMEM((2, tm, GROUP_W), _BF16), pltpu.VMEM((2, tm, B_W), _BF16),
                        pltpu.VMEM((2, BLOCK, GROUP_W), _BF16), pltpu.VMEM((2, BLOCK, B_W), _BF16),
                        pltpu.VMEM((N_SLOTS, 2 * BLOCK, 2 * BLOCK), _F32),
                        pltpu.VMEM((N_SLOTS, 2 * BLOCK, 2 * BLOCK), _BF16)],
        compiler_params=pltpu.CompilerParams(dimension_semantics=("arbitrary",), vmem_limit_bytes=VMEM_LIMIT),
        name="stage1_ffn_qkv_attn",
    )(x2, row(ffn1_norm), ffn1_w_gate.astype(_BF16), ffn1_w_up.astype(_BF16), ffn1_w_down.astype(_BF16),
      row(mix_norm), wqkv, bqkv, bias_nat, sink_rows)

    params = pltpu.CompilerParams(dimension_semantics=("parallel",), vmem_limit_bytes=VMEM_LIMIT)

    n_heads = bias.shape[1]
    ya = pl.pallas_call(
        _stage2_kernel,
        grid=(batch,),
        in_specs=[pl.BlockSpec((1, d1, seq // d1, GROUP_W), lambda b: (b, 0, 0, 0)),
                  pl.BlockSpec((1, d2, seq // d2, GROUP_W), lambda b: (b, 0, 0, 0)),
                  pl.BlockSpec((A_OUT // LANES, seq, LANES), lambda b: (0, b, 0)),
                  pl.BlockSpec((A_OUT // LANES, seq, LANES), lambda b: (0, b, 0)),
                  pl.BlockSpec((None, n_heads, 2 * BLOCK, BLOCK), lambda b: (0, 0, 0, 0),
                               pipeline_mode=pl.Buffered(1))],
        out_specs=pl.BlockSpec((seq, A_OUT), lambda b: (b, 0)),
        out_shape=jax.ShapeDtypeStruct((tokens, A_OUT), _BF16),
        scratch_shapes=[pltpu.VMEM((A_OUT // LANES, seq, LANES), _F32) for _ in range(3)]
        + [pltpu.VMEM((N_SLOTS, 2 * BLOCK, 2 * BLOCK), _F32), pltpu.VMEM((N_SLOTS, 2 * BLOCK, 2 * BLOCK), _BF16)],
        compiler_params=params,
        name="stage2_dilated_attention",
    )(qkv1, qkv2, o0, l0, bias)

    tile = lambda w: pl.BlockSpec((STAGE3_TILE, w), lambda i: (i, 0))
    out = pl.pallas_call(
        _stage3_kernel,
        grid=(tokens // STAGE3_TILE,),
        in_specs=[tile(d_model), tile(d_model), tile(A_OUT), tile(B_OUT),
                  _const_spec((d_model, 2 * d_model)), _const_spec((1, 2 * d_model)),
                  _const_spec((A_OUT, d_model)), _const_spec((B_OUT, d_model)), _const_spec((d_model, d_model)),
                  _const_spec((1, d_model)),
                  _const_spec((d_model, d_ff)), _const_spec((d_model, d_ff)), _const_spec((d_ff, d_model)),
                  _const_spec((1, d_model))],
        out_specs=tile(d_model),
        out_shape=jax.ShapeDtypeStruct((tokens, d_model), _F32),
        compiler_params=params,
        name="stage3_merge_ffn",
    )(h1, u2, ya, yb, gate_w.astype(_BF16), row(gate_b), w_branch_a.astype(_BF16), wb_perm,
      w_out.astype(_BF16), row(ffn2_norm), ffn2_w_gate.astype(_BF16), ffn2_w_up.astype(_BF16),
      ffn2_w_down.astype(_BF16), row(final_norm))
    return out.reshape(batch, seq, d_model)


def kernel(x, ffn1_norm, ffn1_w_gate, ffn1_w_up, ffn1_w_down, mix_norm, w_in, b_in, w_branch_a, w_branch_b,
           w_out, sinks, rel_bias, ffn2_norm, ffn2_w_gate, ffn2_w_up, ffn2_w_down, final_norm):
    assert ffn1_norm.shape[0] == 1, "single-layer kernel"
    return _layer(x, ffn1_norm[0], ffn1_w_gate[0], ffn1_w_up[0], ffn1_w_down[0], mix_norm[0], w_in[0], b_in[0],
                  w_branch_a[0], w_branch_b[0], w_out[0], sinks[0], rel_bias,
                  ffn2_norm[0], ffn2_w_gate[0], ffn2_w_up[0], ffn2_w_down[0], final_norm)
```

```python
import functools
import math

import numpy as np
import jax
import jax.numpy as jnp
from jax import lax
from jax.experimental import pallas as pl
from jax.experimental.pallas import tpu as pltpu

HEAD_DIM = 64
DIL_GROUPS = ((128, 1), (512, 4), (2048, 16))
A_HEADS_PER_GROUP = 4
A_HEADS = A_HEADS_PER_GROUP * len(DIL_GROUPS)
A_OUT = A_HEADS_PER_GROUP * HEAD_DIM
B_Q_HEADS = 8
B_KV_HEADS = 2
B_WINDOW = 128
B_OUT = B_Q_HEADS * HEAD_DIM
B_KV_W = B_KV_HEADS * HEAD_DIM
BLOCK = 128
N_BUCKETS = 32
MAX_DISTANCE = 2048
EPS = 1e-6
GROUP_W = 3 * A_OUT
B_W = B_OUT + 2 * B_KV_W
LANES = 128
SUBLANES = 8
MXU_COLS = 256
FF_CHUNKS = (768, 768, 768, 512)
STAGE3_FF_CHUNKS = (256,) * 11
LOG2E = math.log2(math.e)
MASKED = -1e30

TOKEN_TILE = 512
STAGE3_TILE = 512
SUB_TILE = 256
VMEM_LIMIT = 60 * 1024 * 1024

_BF16 = jnp.bfloat16
_F32 = jnp.float32


def _rms(x, g):
    return (x * lax.rsqrt(jnp.mean(x * x, axis=-1, keepdims=True) + EPS)) * g


def _dot(a, b):
    return jnp.dot(a, b, preferred_element_type=_F32)


def _gate_up(xb, wg_ref, wu_ref, gap=None, chunks=FF_CHUNKS):
    acts = []
    c0 = 0
    for w in chunks:
        if gap is not None:
            gap()
        g = _dot(xb, wg_ref[:, c0:c0 + w])
        u = _dot(xb, wu_ref[:, c0:c0 + w])
        c0 += w
        acts.append(((g * jax.nn.sigmoid(g)) * u).astype(_BF16))
    return jnp.concatenate(acts, axis=1)


def _dot_chunks(a, w_ref, chunks, gap):
    outs = []
    c0 = 0
    for w in chunks:
        gap()
        outs.append(_dot(a, w_ref[:, c0:c0 + w]))
        c0 += w
    return jnp.concatenate(outs, axis=1)


KEY_CHUNK = 32
LOOKAHEAD = 4
G2_CLASSES_PER_STEP = 8
G1_CLASSES_PER_STEP = 2
N_SLOTS = 8
assert N_SLOTS > LOOKAHEAD and all(w % MXU_COLS == 0 for w in FF_CHUNKS)
D_CHUNKS = (512, 512)
UNIT_SCHEDULE = (((1, 0),) + ((1, 1),) * 23 + ((0, 1),) + ((0, 0),) * 9)
Z_ORDER = (1, 2, 0, 3)


def _scores(scr, q2a, q2b, half_a, half_b, kc, bias_a, bias_b, sink_row=None):
    s_ref, _, slot = scr
    kn = kc.shape[0]
    lane = lax.broadcasted_iota(jnp.int32, q2a.shape, 1)
    keep = lambda q2, half: jnp.where((lane < HEAD_DIM) if half == 0 else (lane >= HEAD_DIM), q2, jnp.zeros_like(q2))
    qm = jnp.concatenate([keep(q2a, half_a), keep(q2b, half_b)], axis=0)
    s = lax.dot_general(kc, qm, (((1,), (1,)), ((), ())), preferred_element_type=_F32)
    s = s + jnp.concatenate([bias_a, bias_b], axis=1)
    s_ref[slot, :kn, :] = s
    m = jnp.max(s, axis=0, keepdims=True)
    if sink_row is not None:
        m = jnp.maximum(m, sink_row)
    return m


def _softmax_pv(scr, m, vc, sink_row=None):
    s_ref, p_ref, slot = scr
    kn = vc.shape[0]
    acc = jnp.zeros((SUBLANES, m.shape[1]), _F32)
    for c in range(kn // KEY_CHUNK):
        rows = slice(c * KEY_CHUNK, (c + 1) * KEY_CHUNK)
        p = jnp.exp2(s_ref[slot, rows, :] - m)
        for t in range(KEY_CHUNK // SUBLANES):
            acc = acc + p[t * SUBLANES:(t + 1) * SUBLANES]
        p_ref[slot, rows, :] = p.astype(_BF16)
    l = jnp.sum(acc, axis=0, keepdims=True)
    if sink_row is not None:
        l = l + jnp.exp2(sink_row - m)
    o_t = lax.dot_general(vc, p_ref[slot, :kn, :], (((0,), (0,)), ((), ())), preferred_element_type=_F32)
    return o_t, l


def _pair_finish(o_t, m, l):
    lse = (m + jnp.log2(l)) * (1.0 / LOG2E)
    on = jnp.concatenate([o_t[:HEAD_DIM, :BLOCK] / l[:, :BLOCK], o_t[HEAD_DIM:, BLOCK:] / l[:, BLOCK:]], axis=0)
    lb = jnp.concatenate([jnp.broadcast_to(lse[:, :BLOCK], (HEAD_DIM, BLOCK)),
                          jnp.broadcast_to(lse[:, BLOCK:], (HEAD_DIM, BLOCK))], axis=0)
    return on.T, lb.T


def _run_units(units):
    state = {}
    for i in range(len(units) + LOOKAHEAD):
        if i < len(units):
            state[i] = units[i][0]()
        k = i - LOOKAHEAD
        if k >= 0:
            units[k][1](state.pop(k))


class _UnitQueue:
    def __init__(self, units, schedule):
        self.units, self.next, self.pending, self.schedule = units, 0, [], iter(schedule)

    def _step(self, n_start, n_finish):
        for _ in range(min(n_finish, len(self.pending))):
            k, m = self.pending.pop(0)
            self.units[k][1](m)
        for k in range(self.next, min(self.next + n_start, len(self.units))):
            self.pending.append((k, self.units[k][0]()))
        self.next = min(self.next + n_start, len(self.units))
        assert len(self.pending) <= N_SLOTS

    def gap(self):
        self._step(*next(self.schedule))

    def flush(self):
        assert self.next == len(self.units) and next(self.schedule, None) is None
        self._step(0, len(self.pending))


def _stage1_kernel(x_ref, g1_ref, wg_ref, wu_ref, wd_ref, gmix_ref, wqkv_ref, bqkv_ref, bias_ref, sink_ref,
                   h1_ref, u2_ref, qkv1_ref, qkv2_ref, yb_ref, o0_ref, l0_ref,
                   zs_ref, cur0_ref, curb_ref, prev0_ref, prevb_ref, s_ref, p_ref, *, tiles_per_seq):
    i = pl.program_id(0)
    par = lax.rem(i, 2)
    opar = 1 - par
    opens_seq = lax.rem(i + tiles_per_seq - 1, tiles_per_seq) == 0
    tm = x_ref.shape[0]
    n_pair = A_OUT // LANES
    n_qp = B_OUT // LANES

    @pl.when(i == 0)
    def _():
        cur0_ref[...] = jnp.zeros_like(cur0_ref)
        curb_ref[...] = jnp.zeros_like(curb_ref)
        prev0_ref[...] = jnp.zeros_like(prev0_ref)
        prevb_ref[...] = jnp.zeros_like(prevb_ref)

    def cols(base, j):
        return slice(base + j * LANES, base + (j + 1) * LANES)

    def nat_units(c, slot0):
        rows = slice(c * BLOCK, (c + 1) * BLOCK)
        variant = jnp.where(opens_seq, 1, 0) if c == 0 else 0

        def keys(cur_ref, prev_ref, cs):
            if c == 0:
                return jnp.concatenate([prev_ref[opar, :, cs], cur_ref[opar, rows, cs]], axis=0)
            return cur_ref[opar, (c - 1) * BLOCK:(c + 1) * BLOCK, cs]

        units = []
        for j in range(n_pair):
            scr = (s_ref, p_ref, (slot0 + j) % N_SLOTS)

            def first_pass(j=j, scr=scr):
                q2 = cur0_ref[opar, rows, cols(0, j)]
                return _scores(scr, q2, q2, 0, 1, keys(cur0_ref, prev0_ref, cols(A_OUT, j)),
                               bias_ref[variant, 2 * j], bias_ref[variant, 2 * j + 1])

            def second_pass(m, j=j, scr=scr):
                o_t, l = _softmax_pv(scr, m, keys(cur0_ref, prev0_ref, cols(2 * A_OUT, j)))
                o0_ref[j, rows, :], l0_ref[j, rows, :] = _pair_finish(o_t, m, l)

            units.append((first_pass, second_pass))
        done = {}
        for half in range(2):
            for u in range(n_qp // 2):
                scr = (s_ref, p_ref, (slot0 + n_pair + half * (n_qp // 2) + u) % N_SLOTS)

                def first_pass(half=half, u=u, scr=scr):
                    ha = A_HEADS_PER_GROUP + n_qp * half + 2 * u
                    sink_row = sink_ref[half, :, 2 * u * BLOCK:(2 * u + 2) * BLOCK]
                    return _scores(scr, curb_ref[opar, rows, cols(0, 2 * u)], curb_ref[opar, rows, cols(0, 2 * u + 1)],
                                   half, half, keys(curb_ref, prevb_ref, slice(B_OUT, B_OUT + B_KV_W)),
                                   bias_ref[variant, ha], bias_ref[variant, ha + 1], sink_row=sink_row)

                def second_pass(m, half=half, u=u, scr=scr):
                    sink_row = sink_ref[half, :, 2 * u * BLOCK:(2 * u + 2) * BLOCK]
                    vc = keys(curb_ref, prevb_ref, slice(B_OUT + B_KV_W, B_OUT + 2 * B_KV_W))
                    o_t, l = _softmax_pv(scr, m, vc, sink_row=sink_row)
                    done[half, u] = o_t[half * HEAD_DIM:(half + 1) * HEAD_DIM] / l
                    if half == 1:
                        for jj in range(2):
                            j = 2 * u + jj
                            tile_t = jnp.concatenate([done[0, u][:, jj * BLOCK:(jj + 1) * BLOCK],
                                                      done[1, u][:, jj * BLOCK:(jj + 1) * BLOCK]], axis=0)
                            yb_ref[rows, cols(0, j)] = tile_t.T.astype(_BF16)

                units.append((first_pass, second_pass))
        return units

    units = []
    for c in range(tm // BLOCK):
        units += nat_units(c, c * (n_pair + n_qp))
    att = _UnitQueue(units, UNIT_SCHEDULE)

    subs = [slice(k * SUB_TILE, (k + 1) * SUB_TILE) for k in range(tm // SUB_TILE)]
    xs = [x_ref[sl, :] for sl in subs]
    xbs = [_rms(x, g1_ref[...]).astype(_BF16) for x in xs]
    acts = [_gate_up(xb, wg_ref, wu_ref, att.gap, chunks=STAGE3_FF_CHUNKS) for xb in xbs]
    ys = [_dot_chunks(a, wd_ref, D_CHUNKS, att.gap) for a in acts]
    u2s = []
    for sl, x, y in zip(subs, xs, ys):
        h1 = x + 0.5 * y
        h1_ref[sl, :] = h1
        u2 = _rms(h1, gmix_ref[...]).astype(_BF16)
        u2_ref[sl, :] = u2
        u2s.append(u2)
    prev0_ref[par] = cur0_ref[opar, tm - BLOCK:, :]
    prevb_ref[par] = curb_ref[opar, tm - BLOCK:, :]
    n_slab = GROUP_W // LANES
    for k, (sl, u2) in enumerate(zip(subs, u2s)):
        for q in Z_ORDER:
            cs = slice(q * GROUP_W, (q + 1) * GROUP_W)
            att.gap()
            z = _dot(u2, wqkv_ref[:, cs]) + bqkv_ref[:, cs]
            if q == 0:
                cur0_ref[par, sl, :] = z.astype(_BF16)
            elif q == len(DIL_GROUPS):
                curb_ref[par, sl, :] = z.astype(_BF16)
            else:
                out_ref = (qkv1_ref, qkv2_ref)[q - 1]
                d = DIL_GROUPS[q][1]
                n = SUB_TILE // d
                for j in range(n_slab):
                    zs_ref[k, (q - 1) * n_slab + j] = z[:, j * LANES:(j + 1) * LANES]
                for j in range(n_slab):
                    for r in range(d):
                        rows = zs_ref[k, (q - 1) * n_slab + j, pl.ds(r, n, stride=d), :]
                        out_ref[0, r, k * n:(k + 1) * n, j * LANES:(j + 1) * LANES] = rows.astype(_BF16)
    att.flush()


def _stage2_kernel(qkv1_ref, qkv2_ref, o0_ref, l0_ref, bias_ref,
                   ya_ref, y_ref, o2_ref, l2_ref, s_ref, p_ref):
    n_pair = A_OUT // LANES
    seq = ya_ref.shape[0]

    def cols(base, j):
        return slice(base + j * LANES, base + (j + 1) * LANES)

    def bias_pair(h, first):
        if first:
            return bias_ref[h, BLOCK:, :], bias_ref[h + 1, BLOCK:, :]
        return bias_ref[h], bias_ref[h + 1]

    def pair_unit(slot, load, h, first, finish):
        scr = (s_ref, p_ref, slot)

        def first_pass():
            q2 = load(0, True)
            return _scores(scr, q2, q2, 0, 1, load(A_OUT, False), *bias_pair(h, first))

        def second_pass(m):
            o_t, l = _softmax_pv(scr, m, load(2 * A_OUT, False))
            finish(*_pair_finish(o_t, m, l))

        return first_pass, second_pass

    d1, d2 = DIL_GROUPS[1][1], DIL_GROUPS[2][1]
    assert G2_CLASSES_PER_STEP % d1 == 0

    def g2_body(it, carry):
        units = []
        for c in range(G2_CLASSES_PER_STEP):
            r = it * G2_CLASSES_PER_STEP + c
            for j in range(n_pair):
                def load(base, is_q, r=r, j=j):
                    return qkv2_ref[0, r, :, cols(base, j)]

                def finish(out, lse2, c=c, j=j):
                    start = (c % d1) * (seq // d1) + it * (G2_CLASSES_PER_STEP // d1) + c // d1
                    o2_ref[j, pl.ds(start, BLOCK, stride=d1), :] = out
                    l2_ref[j, pl.ds(start, BLOCK, stride=d1), :] = lse2

                units.append(pair_unit(len(units) % N_SLOTS, load, 2 * A_HEADS_PER_GROUP + 2 * j, True, finish))
        _run_units(units)
        return carry

    lax.fori_loop(0, d2 // G2_CLASSES_PER_STEP, g2_body, 0)

    nb1 = seq // d1 // BLOCK

    def g1_body(it, carry):
        units = []
        for c in range(G1_CLASSES_PER_STEP):
            r = it * G1_CLASSES_PER_STEP + c
            for b in range(nb1):
                first = b == 0
                row0 = b * BLOCK
                k0 = 0 if first else (b - 1) * BLOCK
                kn = BLOCK if first else 2 * BLOCK
                for j in range(n_pair):
                    def load(base, is_q, r=r, j=j, row0=row0, k0=k0, kn=kn):
                        return qkv1_ref[0, r, pl.ds(row0, BLOCK) if is_q else pl.ds(k0, kn), cols(base, j)]

                    def finish(o1, l1, r=r, j=j, row0=row0):
                        tok = pl.ds(row0 * d1 + r, BLOCK, stride=d1)
                        o0, l0 = o0_ref[j, tok, :], l0_ref[j, tok, :]
                        own = pl.ds(pl.multiple_of(r * (seq // d1) + row0, BLOCK), BLOCK)
                        o2, l2 = o2_ref[j, own, :], l2_ref[j, own, :]
                        top = jnp.maximum(jnp.maximum(l0, l1), l2)
                        w0, w1, w2 = jnp.exp(l0 - top), jnp.exp(l1 - top), jnp.exp(l2 - top)
                        y_ref[j, tok, :] = (w0 * o0 + w1 * o1 + w2 * o2) / (w0 + w1 + w2)

                    units.append(pair_unit(len(units) % N_SLOTS, load, A_HEADS_PER_GROUP + 2 * j, first, finish))
        _run_units(units)
        return carry

    lax.fori_loop(0, d1 // G1_CLASSES_PER_STEP, g1_body, 0)
    for j in range(n_pair):
        ya_ref[:, cols(0, j)] = y_ref[j].astype(_BF16)


def _stage3_kernel(h1_ref, u2_ref, ya_ref, yb_ref, wgate_ref, bgate_ref, wa_ref, wb_ref, wo_ref,
                   g2_ref, wg_ref, wu_ref, wd_ref, gf_ref, out_ref):
    d_model = h1_ref.shape[1]
    n_sub = h1_ref.shape[0] // SUB_TILE
    for first in range(0, n_sub, 2):
        subs = [slice(i * SUB_TILE, (i + 1) * SUB_TILE) for i in range(first, min(first + 2, n_sub))]
        pre = [(_dot(u2_ref[sl, :], wgate_ref[...]), _dot(ya_ref[sl, :], wa_ref[...]),
                _dot(yb_ref[sl, :], wb_ref[...])) for sl in subs]
        merged = []
        for gl, pa, pb in pre:
            gates = jax.nn.sigmoid(gl + bgate_ref[...])
            merged.append((gates[:, :d_model] * pa + gates[:, d_model:] * pb).astype(_BF16))
        h2s = [h1_ref[sl, :] + _dot(mg, wo_ref[...]) for sl, mg in zip(subs, merged)]
        acts = [_gate_up(_rms(h2, g2_ref[...]).astype(_BF16), wg_ref, wu_ref, chunks=STAGE3_FF_CHUNKS)
                for h2 in h2s]
        ys = [_dot(a, wd_ref[...]) for a in acts]
        for sl, h2, y in zip(subs, h2s, ys):
            out_ref[sl, :] = _rms(h2 + 0.5 * y, gf_ref[...])


def _t5_bucket_np(dist):
    max_exact = N_BUCKETS // 2
    n = np.maximum(dist, 0)
    nf = np.maximum(n, 1).astype(np.float32)
    large = max_exact + (np.log(nf / max_exact) / math.log(MAX_DISTANCE / max_exact)
                         * (N_BUCKETS - max_exact)).astype(np.int32)
    large = np.minimum(large, N_BUCKETS - 1)
    return np.where(n < max_exact, n, large)


def _bias_kernel(table_ref, idx_ref, out_ref):
    h = pl.program_id(0)
    idx = idx_ref[0]
    acc = jnp.full(idx.shape, MASKED, _F32)
    for b in range(N_BUCKETS):
        acc = jnp.where(idx == b, table_ref[b, h] * LOG2E, acc)
    out_ref[0, 0] = acc
    prev_key = lax.broadcasted_iota(jnp.int32, idx.shape, 0) < BLOCK
    out_ref[1, 0] = jnp.where(prev_key, MASKED, acc)


def _band_bias(rel_bias):
    ki = np.arange(2 * BLOCK)[:, None]
    qi = np.arange(BLOCK)[None, :]
    dist = qi + BLOCK - ki
    variants = [(d, w // d) for w, d in DIL_GROUPS] + [(1, B_WINDOW - 1)]
    idx = jnp.asarray(np.stack([np.where((dist >= 0) & (dist <= max_steps), _t5_bucket_np(dist * stride), -1)
                                for stride, max_steps in variants]).astype(np.int32))
    n_heads = rel_bias.shape[1]
    last = len(variants) - 1
    return pl.pallas_call(
        _bias_kernel,
        grid=(n_heads,),
        in_specs=[pl.BlockSpec(memory_space=pltpu.SMEM),
                  pl.BlockSpec((1, 2 * BLOCK, BLOCK),
                               lambda h: (jnp.minimum(h // A_HEADS_PER_GROUP, last), 0, 0))],
        out_specs=pl.BlockSpec((2, 1, 2 * BLOCK, BLOCK), lambda h: (0, h, 0, 0)),
        out_shape=jax.ShapeDtypeStruct((2, n_heads, 2 * BLOCK, BLOCK), _F32),
        name="band_bias",
    )(rel_bias.astype(_F32), idx)


def _const_spec(shape):
    nd = len(shape)
    return pl.BlockSpec(shape, lambda *_: (0,) * nd, pipeline_mode=pl.Buffered(1))


def _layer(x, ffn1_norm, ffn1_w_gate, ffn1_w_up, ffn1_w_down, mix_norm, w_in, b_in,
           w_branch_a, w_branch_b, w_out, sinks, rel_bias,
           ffn2_norm, ffn2_w_gate, ffn2_w_up, ffn2_w_down, final_norm):
    batch, seq, d_model = x.shape
    tokens = batch * seq
    tm = TOKEN_TILE
    tiles_per_seq = seq // tm
    n_tiles = tokens // tm
    d_ff = ffn1_w_gate.shape[1]
    a_w = A_HEADS * HEAD_DIM
    scale = HEAD_DIM ** -0.5 * LOG2E

    aq, ak, av = w_in[:, :a_w], w_in[:, a_w:2 * a_w], w_in[:, 2 * a_w:3 * a_w]
    baq, bak, bav = b_in[:a_w], b_in[a_w:2 * a_w], b_in[2 * a_w:3 * a_w]
    off = 3 * a_w
    bq_w, bq_b = w_in[:, off:off + B_OUT], b_in[off:off + B_OUT]
    bkv_w, bkv_b = w_in[:, off + B_OUT:off + B_OUT + 2 * B_KV_W], b_in[off + B_OUT:off + B_OUT + 2 * B_KV_W]
    gate_w, gate_b = w_in[:, off + B_W:], b_in[off + B_W:]
    perm = np.concatenate([np.r_[j * HEAD_DIM:(j + 1) * HEAD_DIM,
                                 (j + 4) * HEAD_DIM:(j + 5) * HEAD_DIM] for j in range(B_Q_HEADS // 2)])
    w_cols, b_cols = [], []
    for gi in range(len(DIL_GROUPS)):
        sl = slice(gi * A_OUT, (gi + 1) * A_OUT)
        w_cols += [aq[:, sl] * scale, ak[:, sl], av[:, sl]]
        b_cols += [baq[sl] * scale, bak[sl], bav[sl]]
    w_cols += [bq_w[:, perm] * scale, bkv_w]
    b_cols += [bq_b[perm] * scale, bkv_b]
    wqkv = jnp.concatenate(w_cols, axis=1).astype(_BF16)
    bqkv = jnp.concatenate(b_cols)[None, :].astype(_F32)
    wb_perm = w_branch_b[perm, :].astype(_BF16)

    row = lambda v: v[None, :].astype(_F32)
    x2 = x.reshape(tokens, d_model)

    bias = _band_bias(rel_bias)
    bias_nat = jnp.concatenate([bias[:, :A_HEADS_PER_GROUP], bias[:, A_HEADS:]], axis=1)
    sink_rows = jnp.repeat((sinks.astype(_F32) * LOG2E).reshape(B_KV_HEADS, B_Q_HEADS // B_KV_HEADS), BLOCK,
                           axis=1)[:, None, :]

    cur = lambda i: jnp.minimum(i, n_tiles - 1)
    prv = lambda i: jnp.maximum(i - 1, 0)
    d1, d2 = DIL_GROUPS[1][1], DIL_GROUPS[2][1]
    n_sub = tm // SUB_TILE
    h1, u2, qkv1, qkv2, yb, o0, l0 = pl.pallas_call(
        functools.partial(_stage1_kernel, tiles_per_seq=tiles_per_seq),
        grid=(n_tiles + 1,),
        in_specs=[pl.BlockSpec((tm, d_model), lambda i: (cur(i), 0)), _const_spec((1, d_model)),
                  _const_spec((d_model, d_ff)), _const_spec((d_model, d_ff)), _const_spec((d_ff, d_model)),
                  _const_spec((1, d_model)), _const_spec((d_model, 3 * GROUP_W + B_W)),
                  _const_spec((1, 3 * GROUP_W + B_W)),
                  _const_spec(bias_nat.shape), _const_spec(sink_rows.shape)],
        out_specs=[pl.BlockSpec((tm, d_model), lambda i: (cur(i), 0)),
                   pl.BlockSpec((tm, d_model), lambda i: (cur(i), 0)),
                   pl.BlockSpec((1, d1, tm // d1, GROUP_W),
                                lambda i: (cur(i) // tiles_per_seq, 0, cur(i) % tiles_per_seq, 0)),
                   pl.BlockSpec((1, d2, tm // d2, GROUP_W),
                                lambda i: (cur(i) // tiles_per_seq, 0, cur(i) % tiles_per_seq, 0)),
                   pl.BlockSpec((tm, B_OUT), lambda i: (prv(i), 0)),
                   pl.BlockSpec((A_OUT // LANES, tm, LANES), lambda i: (0, prv(i), 0)),
                   pl.BlockSpec((A_OUT // LANES, tm, LANES), lambda i: (0, prv(i), 0))],
        out_shape=[jax.ShapeDtypeStruct((tokens, d_model), _F32),
                   jax.ShapeDtypeStruct((tokens, d_model), _BF16),
                   jax.ShapeDtypeStruct((batch, d1, seq // d1, GROUP_W), _BF16),
                   jax.ShapeDtypeStruct((batch, d2, seq // d2, GROUP_W), _BF16),
                   jax.ShapeDtypeStruct((tokens, B_OUT), _BF16),
                   jax.ShapeDtypeStruct((A_OUT // LANES, tokens, LANES), _F32),
                   jax.ShapeDtypeStruct((A_OUT // LANES, tokens, LANES), _F32)],
        scratch_shapes=[pltpu.VMEM((n_sub, 2 * GROUP_W // LANES, SUB_TILE, LANES), _F32),
                        pltpu.VMEM((2, tm, GROUP_W), _BF16), pltpu.VMEM((2, tm, B_W), _BF16),
                        pltpu.VMEM((2, BLOCK, GROUP_W), _BF16), pltpu.VMEM((2, BLOCK, B_W), _BF16),
                        pltpu.VMEM((N_SLOTS, 2 * BLOCK, 2 * BLOCK), _F32),
                        pltpu.VMEM((N_SLOTS, 2 * BLOCK, 2 * BLOCK), _BF16)],
        compiler_params=pltpu.CompilerParams(dimension_semantics=("arbitrary",), vmem_limit_bytes=VMEM_LIMIT),
        name="stage1_ffn_qkv_attn",
    )(x2, row(ffn1_norm), ffn1_w_gate.astype(_BF16), ffn1_w_up.astype(_BF16), ffn1_w_down.astype(_BF16),
      row(mix_norm), wqkv, bqkv, bias_nat, sink_rows)

    params = pltpu.CompilerParams(dimension_semantics=("parallel",), vmem_limit_bytes=VMEM_LIMIT)

    n_heads = bias.shape[1]
    ya = pl.pallas_call(
        _stage2_kernel,
        grid=(batch,),
        in_specs=[pl.BlockSpec((1, d1, seq // d1, GROUP_W), lambda b: (b, 0, 0, 0)),
                  pl.BlockSpec((1, d2, seq // d2, GROUP_W), lambda b: (b, 0, 0, 0)),
                  pl.BlockSpec((A_OUT // LANES, seq, LANES), lambda b: (0, b, 0)),
                  pl.BlockSpec((A_OUT // LANES, seq, LANES), lambda b: (0, b, 0)),
                  pl.BlockSpec((None, n_heads, 2 * BLOCK, BLOCK), lambda b: (0, 0, 0, 0),
                               pipeline_mode=pl.Buffered(1))],
        out_specs=pl.BlockSpec((seq, A_OUT), lambda b: (b, 0)),
        out_shape=jax.ShapeDtypeStruct((tokens, A_OUT), _BF16),
        scratch_shapes=[pltpu.VMEM((A_OUT // LANES, seq, LANES), _F32) for _ in range(3)]
        + [pltpu.VMEM((N_SLOTS, 2 * BLOCK, 2 * BLOCK), _F32), pltpu.VMEM((N_SLOTS, 2 * BLOCK, 2 * BLOCK), _BF16)],
        compiler_params=params,
        name="stage2_dilated_attention",
    )(qkv1, qkv2, o0, l0, bias)

    tile = lambda w: pl.BlockSpec((STAGE3_TILE, w), lambda i: (i, 0))
    out = pl.pallas_call(
        _stage3_kernel,
        grid=(tokens // STAGE3_TILE,),
        in_specs=[tile(d_model), tile(d_model), tile(A_OUT), tile(B_OUT),
                  _const_spec((d_model, 2 * d_model)), _const_spec((1, 2 * d_model)),
                  _const_spec((A_OUT, d_model)), _const_spec((B_OUT, d_model)), _const_spec((d_model, d_model)),
                  _const_spec((1, d_model)),
                  _const_spec((d_model, d_ff)), _const_spec((d_model, d_ff)), _const_spec((d_ff, d_model)),
                  _const_spec((1, d_model))],
        out_specs=tile(d_model),
        out_shape=jax.ShapeDtypeStruct((tokens, d_model), _F32),
        compiler_params=params,
        name="stage3_merge_ffn",
    )(h1, u2, ya, yb, gate_w.astype(_BF16), row(gate_b), w_branch_a.astype(_BF16), wb_perm,
      w_out.astype(_BF16), row(ffn2_norm), ffn2_w_gate.astype(_BF16), ffn2_w_up.astype(_BF16),
      ffn2_w_down.astype(_BF16), row(final_norm))
    return out.reshape(batch, seq, d_model)


def kernel(x, ffn1_norm, ffn1_w_gate, ffn1_w_up, ffn1_w_down, mix_norm, w_in, b_in, w_branch_a, w_branch_b,
           w_out, sinks, rel_bias, ffn2_norm, ffn2_w_gate, ffn2_w_up, ffn2_w_down, final_norm):
    assert ffn1_norm.shape[0] == 1, "single-layer kernel"
    return _layer(x, ffn1_norm[0], ffn1_w_gate[0], ffn1_w_up[0], ffn1_w_down[0], mix_norm[0], w_in[0], b_in[0],
                  w_branch_a[0], w_branch_b[0], w_out[0], sinks[0], rel_bias,
                  ffn2_norm[0], ffn2_w_gate[0], ffn2_w_up[0], ffn2_w_down[0], final_norm)
```

```python
import functools
import math

import numpy as np
import jax
import jax.numpy as jnp
from jax import lax
from jax.experimental import pallas as pl
from jax.experimental.pallas import tpu as pltpu

HEAD_DIM = 64
DIL_GROUPS = ((128, 1), (512, 4), (2048, 16))
A_HEADS_PER_GROUP = 4
A_HEADS = A_HEADS_PER_GROUP * len(DIL_GROUPS)
A_OUT = A_HEADS_PER_GROUP * HEAD_DIM
B_Q_HEADS = 8
B_KV_HEADS = 2
B_WINDOW = 128
B_OUT = B_Q_HEADS * HEAD_DIM
B_KV_W = B_KV_HEADS * HEAD_DIM
BLOCK = 128
N_BUCKETS = 32
MAX_DISTANCE = 2048
EPS = 1e-6
GROUP_W = 3 * A_OUT
B_W = B_OUT + 2 * B_KV_W
LANES = 128
SUBLANES = 8
MXU_COLS = 256
FF_CHUNK = MXU_COLS
LOG2E = math.log2(math.e)
MASKED = -1e30

TOKEN_TILE = 512
STAGE3_TILE = 512
SUB_TILE = 256
VMEM_LIMIT = 60 * 1024 * 1024

_BF16 = jnp.bfloat16
_F32 = jnp.float32


def _rms(x, g):
    return (x * lax.rsqrt(jnp.mean(x * x, axis=-1, keepdims=True) + EPS)) * g


def _dot(a, b):
    return jnp.dot(a, b, preferred_element_type=_F32)


def _gate_up(xb, wg_ref, wu_ref, gap=None):
    acts = []
    for c0 in range(0, wg_ref.shape[1], FF_CHUNK):
        if gap is not None:
            gap()
        g = _dot(xb, wg_ref[:, c0:c0 + FF_CHUNK])
        u = _dot(xb, wu_ref[:, c0:c0 + FF_CHUNK])
        acts.append(((g * jax.nn.sigmoid(g)) * u).astype(_BF16))
    return jnp.concatenate(acts, axis=1)


def _dot_chunks(a, w_ref, chunks, gap):
    outs = []
    c0 = 0
    for w in chunks:
        gap()
        outs.append(_dot(a, w_ref[:, c0:c0 + w]))
        c0 += w
    return jnp.concatenate(outs, axis=1)


KEY_CHUNK = 32
LOOKAHEAD = 4
G2_CLASSES_PER_STEP = 8
G1_CLASSES_PER_STEP = 2
N_SLOTS = 8
assert N_SLOTS > LOOKAHEAD
D_CHUNKS = (512, 512)
UNIT_SCHEDULE = (((1, 0),) + ((1, 1),) * 23 + ((0, 1),) + ((0, 0),) * 9)
Z_ORDER = (1, 2, 0, 3)


def _scores(scr, q2a, q2b, half_a, half_b, kc, bias_a, bias_b, sink_row=None):
    s_ref, _, slot = scr
    kn = kc.shape[0]
    lane = lax.broadcasted_iota(jnp.int32, q2a.shape, 1)
    keep = lambda q2, half: jnp.where((lane < HEAD_DIM) if half == 0 else (lane >= HEAD_DIM), q2, jnp.zeros_like(q2))
    qm = jnp.concatenate([keep(q2a, half_a), keep(q2b, half_b)], axis=0)
    s = lax.dot_general(kc, qm, (((1,), (1,)), ((), ())), preferred_element_type=_F32)
    s = s + jnp.concatenate([bias_a, bias_b], axis=1)
    s_ref[slot, :kn, :] = s
    m = jnp.max(s, axis=0, keepdims=True)
    if sink_row is not None:
        m = jnp.maximum(m, sink_row)
    return m


def _softmax_pv(scr, m, vc, sink_row=None):
    s_ref, p_ref, slot = scr
    kn = vc.shape[0]
    acc = jnp.zeros((SUBLANES, m.shape[1]), _F32)
    for c in range(kn // KEY_CHUNK):
        rows = slice(c * KEY_CHUNK, (c + 1) * KEY_CHUNK)
        p = jnp.exp2(s_ref[slot, rows, :] - m)
        for t in range(KEY_CHUNK // SUBLANES):
            acc = acc + p[t * SUBLANES:(t + 1) * SUBLANES]
        p_ref[slot, rows, :] = p.astype(_BF16)
    l = jnp.sum(acc, axis=0, keepdims=True)
    if sink_row is not None:
        l = l + jnp.exp2(sink_row - m)
    o_t = lax.dot_general(vc, p_ref[slot, :kn, :], (((0,), (0,)), ((), ())), preferred_element_type=_F32)
    return o_t, l


def _pair_finish(o_t, m, l):
    lse = (m + jnp.log2(l)) * (1.0 / LOG2E)
    on = jnp.concatenate([o_t[:HEAD_DIM, :BLOCK] / l[:, :BLOCK], o_t[HEAD_DIM:, BLOCK:] / l[:, BLOCK:]], axis=0)
    lb = jnp.concatenate([jnp.broadcast_to(lse[:, :BLOCK], (HEAD_DIM, BLOCK)),
                          jnp.broadcast_to(lse[:, BLOCK:], (HEAD_DIM, BLOCK))], axis=0)
    return on.T, lb.T


def _run_units(units):
    state = {}
    for i in range(len(units) + LOOKAHEAD):
        if i < len(units):
            state[i] = units[i][0]()
        k = i - LOOKAHEAD
        if k >= 0:
            units[k][1](state.pop(k))


class _UnitQueue:
    def __init__(self, units, schedule):
        self.units, self.next, self.pending, self.schedule = units, 0, [], iter(schedule)

    def _step(self, n_start, n_finish):
        for _ in range(min(n_finish, len(self.pending))):
            k, m = self.pending.pop(0)
            self.units[k][1](m)
        for k in range(self.next, min(self.next + n_start, len(self.units))):
            self.pending.append((k, self.units[k][0]()))
        self.next = min(self.next + n_start, len(self.units))
        assert len(self.pending) <= N_SLOTS

    def gap(self):
        self._step(*next(self.schedule))

    def flush(self):
        assert self.next == len(self.units) and next(self.schedule, None) is None
        self._step(0, len(self.pending))


def _stage1_kernel(x_ref, g1_ref, wg_ref, wu_ref, wd_ref, gmix_ref, wqkv_ref, bqkv_ref, bias_ref, sink_ref,
                   h1_ref, u2_ref, qkv1_ref, qkv2_ref, yb_ref, o0_ref, l0_ref,
                   zs_ref, cur0_ref, curb_ref, prev0_ref, prevb_ref, s_ref, p_ref, *, tiles_per_seq):
    i = pl.program_id(0)
    par = lax.rem(i, 2)
    opar = 1 - par
    opens_seq = lax.rem(i + tiles_per_seq - 1, tiles_per_seq) == 0
    tm = x_ref.shape[0]
    n_pair = A_OUT // LANES
    n_qp = B_OUT // LANES

    @pl.when(i == 0)
    def _():
        cur0_ref[...] = jnp.zeros_like(cur0_ref)
        curb_ref[...] = jnp.zeros_like(curb_ref)
        prev0_ref[...] = jnp.zeros_like(prev0_ref)
        prevb_ref[...] = jnp.zeros_like(prevb_ref)

    def cols(base, j):
        return slice(base + j * LANES, base + (j + 1) * LANES)

    def nat_units(c, slot0):
        rows = slice(c * BLOCK, (c + 1) * BLOCK)
        variant = jnp.where(opens_seq, 1, 0) if c == 0 else 0

        def keys(cur_ref, prev_ref, cs):
            if c == 0:
                return jnp.concatenate([prev_ref[opar, :, cs], cur_ref[opar, rows, cs]], axis=0)
            return cur_ref[opar, (c - 1) * BLOCK:(c + 1) * BLOCK, cs]

        units = []
        for j in range(n_pair):
            scr = (s_ref, p_ref, (slot0 + j) % N_SLOTS)

            def first_pass(j=j, scr=scr):
                q2 = cur0_ref[opar, rows, cols(0, j)]
                return _scores(scr, q2, q2, 0, 1, keys(cur0_ref, prev0_ref, cols(A_OUT, j)),
                               bias_ref[variant, 2 * j], bias_ref[variant, 2 * j + 1])

            def second_pass(m, j=j, scr=scr):
                o_t, l = _softmax_pv(scr, m, keys(cur0_ref, prev0_ref, cols(2 * A_OUT, j)))
                o0_ref[j, rows, :], l0_ref[j, rows, :] = _pair_finish(o_t, m, l)

            units.append((first_pass, second_pass))
        done = {}
        for half in range(2):
            for u in range(n_qp // 2):
                scr = (s_ref, p_ref, (slot0 + n_pair + half * (n_qp // 2) + u) % N_SLOTS)

                def first_pass(half=half, u=u, scr=scr):
                    ha = A_HEADS_PER_GROUP + n_qp * half + 2 * u
                    sink_row = sink_ref[half, :, 2 * u * BLOCK:(2 * u + 2) * BLOCK]
                    return _scores(scr, curb_ref[opar, rows, cols(0, 2 * u)], curb_ref[opar, rows, cols(0, 2 * u + 1)],
                                   half, half, keys(curb_ref, prevb_ref, slice(B_OUT, B_OUT + B_KV_W)),
                                   bias_ref[variant, ha], bias_ref[variant, ha + 1], sink_row=sink_row)

                def second_pass(m, half=half, u=u, scr=scr):
                    sink_row = sink_ref[half, :, 2 * u * BLOCK:(2 * u + 2) * BLOCK]
                    vc = keys(curb_ref, prevb_ref, slice(B_OUT + B_KV_W, B_OUT + 2 * B_KV_W))
                    o_t, l = _softmax_pv(scr, m, vc, sink_row=sink_row)
                    done[half, u] = o_t[half * HEAD_DIM:(half + 1) * HEAD_DIM] / l
                    if half == 1:
                        for jj in range(2):
                            j = 2 * u + jj
                            tile_t = jnp.concatenate([done[0, u][:, jj * BLOCK:(jj + 1) * BLOCK],
                                                      done[1, u][:, jj * BLOCK:(jj + 1) * BLOCK]], axis=0)
                            yb_ref[rows, cols(0, j)] = tile_t.T.astype(_BF16)

                units.append((first_pass, second_pass))
        return units

    units = []
    for c in range(tm // BLOCK):
        units += nat_units(c, c * (n_pair + n_qp))
    att = _UnitQueue(units, UNIT_SCHEDULE)

    subs = [slice(k * SUB_TILE, (k + 1) * SUB_TILE) for k in range(tm // SUB_TILE)]
    xs = [x_ref[sl, :] for sl in subs]
    xbs = [_rms(x, g1_ref[...]).astype(_BF16) for x in xs]
    acts = [_gate_up(xb, wg_ref, wu_ref, att.gap) for xb in xbs]
    ys = [_dot_chunks(a, wd_ref, D_CHUNKS, att.gap) for a in acts]
    u2s = []
    for sl, x, y in zip(subs, xs, ys):
        h1 = x + 0.5 * y
        h1_ref[sl, :] = h1
        u2 = _rms(h1, gmix_ref[...]).astype(_BF16)
        u2_ref[sl, :] = u2
        u2s.append(u2)
    prev0_ref[par] = cur0_ref[opar, tm - BLOCK:, :]
    prevb_ref[par] = curb_ref[opar, tm - BLOCK:, :]
    n_slab = GROUP_W // LANES
    for k, (sl, u2) in enumerate(zip(subs, u2s)):
        for q in Z_ORDER:
            cs = slice(q * GROUP_W, (q + 1) * GROUP_W)
            att.gap()
            z = _dot(u2, wqkv_ref[:, cs]) + bqkv_ref[:, cs]
            if q == 0:
                cur0_ref[par, sl, :] = z.astype(_BF16)
            elif q == len(DIL_GROUPS):
                curb_ref[par, sl, :] = z.astype(_BF16)
            else:
                out_ref = (qkv1_ref, qkv2_ref)[q - 1]
                d = DIL_GROUPS[q][1]
                n = SUB_TILE // d
                for j in range(n_slab):
                    zs_ref[k, (q - 1) * n_slab + j] = z[:, j * LANES:(j + 1) * LANES]
                for j in range(n_slab):
                    for r in range(d):
                        rows = zs_ref[k, (q - 1) * n_slab + j, pl.ds(r, n, stride=d), :]
                        out_ref[0, r, k * n:(k + 1) * n, j * LANES:(j + 1) * LANES] = rows.astype(_BF16)
    att.flush()


def _stage2_kernel(qkv1_ref, qkv2_ref, o0_ref, l0_ref, bias_ref,
                   ya_ref, y_ref, o2_ref, l2_ref, s_ref, p_ref):
    n_pair = A_OUT // LANES
    seq = ya_ref.shape[0]

    def cols(base, j):
        return slice(base + j * LANES, base + (j + 1) * LANES)

    def bias_pair(h, first):
        if first:
            return bias_ref[h, BLOCK:, :], bias_ref[h + 1, BLOCK:, :]
        return bias_ref[h], bias_ref[h + 1]

    def pair_unit(slot, load, h, first, finish):
        scr = (s_ref, p_ref, slot)

        def first_pass():
            q2 = load(0, True)
            return _scores(scr, q2, q2, 0, 1, load(A_OUT, False), *bias_pair(h, first))

        def second_pass(m):
            o_t, l = _softmax_pv(scr, m, load(2 * A_OUT, False))
            finish(*_pair_finish(o_t, m, l))

        return first_pass, second_pass

    d1, d2 = DIL_GROUPS[1][1], DIL_GROUPS[2][1]
    assert G2_CLASSES_PER_STEP % d1 == 0

    def g2_body(it, carry):
        units = []
        for c in range(G2_CLASSES_PER_STEP):
            r = it * G2_CLASSES_PER_STEP + c
            for j in range(n_pair):
                def load(base, is_q, r=r, j=j):
                    return qkv2_ref[0, r, :, cols(base, j)]

                def finish(out, lse2, c=c, j=j):
                    start = (c % d1) * (seq // d1) + it * (G2_CLASSES_PER_STEP // d1) + c // d1
                    o2_ref[j, pl.ds(start, BLOCK, stride=d1), :] = out
                    l2_ref[j, pl.ds(start, BLOCK, stride=d1), :] = lse2

                units.append(pair_unit(len(units) % N_SLOTS, load, 2 * A_HEADS_PER_GROUP + 2 * j, True, finish))
        _run_units(units)
        return carry

    lax.fori_loop(0, d2 // G2_CLASSES_PER_STEP, g2_body, 0)

    nb1 = seq // d1 // BLOCK

    def g1_body(it, carry):
        units = []
        for c in range(G1_CLASSES_PER_STEP):
            r = it * G1_CLASSES_PER_STEP + c
            for b in range(nb1):
                first = b == 0
                row0 = b * BLOCK
                k0 = 0 if first else (b - 1) * BLOCK
                kn = BLOCK if first else 2 * BLOCK
                for j in range(n_pair):
                    def load(base, is_q, r=r, j=j, row0=row0, k0=k0, kn=kn):
                        return qkv1_ref[0, r, pl.ds(row0, BLOCK) if is_q else pl.ds(k0, kn), cols(base, j)]

                    def finish(o1, l1, r=r, j=j, row0=row0):
                        tok = pl.ds(row0 * d1 + r, BLOCK, stride=d1)
                        o0, l0 = o0_ref[j, tok, :], l0_ref[j, tok, :]
                        own = pl.ds(pl.multiple_of(r * (seq // d1) + row0, BLOCK), BLOCK)
                        o2, l2 = o2_ref[j, own, :], l2_ref[j, own, :]
                        top = jnp.maximum(jnp.maximum(l0, l1), l2)
                        w0, w1, w2 = jnp.exp(l0 - top), jnp.exp(l1 - top), jnp.exp(l2 - top)
                        y_ref[j, tok, :] = (w0 * o0 + w1 * o1 + w2 * o2) / (w0 + w1 + w2)

                    units.append(pair_unit(len(units) % N_SLOTS, load, A_HEADS_PER_GROUP + 2 * j, first, finish))
        _run_units(units)
        return carry

    lax.fori_loop(0, d1 // G1_CLASSES_PER_STEP, g1_body, 0)
    for j in range(n_pair):
        ya_ref[:, cols(0, j)] = y_ref[j].astype(_BF16)


def _stage3_kernel(h1_ref, u2_ref, ya_ref, yb_ref, wgate_ref, bgate_ref, wa_ref, wb_ref, wo_ref,
                   g2_ref, wg_ref, wu_ref, wd_ref, gf_ref, out_ref):
    d_model = h1_ref.shape[1]
    n_sub = h1_ref.shape[0] // SUB_TILE
    for first in range(0, n_sub, 2):
        subs = [slice(i * SUB_TILE, (i + 1) * SUB_TILE) for i in range(first, min(first + 2, n_sub))]
        pre = [(_dot(u2_ref[sl, :], wgate_ref[...]), _dot(ya_ref[sl, :], wa_ref[...]),
                _dot(yb_ref[sl, :], wb_ref[...])) for sl in subs]
        merged = []
        for gl, pa, pb in pre:
            gates = jax.nn.sigmoid(gl + bgate_ref[...])
            merged.append((gates[:, :d_model] * pa + gates[:, d_model:] * pb).astype(_BF16))
        h2s = [h1_ref[sl, :] + _dot(mg, wo_ref[...]) for sl, mg in zip(subs, merged)]
        acts = [_gate_up(_rms(h2, g2_ref[...]).astype(_BF16), wg_ref, wu_ref) for h2 in h2s]
        ys = [_dot(a, wd_ref[...]) for a in acts]
        for sl, h2, y in zip(subs, h2s, ys):
            out_ref[sl, :] = _rms(h2 + 0.5 * y, gf_ref[...])


def _t5_bucket_np(dist):
    max_exact = N_BUCKETS // 2
    n = np.maximum(dist, 0)
    nf = np.maximum(n, 1).astype(np.float32)
    large = max_exact + (np.log(nf / max_exact) / math.log(MAX_DISTANCE / max_exact)
                         * (N_BUCKETS - max_exact)).astype(np.int32)
    large = np.minimum(large, N_BUCKETS - 1)
    return np.where(n < max_exact, n, large)


def _bias_kernel(table_ref, idx_ref, out_ref):
    h = pl.program_id(0)
    idx = idx_ref[0]
    acc = jnp.full(idx.shape, MASKED, _F32)
    for b in range(N_BUCKETS):
        acc = jnp.where(idx == b, table_ref[b, h] * LOG2E, acc)
    out_ref[0, 0] = acc
    prev_key = lax.broadcasted_iota(jnp.int32, idx.shape, 0) < BLOCK
    out_ref[1, 0] = jnp.where(prev_key, MASKED, acc)


def _band_bias(rel_bias):
    ki = np.arange(2 * BLOCK)[:, None]
    qi = np.arange(BLOCK)[None, :]
    dist = qi + BLOCK - ki
    variants = [(d, w // d) for w, d in DIL_GROUPS] + [(1, B_WINDOW - 1)]
    idx = jnp.asarray(np.stack([np.where((dist >= 0) & (dist <= max_steps), _t5_bucket_np(dist * stride), -1)
                                for stride, max_steps in variants]).astype(np.int32))
    n_heads = rel_bias.shape[1]
    last = len(variants) - 1
    return pl.pallas_call(
        _bias_kernel,
        grid=(n_heads,),
        in_specs=[pl.BlockSpec(memory_space=pltpu.SMEM),
                  pl.BlockSpec((1, 2 * BLOCK, BLOCK),
                               lambda h: (jnp.minimum(h // A_HEADS_PER_GROUP, last), 0, 0))],
        out_specs=pl.BlockSpec((2, 1, 2 * BLOCK, BLOCK), lambda h: (0, h, 0, 0)),
        out_shape=jax.ShapeDtypeStruct((2, n_heads, 2 * BLOCK, BLOCK), _F32),
        name="band_bias",
    )(rel_bias.astype(_F32), idx)


def _const_spec(shape):
    nd = len(shape)
    return pl.BlockSpec(shape, lambda *_: (0,) * nd, pipeline_mode=pl.Buffered(1))


def _layer(x, ffn1_norm, ffn1_w_gate, ffn1_w_up, ffn1_w_down, mix_norm, w_in, b_in,
           w_branch_a, w_branch_b, w_out, sinks, rel_bias,
           ffn2_norm, ffn2_w_gate, ffn2_w_up, ffn2_w_down, final_norm):
    batch, seq, d_model = x.shape
    tokens = batch * seq
    tm = TOKEN_TILE
    tiles_per_seq = seq // tm
    n_tiles = tokens // tm
    d_ff = ffn1_w_gate.shape[1]
    a_w = A_HEADS * HEAD_DIM
    scale = HEAD_DIM ** -0.5 * LOG2E

    aq, ak, av = w_in[:, :a_w], w_in[:, a_w:2 * a_w], w_in[:, 2 * a_w:3 * a_w]
    baq, bak, bav = b_in[:a_w], b_in[a_w:2 * a_w], b_in[2 * a_w:3 * a_w]
    off = 3 * a_w
    bq_w, bq_b = w_in[:, off:off + B_OUT], b_in[off:off + B_OUT]
    bkv_w, bkv_b = w_in[:, off + B_OUT:off + B_OUT + 2 * B_KV_W], b_in[off + B_OUT:off + B_OUT + 2 * B_KV_W]
    gate_w, gate_b = w_in[:, off + B_W:], b_in[off + B_W:]
    perm = np.concatenate([np.r_[j * HEAD_DIM:(j + 1) * HEAD_DIM,
                                 (j + 4) * HEAD_DIM:(j + 5) * HEAD_DIM] for j in range(B_Q_HEADS // 2)])
    w_cols, b_cols = [], []
    for gi in range(len(DIL_GROUPS)):
        sl = slice(gi * A_OUT, (gi + 1) * A_OUT)
        w_cols += [aq[:, sl] * scale, ak[:, sl], av[:, sl]]
        b_cols += [baq[sl] * scale, bak[sl], bav[sl]]
    w_cols += [bq_w[:, perm] * scale, bkv_w]
    b_cols += [bq_b[perm] * scale, bkv_b]
    wqkv = jnp.concatenate(w_cols, axis=1).astype(_BF16)
    bqkv = jnp.concatenate(b_cols)[None, :].astype(_F32)
    wb_perm = w_branch_b[perm, :].astype(_BF16)

    row = lambda v: v[None, :].astype(_F32)
    x2 = x.reshape(tokens, d_model)

    bias = _band_bias(rel_bias)
    bias_nat = jnp.concatenate([bias[:, :A_HEADS_PER_GROUP], bias[:, A_HEADS:]], axis=1)
    sink_rows = jnp.repeat((sinks.astype(_F32) * LOG2E).reshape(B_KV_HEADS, B_Q_HEADS // B_KV_HEADS), BLOCK,
                           axis=1)[:, None, :]

    cur = lambda i: jnp.minimum(i, n_tiles - 1)
    prv = lambda i: jnp.maximum(i - 1, 0)
    d1, d2 = DIL_GROUPS[1][1], DIL_GROUPS[2][1]
    n_sub = tm // SUB_TILE
    h1, u2, qkv1, qkv2, yb, o0, l0 = pl.pallas_call(
        functools.partial(_stage1_kernel, tiles_per_seq=tiles_per_seq),
        grid=(n_tiles + 1,),
        in_specs=[pl.BlockSpec((tm, d_model), lambda i: (cur(i), 0)), _const_spec((1, d_model)),
                  _const_spec((d_model, d_ff)), _const_spec((d_model, d_ff)), _const_spec((d_ff, d_model)),
                  _const_spec((1, d_model)), _const_spec((d_model, 3 * GROUP_W + B_W)),
                  _const_spec((1, 3 * GROUP_W + B_W)),
                  _const_spec(bias_nat.shape), _const_spec(sink_rows.shape)],
        out_specs=[pl.BlockSpec((tm, d_model), lambda i: (cur(i), 0)),
                   pl.BlockSpec((tm, d_model), lambda i: (cur(i), 0)),
                   pl.BlockSpec((1, d1, tm // d1, GROUP_W),
                                lambda i: (cur(i) // tiles_per_seq, 0, cur(i) % tiles_per_seq, 0)),
                   pl.BlockSpec((1, d2, tm // d2, GROUP_W),
                                lambda i: (cur(i) // tiles_per_seq, 0, cur(i) % tiles_per_seq, 0)),
                   pl.BlockSpec((tm, B_OUT), lambda i: (prv(i), 0)),
                   pl.BlockSpec((A_OUT // LANES, tm, LANES), lambda i: (0, prv(i), 0)),
                   pl.BlockSpec((A_OUT // LANES, tm, LANES), lambda i: (0, prv(i), 0))],
        out_shape=[jax.ShapeDtypeStruct((tokens, d_model), _F32),
                   jax.ShapeDtypeStruct((tokens, d_model), _BF16),
                   jax.ShapeDtypeStruct((batch, d1, seq // d1, GROUP_W), _BF16),
                   jax.ShapeDtypeStruct((batch, d2, seq // d2, GROUP_W), _BF16),
                   jax.ShapeDtypeStruct((tokens, B_OUT), _BF16),
                   jax.ShapeDtypeStruct((A_OUT // LANES, tokens, LANES), _F32),
                   jax.ShapeDtypeStruct((A_OUT // LANES, tokens, LANES), _F32)],
        scratch_shapes=[pltpu.VMEM((n_sub, 2 * GROUP_W // LANES, SUB_TILE, LANES), _F32),
                        pltpu.VMEM((2, tm, GROUP_W), _BF16), pltpu.VMEM((2, tm, B_W), _BF16),
                        pltpu.VMEM((2, BLOCK, GROUP_W), _BF16), pltpu.VMEM((2, BLOCK, B_W), _BF16),
                        pltpu.VMEM((N_SLOTS, 2 * BLOCK, 2 * BLOCK), _F32),
                        pltpu.VMEM((N_SLOTS, 2 * BLOCK, 2 * BLOCK), _BF16)],
        compiler_params=pltpu.CompilerParams(dimension_semantics=("arbitrary",), vmem_limit_bytes=VMEM_LIMIT),
        name="stage1_ffn_qkv_attn",
    )(x2, row(ffn1_norm), ffn1_w_gate.astype(_BF16), ffn1_w_up.astype(_BF16), ffn1_w_down.astype(_BF16),
      row(mix_norm), wqkv, bqkv, bias_nat, sink_rows)

    params = pltpu.CompilerParams(dimension_semantics=("parallel",), vmem_limit_bytes=VMEM_LIMIT)

    n_heads = bias.shape[1]
    ya = pl.pallas_call(
        _stage2_kernel,
        grid=(batch,),
        in_specs=[pl.BlockSpec((1, d1, seq // d1, GROUP_W), lambda b: (b, 0, 0, 0)),
                  pl.BlockSpec((1, d2, seq // d2, GROUP_W), lambda b: (b, 0, 0, 0)),
                  pl.BlockSpec((A_OUT // LANES, seq, LANES), lambda b: (0, b, 0)),
                  pl.BlockSpec((A_OUT // LANES, seq, LANES), lambda b: (0, b, 0)),
                  pl.BlockSpec((None, n_heads, 2 * BLOCK, BLOCK), lambda b: (0, 0, 0, 0),
                               pipeline_mode=pl.Buffered(1))],
        out_specs=pl.BlockSpec((seq, A_OUT), lambda b: (b, 0)),
        out_shape=jax.ShapeDtypeStruct((tokens, A_OUT), _BF16),
        scratch_shapes=[pltpu.VMEM((A_OUT // LANES, seq, LANES), _F32) for _ in range(3)]
        + [pltpu.VMEM((N_SLOTS, 2 * BLOCK, 2 * BLOCK), _F32), pltpu.VMEM((N_SLOTS, 2 * BLOCK, 2 * BLOCK), _BF16)],
        compiler_params=params,
        name="stage2_dilated_attention",
    )(qkv1, qkv2, o0, l0, bias)

    tile = lambda w: pl.BlockSpec((STAGE3_TILE, w), lambda i: (i, 0))
    out = pl.pallas_call(
        _stage3_kernel,
        grid=(tokens // STAGE3_TILE,),
        in_specs=[tile(d_model), tile(d_model), tile(A_OUT), tile(B_OUT),
                  _const_spec((d_model, 2 * d_model)), _const_spec((1, 2 * d_model)),
                  _const_spec((A_OUT, d_model)), _const_spec((B_OUT, d_model)), _const_spec((d_model, d_model)),
                  _const_spec((1, d_model)),
                  _const_spec((d_model, d_ff)), _const_spec((d_model, d_ff)), _const_spec((d_ff, d_model)),
                  _const_spec((1, d_model))],
        out_specs=tile(d_model),
        out_shape=jax.ShapeDtypeStruct((tokens, d_model), _F32),
        compiler_params=params,
        name="stage3_merge_ffn",
    )(h1, u2, ya, yb, gate_w.astype(_BF16), row(gate_b), w_branch_a.astype(_BF16), wb_perm,
      w_out.astype(_BF16), row(ffn2_norm), ffn2_w_gate.astype(_BF16), ffn2_w_up.astype(_BF16),
      ffn2_w_down.astype(_BF16), row(final_norm))
    return out.reshape(batch, seq, d_model)


def kernel(x, ffn1_norm, ffn1_w_gate, ffn1_w_up, ffn1_w_down, mix_norm, w_in, b_in, w_branch_a, w_branch_b,
           w_out, sinks, rel_bias, ffn2_norm, ffn2_w_gate, ffn2_w_up, ffn2_w_down, final_norm):
    assert ffn1_norm.shape[0] == 1, "single-layer kernel"
    return _layer(x, ffn1_norm[0], ffn1_w_gate[0], ffn1_w_up[0], ffn1_w_down[0], mix_norm[0], w_in[0], b_in[0],
                  w_branch_a[0], w_branch_b[0], w_out[0], sinks[0], rel_bias,
                  ffn2_norm[0], ffn2_w_gate[0], ffn2_w_up[0], ffn2_w_down[0], final_norm)
```
